```python
import numpy as np
import jax, jax.numpy as jnp
from jax import lax

D_MODEL = 1024
BATCH = 16
SEQ = 4096
DEPTH = 2

N_BRANCH = 4
BRANCH_WIDTH = D_MODEL // N_BRANCH
HEAD_DIM = 64
NSA_HEADS = BRANCH_WIDTH // HEAD_DIM
NSA_CMP_LEN = 32
NSA_CMP_STRIDE = 16
NSA_CMP_HIDDEN = 2 * HEAD_DIM
NSA_SEL_LEN = 64
NSA_N_SEL = 16
NSA_WINDOW = 512
NSA_FORCE_BONUS = 1.0e4
POOL_WINDOWS = (2, 4, 8, 16)
POOL_GROUP = BRANCH_WIDTH // len(POOL_WINDOWS)
CONV_WIDTH = 3
FOX_HEADS = BRANCH_WIDTH // HEAD_DIM
ROPE_THETA = 500000.0
ROPE_DIM = HEAD_DIM // 4
Q_BLOCK = 128
NORM_EPS = 1e-6
MASK_VALUE = -1e30

IN_SPLITS = (
    NSA_HEADS * HEAD_DIM,
    6 * HEAD_DIM,
    3 * NSA_HEADS,
    BRANCH_WIDTH,
    3 * BRANCH_WIDTH,
    3 * BRANCH_WIDTH,
    FOX_HEADS,
    N_BRANCH * BRANCH_WIDTH,
    N_BRANCH * D_MODEL,
)
D_IN = sum(IN_SPLITS)

kernel_name = "hybrid_nsa_pool_conv_fox_parallel"


def rmsnorm(x, g):
    xf = x.astype(jnp.float32)
    y = xf * lax.rsqrt(jnp.mean(xf * xf, axis=-1, keepdims=True) + NORM_EPS)
    return (y * g.astype(jnp.float32)).astype(x.dtype)


def rope(x, pos):
    half = ROPE_DIM // 2
    inv = ROPE_THETA ** (-jnp.arange(half, dtype=jnp.float32) / half)
    ang = pos.astype(jnp.float32)[:, None] * inv[None, :]
    cos = jnp.cos(ang)[None, :, None, :].astype(x.dtype)
    sin = jnp.sin(ang)[None, :, None, :].astype(x.dtype)
    x1, x2, rest = x[..., :half], x[..., half:ROPE_DIM], x[..., ROPE_DIM:]
    return jnp.concatenate([x1 * cos - x2 * sin, x2 * cos + x1 * sin, rest], axis=-1)


def masked_softmax(s, mask):
    s = jnp.where(mask, s.astype(jnp.float32), MASK_VALUE)
    p = jax.nn.softmax(s, axis=-1)
    return jnp.where(mask, p, 0.0)


def to_blocks(a):
    B, T = a.shape[:2]
    a = a.reshape((B, T // Q_BLOCK, Q_BLOCK) + a.shape[2:])
    return jnp.moveaxis(a, 1, 0)


def from_blocks(a):
    a = jnp.moveaxis(a, 0, 1)
    return a.reshape((a.shape[0], a.shape[1] * a.shape[2]) + a.shape[3:])


def nsa_compress(k, pos_emb, w1, b1, w2, b2):
    B, T, hd = k.shape
    nc = (T - NSA_CMP_LEN) // NSA_CMP_STRIDE + 1
    idx = np.arange(nc)[:, None] * NSA_CMP_STRIDE + np.arange(NSA_CMP_LEN)[None, :]
    blocks = k[:, idx] + pos_emb
    flat = blocks.reshape(B, nc, NSA_CMP_LEN * hd)
    return jax.nn.silu(flat @ w1 + b1) @ w2 + b2


def nsa_mixer(q, kv, gate_logits, cmp_pos, cmp_w1, cmp_b1, cmp_w2, cmp_b2):
    B, T = q.shape[:2]
    pos = jnp.arange(T)
    q = rope(q.reshape(B, T, NSA_HEADS, HEAD_DIM), pos) * (HEAD_DIM ** -0.5)
    k_c, v_c, k_s, v_s, k_w, v_w = jnp.split(kv, 6, axis=-1)
    k_s = rope(k_s[:, :, None], pos)[:, :, 0]
    k_w = rope(k_w[:, :, None], pos)[:, :, 0]
    kc = nsa_compress(k_c, cmp_pos[0], cmp_w1[0], cmp_b1[0], cmp_w2[0], cmp_b2[0])
    vc = nsa_compress(v_c, cmp_pos[1], cmp_w1[1], cmp_b1[1], cmp_w2[1], cmp_b2[1])
    nc = kc.shape[1]
    cmp_end_np = np.arange(nc) * NSA_CMP_STRIDE + NSA_CMP_LEN - 1
    cmp_end = jnp.asarray(cmp_end_np)
    kc = rope(kc[:, :, None], cmp_end)[:, :, 0]
    n_blk = T // NSA_SEL_LEN
    n_sel = min(NSA_N_SEL, n_blk)
    ci = np.arange(nc)[:, None] * NSA_CMP_STRIDE
    sj = np.arange(n_blk)[None, :] * NSA_SEL_LEN
    overlap = jnp.asarray((ci < sj + NSA_SEL_LEN) & (ci + NSA_CMP_LEN > sj), jnp.float32)
    k_blk = k_s.reshape(B, n_blk, NSA_SEL_LEN, HEAD_DIM)
    v_blk = v_s.reshape(B, n_blk, NSA_SEL_LEN, HEAD_DIM)
    pad = ((0, 0), (NSA_WINDOW, 0), (0, 0))
    k_wp, v_wp = jnp.pad(k_w, pad), jnp.pad(v_w, pad)
    blk_ids = jnp.arange(n_blk)

    def one_block(args):
        qi, q_blk = args
        t = qi * Q_BLOCK + jnp.arange(Q_BLOCK)
        s = jnp.einsum('bqhd,bnd->bhqn', q_blk, kc)
        p_c = masked_softmax(s, (cmp_end[None, :] <= t[:, None])[None, None])
        o_cmp = jnp.einsum('bhqn,bnd->bqhd', p_c.astype(vc.dtype), vc)
        imp = jnp.einsum('bhqn,nj->bqj', p_c, overlap)
        jt = (t // NSA_SEL_LEN)[:, None]
        j = blk_ids[None, :]
        forced = (j == 0) | (j == jt) | (j == jt - 1)
        valid = j * NSA_SEL_LEN <= t[:, None]
        imp = jnp.where(valid, jnp.where(forced, imp + NSA_FORCE_BONUS, imp), MASK_VALUE)
        _, sel = lax.top_k(imp, n_sel)
        ks = jax.vmap(lambda kb, ib: kb[ib])(k_blk, sel)
        vs = jax.vmap(lambda vb, ib: vb[ib])(v_blk, sel)
        kpos = sel[..., None] * NSA_SEL_LEN + jnp.arange(NSA_SEL_LEN)
        m_s = (kpos <= t[None, :, None, None]).reshape(B, 1, Q_BLOCK, n_sel * NSA_SEL_LEN)
        s = jnp.einsum('bqhd,bqnkd->bhqnk', q_blk, ks).reshape(B, NSA_HEADS, Q_BLOCK, n_sel * NSA_SEL_LEN)
        p_s = masked_softmax(s, m_s).reshape(B, NSA_HEADS, Q_BLOCK, n_sel, NSA_SEL_LEN)
        o_slc = jnp.einsum('bhqnk,bqnkd->bqhd', p_s.astype(vs.dtype), vs)
        kw = lax.dynamic_slice_in_dim(k_wp, qi * Q_BLOCK, Q_BLOCK + NSA_WINDOW, axis=1)
        vw = lax.dynamic_slice_in_dim(v_wp, qi * Q_BLOCK, Q_BLOCK + NSA_WINDOW, axis=1)
        wpos = qi * Q_BLOCK - NSA_WINDOW + jnp.arange(Q_BLOCK + NSA_WINDOW)
        m_w = (wpos[None, :] <= t[:, None]) & (wpos[None, :] > t[:, None] - NSA_WINDOW) & (wpos[None, :] >= 0)
        s = jnp.einsum('bqhd,bkd->bhqk', q_blk, kw)
        p_w = masked_softmax(s, m_w[None, None])
        o_win = jnp.einsum('bhqk,bkd->bqhd', p_w.astype(vw.dtype), vw)
        return o_cmp, o_slc, o_win

    o_cmp, o_slc, o_win = lax.map(one_block, (jnp.arange(T // Q_BLOCK), to_blocks(q)))
    g = jax.nn.sigmoid(gate_logits.reshape(B, T, 3, NSA_HEADS))[..., None]
    o = (g[:, :, 0] * from_blocks(o_cmp) + g[:, :, 1] * from_blocks(o_slc)
         + g[:, :, 2] * from_blocks(o_win))
    return o.reshape(B, T, NSA_HEADS * HEAD_DIM)


def pool_mixer(u, pool_w, pool_scale):
    B, T, C = u.shape
    cs = jnp.concatenate([jnp.zeros((B, 1, C), jnp.float32),
                          jnp.cumsum(u.astype(jnp.float32), axis=1)], axis=1)
    outs = []
    for gi, w in enumerate(POOL_WINDOWS):
        sl = slice(gi * POOL_GROUP, (gi + 1) * POOL_GROUP)
        csg = cs[..., sl]
        lag = jnp.concatenate([jnp.zeros((B, w - 1, POOL_GROUP), jnp.float32),
                               csg[:, :T + 1 - w]], axis=1)
        cnt = jnp.minimum(jnp.arange(1, T + 1), w).astype(jnp.float32)[None, :, None]
        outs.append(((csg[:, 1:] - lag) / cnt).astype(u.dtype) - u[..., sl])
    pooled = jnp.stack(outs, axis=2)
    mixed = jnp.einsum('btgc,gcd->btgd', pooled, pool_w).reshape(B, T, C)
    return mixed * pool_scale


def conv_mixer(xin, b_gate, c_gate, conv_w):
    T = xin.shape[1]
    u = jnp.pad(c_gate * xin, ((0, 0), (CONV_WIDTH - 1, 0), (0, 0)))
    y = sum(u[:, k:k + T] * conv_w[k] for k in range(CONV_WIDTH))
    return b_gate * y


def fox_mixer(q, k, v, f_logit, f_bias):
    B, T = q.shape[:2]
    q = q.reshape(B, T, FOX_HEADS, HEAD_DIM) * (HEAD_DIM ** -0.5)
    k = k.reshape(B, T, FOX_HEADS, HEAD_DIM)
    v = v.reshape(B, T, FOX_HEADS, HEAD_DIM)
    logf = jax.nn.log_sigmoid(f_logit.astype(jnp.float32) + f_bias.astype(jnp.float32))
    c = jnp.cumsum(logf, axis=1)
    c_keys = jnp.transpose(c, (0, 2, 1))
    kpos = jnp.arange(T)

    def one_block(args):
        qi, q_blk, c_blk = args
        t = qi * Q_BLOCK + jnp.arange(Q_BLOCK)
        s = jnp.einsum('bqhd,bkhd->bhqk', q_blk, k).astype(jnp.float32)
        s = s + jnp.transpose(c_blk, (0, 2, 1))[..., None] - c_keys[:, :, None, :]
        p = masked_softmax(s, (kpos[None, :] <= t[:, None])[None, None])
        return jnp.einsum('bhqk,bkhd->bqhd', p.astype(v.dtype), v)

    o = lax.map(one_block, (jnp.arange(T // Q_BLOCK), to_blocks(q), to_blocks(c)))
    return from_blocks(o).reshape(B, T, FOX_HEADS * HEAD_DIM)


def hybrid_layer(x, norm_g, w_in, fox_f_bias, cmp_pos, cmp_w1, cmp_b1, cmp_w2, cmp_b2,
                 pool_w, pool_scale, conv_w, w_branch, w_out):
    B, T, _ = x.shape
    h = rmsnorm(x, norm_g)
    proj = h @ w_in
    split_at = np.cumsum(IN_SPLITS)[:-1].tolist()
    (nsa_q, nsa_kv, nsa_g, pool_in, conv_in, fox_qkv, fox_f,
     gate_in, merge_in) = jnp.split(proj, split_at, axis=-1)
    o_nsa = nsa_mixer(nsa_q, nsa_kv, nsa_g, cmp_pos, cmp_w1, cmp_b1, cmp_w2, cmp_b2)
    o_pool = pool_mixer(pool_in, pool_w, pool_scale)
    c_x, c_b, c_c = jnp.split(conv_in, 3, axis=-1)
    o_conv = conv_mixer(c_x, c_b, c_c, conv_w)
    f_q, f_k, f_v = jnp.split(fox_qkv, 3, axis=-1)
    o_fox = fox_mixer(f_q, f_k, f_v, fox_f, fox_f_bias)
    gates = gate_in.reshape(B, T, N_BRANCH, BRANCH_WIDTH)
    merge = merge_in.reshape(B, T, N_BRANCH, D_MODEL)
    acc = None
    for i, o in enumerate((o_nsa, o_pool, o_conv, o_fox)):
        branch = (o * jax.nn.silu(gates[:, :, i])) @ w_branch[i]
        term = jax.nn.sigmoid(merge[:, :, i]) * branch
        acc = term if acc is None else acc + term
    return x + acc @ w_out


def setup_inputs(seed: int = 0) -> dict:
    key = jax.random.key(seed)
    ks = jax.random.split(key, 20)
    f32 = jnp.float32
    nrm = lambda k, shape, scale: jax.random.normal(k, shape, f32) * scale
    cmp_in = NSA_CMP_LEN * HEAD_DIM
    return {
        "x": nrm(ks[0], (BATCH, SEQ, D_MODEL), 1.0),
        "norm_g": 1.0 + nrm(ks[1], (DEPTH, D_MODEL), 0.05),
        "w_in": nrm(ks[2], (DEPTH, D_MODEL, D_IN), D_MODEL ** -0.5),
        "fox_f_bias": 3.0 + nrm(ks[3], (DEPTH, FOX_HEADS), 0.5),
        "cmp_pos": nrm(ks[4], (DEPTH, 2, NSA_CMP_LEN, HEAD_DIM), 0.1),
        "cmp_w1": nrm(ks[5], (DEPTH, 2, cmp_in, NSA_CMP_HIDDEN), cmp_in ** -0.5),
        "cmp_b1": nrm(ks[6], (DEPTH, 2, NSA_CMP_HIDDEN), 0.02),
        "cmp_w2": nrm(ks[7], (DEPTH, 2, NSA_CMP_HIDDEN, HEAD_DIM), NSA_CMP_HIDDEN ** -0.5),
        "cmp_b2": nrm(ks[8], (DEPTH, 2, HEAD_DIM), 0.02),
        "pool_w": nrm(ks[9], (DEPTH, len(POOL_WINDOWS), POOL_GROUP, POOL_GROUP), POOL_GROUP ** -0.5),
        "pool_scale": 1.0 + nrm(ks[10], (DEPTH, BRANCH_WIDTH), 0.1),
        "conv_w": nrm(ks[11], (DEPTH, CONV_WIDTH, BRANCH_WIDTH), CONV_WIDTH ** -0.5),
        "w_branch": nrm(ks[12], (DEPTH, N_BRANCH, BRANCH_WIDTH, D_MODEL), BRANCH_WIDTH ** -0.5),
        "w_out": nrm(ks[13], (DEPTH, D_MODEL, D_MODEL), D_MODEL ** -0.5),
        "final_norm_g": 1.0 + nrm(ks[14], (D_MODEL,), 0.05),
    }


def reference(x, norm_g, w_in, fox_f_bias, cmp_pos, cmp_w1, cmp_b1, cmp_w2, cmp_b2,
              pool_w, pool_scale, conv_w, w_branch, w_out, final_norm_g):
    for l in range(DEPTH):
        x = hybrid_layer(x, norm_g[l], w_in[l], fox_f_bias[l], cmp_pos[l], cmp_w1[l],
                         cmp_b1[l], cmp_w2[l], cmp_b2[l], pool_w[l], pool_scale[l],
                         conv_w[l], w_branch[l], w_out[l])
    return rmsnorm(x, final_norm_g)
```

```python
import functools

import numpy as np
import jax
import jax.numpy as jnp
from jax import lax
from jax.experimental import pallas as pl
from jax.experimental.pallas import tpu as pltpu

F32 = jnp.float32
BF16 = jnp.bfloat16

D_MODEL = 1024
N_BRANCH = 4
BRANCH_WIDTH = 256
HEAD_DIM = 64
N_HEADS = 4
CMP_LEN = 32
CMP_STRIDE = 16
CMP_HIDDEN = 128
SEL_LEN = 64
N_SEL = 16
WINDOW = 512
FORCE_BONUS = 1.0e4
POOL_WINDOWS = (2, 4, 8, 16)
ROPE_THETA = 500000.0
ROPE_HALF = 8
NORM_EPS = 1e-6
MASK_VALUE = -1e30

_OFF = dict(nsa_q=0, k_c=256, v_c=320, k_s=384, v_s=448, k_w=512, v_w=576, nsa_g=640, pool=652,
            conv=908, fox=1676, fox_f=2444, gate=2448, merge=3472, end=7568)

_P_ROPE = (0, 512)
_P_VSS = (512, 640)
_P_VWW = (640, 768)
_P_KVC = (768, 896)
_P_MISC = (896, 1024)
_P_FOX = (1024, 1792)
_P_POOL = (1792, 2048)
_P_CONV = (2048, 2816)
_P_COLS = 2816

TM_PROJ = 512
TQ_NSA = 128
TK_NSA = 512
TQ_FOX = 512
TM_MERGE = 512
POOL_HALO = 16
CONV_HALO = 8
VMEM_LIMIT = 56 * 1024 * 1024


def _nt_dot(a, b, precision=None):
    return lax.dot_general(a, b, (((1,), (1,)), ((), ())), preferred_element_type=F32,
                           precision=precision)


def _dot(a, b, precision=None):
    return jnp.dot(a, b, preferred_element_type=F32, precision=precision)


def _sigmoid(x):
    return 1.0 / (1.0 + jnp.exp(-x))


def _rmsnorm(x, g):
    return x * lax.rsqrt(jnp.mean(x * x, axis=-1, keepdims=True) + NORM_EPS) * g


def _flash_update(s, m, l, acc, v):
    m_new = jnp.maximum(m, jnp.max(s, axis=-1, keepdims=True))
    alpha = jnp.exp(m - m_new)
    p = jnp.exp(s - m_new)
    l = alpha * l + jnp.sum(p, axis=-1, keepdims=True)
    acc = alpha * acc + _dot(p.astype(BF16), v)
    return m_new, l, acc


def _proj_kernel(x_ref, g_ref, w_ref, cc_ref, ss_ref, fb_ref, cw_ref, pw_ref, ps_ref, tri_ref,
                 q_ref, ksw_ref, kws_ref, vss_ref, vww_ref, kvc_ref, misc_ref, ct_ref, fox_ref,
                 opool_ref, oconv_ref, pext, cext, ccarry, *, tm):
    ti = pl.program_id(1)
    x = x_ref[...]
    h = _rmsnorm(x, g_ref[...]).astype(BF16)

    def seg(lo, hi):
        return _dot(h, w_ref[:, lo:hi])

    lane = lax.broadcasted_iota(jnp.int32, (tm, 128), 1)
    row = lax.broadcasted_iota(jnp.int32, (tm, 128), 0) + ti * tm
    first_half = (lane % HEAD_DIM) < ROPE_HALF
    cc = cc_ref[...]
    ss = ss_ref[...]

    def rope(xc):
        partner = jnp.where(first_half, pltpu.roll(xc, 128 - ROPE_HALF, 1), pltpu.roll(xc, ROPE_HALF, 1))
        return xc * cc + partner * ss

    r0 = _P_ROPE[0]
    q_ref[:, 0:128] = (rope(seg(r0, r0 + 128)) * (HEAD_DIM ** -0.5)).astype(BF16)
    q_ref[:, 128:256] = (rope(seg(r0 + 128, r0 + 256)) * (HEAD_DIM ** -0.5)).astype(BF16)
    ksw_ref[...] = rope(seg(r0 + 256, r0 + 384)).astype(BF16)
    kws_ref[...] = rope(seg(r0 + 384, r0 + 512)).astype(BF16)
    vss_ref[...] = seg(*_P_VSS).astype(BF16)
    vww_ref[...] = seg(*_P_VWW).astype(BF16)
    kvc_ref[...] = seg(*_P_KVC)

    z = seg(*_P_MISC)
    zb = z + fb_ref[...]
    logf = jnp.minimum(zb, 0.0) - jnp.log(1.0 + jnp.exp(-jnp.abs(zb)))
    is_f = (lane >= 12) & (lane < 16)
    logf = jnp.where(is_f, logf, 0.0)

    @pl.when(ti == 0)
    def _():
        ccarry[...] = jnp.zeros_like(ccarry)
        pext[0:POOL_HALO, :] = jnp.zeros((POOL_HALO, BRANCH_WIDTH), F32)
        cext[0:CONV_HALO, :] = jnp.zeros((CONV_HALO, BRANCH_WIDTH), F32)

    csum = _dot(tri_ref[...], logf, precision=lax.Precision.HIGHEST) + ccarry[0:1, :]
    ccarry[0:1, :] = csum[tm - 1:tm, :]
    misc = jnp.where(lane < 12, _sigmoid(z), csum)
    misc_ref[...] = misc
    ct_ref[...] = jnp.transpose(misc)[8:16, :]

    fo = seg(*_P_FOX)
    fox_ref[:, 0:256] = (fo[:, 0:256] * (HEAD_DIM ** -0.5)).astype(BF16)
    fox_ref[:, 256:768] = fo[:, 256:768].astype(BF16)

    pext[POOL_HALO:POOL_HALO + tm, :] = seg(*_P_POOL)

    def pld(k, c):
        return pext[pl.ds(POOL_HALO - k, tm), c * 128:(c + 1) * 128]

    left = lane < 64
    rowp1 = (row + 1).astype(F32)

    def cnt(w):
        return jnp.minimum(rowp1, float(w))

    e0 = pld(0, 0)
    s2 = e0 + pld(1, 0)
    s4 = s2 + pld(2, 0) + pld(3, 0)
    p0 = jnp.where(left, s2 / cnt(2), s4 / cnt(4)) - e0
    f0 = pld(0, 1)
    s8 = f0
    for k in range(1, 8):
        s8 = s8 + pld(k, 1)
    s16 = s8
    for k in range(8, 16):
        s16 = s16 + pld(k, 1)
    p1 = jnp.where(left, s8 / cnt(8), s16 / cnt(16)) - f0
    pw = pw_ref[...]
    mixed = _dot(p0.astype(BF16), pw[0:128, :]) + _dot(p1.astype(BF16), pw[128:256, :])
    opool_ref[...] = mixed * ps_ref[...]
    pext[0:POOL_HALO, :] = pext[tm:tm + POOL_HALO, :]

    cv = seg(*_P_CONV)
    u = cv[:, 512:768] * cv[:, 0:256]
    cext[CONV_HALO:CONV_HALO + tm, :] = u
    y = (cext[pl.ds(CONV_HALO - 2, tm), :] * cw_ref[0:1, :]
         + cext[pl.ds(CONV_HALO - 1, tm), :] * cw_ref[1:2, :]
         + u * cw_ref[2:3, :])
    oconv_ref[...] = cv[:, 256:512] * y
    cext[0:CONV_HALO, :] = cext[tm:tm + CONV_HALO, :]


def _proj_call(x, g, w1, cc, ss, fb, cw, pw, ps, tri):
    B, T, _ = x.shape
    tm = TM_PROJ
    nt = T // tm
    tok = lambda width: pl.BlockSpec((None, tm, width), lambda b, t: (b, t, 0))
    const = lambda shape: pl.BlockSpec(shape, lambda b, t: (0,) * len(shape))
    out_shape = [
        jax.ShapeDtypeStruct((B, T, 256), BF16),
        jax.ShapeDtypeStruct((B, T, 128), BF16),
        jax.ShapeDtypeStruct((B, T, 128), BF16),
        jax.ShapeDtypeStruct((B, T, 128), BF16),
        jax.ShapeDtypeStruct((B, T, 128), BF16),
        jax.ShapeDtypeStruct((B, T, 128), F32),
        jax.ShapeDtypeStruct((B, T, 128), F32),
        jax.ShapeDtypeStruct((B, nt, 8, tm), F32),
        jax.ShapeDtypeStruct((B, T, 768), BF16),
        jax.ShapeDtypeStruct((B, T, 256), F32),
        jax.ShapeDtypeStruct((B, T, 256), F32),
    ]
    out_specs = [tok(256), tok(128), tok(128), tok(128), tok(128), tok(128), tok(128),
                 pl.BlockSpec((None, None, 8, tm), lambda b, t: (b, t, 0, 0)),
                 tok(768), tok(256), tok(256)]
    in_specs = [tok(D_MODEL), const((1, D_MODEL)), const((D_MODEL, _P_COLS)),
                pl.BlockSpec((tm, 128), lambda b, t: (t, 0)), pl.BlockSpec((tm, 128), lambda b, t: (t, 0)),
                const((1, 128)), const((3, 256)), const((256, 256)), const((1, 256)), const((tm, tm))]
    return pl.pallas_call(
        functools.partial(_proj_kernel, tm=tm),
        grid=(B, nt),
        in_specs=in_specs,
        out_specs=out_specs,
        out_shape=out_shape,
        scratch_shapes=[pltpu.VMEM((tm + POOL_HALO, 256), F32), pltpu.VMEM((tm + CONV_HALO, 256), F32),
                        pltpu.VMEM((8, 128), F32)],
        compiler_params=pltpu.CompilerParams(dimension_semantics=("arbitrary", "arbitrary"),
                                             vmem_limit_bytes=VMEM_LIMIT),
        name="proj",
    )(x, g, w1, cc, ss, fb, cw, pw, ps, tri)


def _compress_kernel(x_ref, pt_ref, pb_ref, wt_ref, wb_ref, b1_ref, w2_ref, b2_ref, cc_ref, ss_ref,
                     kc_ref, vc_ref, *, rows):
    x = x_ref[...]
    a = _dot((x + pt_ref[...]).astype(BF16), wt_ref[...])
    b = _dot((x + pb_ref[...]).astype(BF16), wb_ref[...])
    hid = a + pltpu.roll(b, rows - 1, 0) + b1_ref[...]
    act = hid * _sigmoid(hid)
    out = _dot(act.astype(BF16), w2_ref[...]) + b2_ref[...]
    kc_ref[...] = (out[:, 0:128] * cc_ref[...] + out[:, 128:256] * ss_ref[...]).astype(BF16)
    vc_ref[...] = out[:, 256:384].astype(BF16)


def _compress_call(x16, pt, pb, wt, wb, b1, w2, b2, cc, ss):
    B, rows, width = x16.shape
    const = lambda shape: pl.BlockSpec(shape, lambda b: (0,) * len(shape))
    return pl.pallas_call(
        functools.partial(_compress_kernel, rows=rows),
        grid=(B,),
        in_specs=[pl.BlockSpec((None, rows, width), lambda b: (b, 0, 0)),
                  const((1, width)), const((1, width)), const((width, 256)), const((width, 256)),
                  const((1, 256)), const((256, 384)), const((1, 384)), const((rows, 128)), const((rows, 128))],
        out_specs=[pl.BlockSpec((None, rows, 128), lambda b: (b, 0, 0)),
                   pl.BlockSpec((None, rows, 128), lambda b: (b, 0, 0))],
        out_shape=[jax.ShapeDtypeStruct((B, rows, 128), BF16), jax.ShapeDtypeStruct((B, rows, 128), BF16)],
        compiler_params=pltpu.CompilerParams(dimension_semantics=("arbitrary",),
                                             vmem_limit_bytes=VMEM_LIMIT),
        name="compress",
    )(x16, pt, pb, wt, wb, b1, w2, b2, cc, ss)


def _nsa_kernel(q_ref, ksw_ref, kws_ref, vss_ref, vww_ref, kc_ref, vc_ref, misc_ref, ovt_ref,
                o_ref, key_ref, mask_ref, *, seq, nblk, ncmp):
    qi = pl.program_id(1)
    tq, tk = TQ_NSA, TK_NSA
    lane = lax.broadcasted_iota(jnp.int32, (tq, 128), 1)
    left = lane < 64
    t_col = qi * tq + lax.broadcasted_iota(jnp.int32, (tq, 1), 0)

    q = q_ref[...]
    zero = jnp.zeros((tq, 128), BF16)
    qa, qb = q[:, 0:128], q[:, 128:256]
    q_even = jnp.concatenate([jnp.where(left, qa, zero), jnp.where(left, qb, zero)], axis=0)
    q_odd = jnp.concatenate([jnp.where(left, zero, qa), jnp.where(left, zero, qb)], axis=0)
    q_all = jnp.concatenate([q_even, q_odd], axis=0)

    sc = _nt_dot(q_all, kc_ref[...]).reshape(4, tq, ncmp)
    n_idx = lax.broadcasted_iota(jnp.int32, (tq, ncmp), 1)
    cmask = ((n_idx * CMP_STRIDE + (CMP_LEN - 1)) <= t_col)[None]
    sc = jnp.where(cmask, sc, MASK_VALUE)
    mc = jnp.max(sc, axis=-1, keepdims=True)
    pc = jnp.where(cmask, jnp.exp(sc - mc), 0.0)
    lc = jnp.sum(pc, axis=-1, keepdims=True)
    pc = pc / jnp.where(lc > 0.0, lc, 1.0)
    o_cmp = _dot(pc.reshape(4 * tq, ncmp).astype(BF16), vc_ref[...])
    psum = (pc[0] + pc[1]) + (pc[2] + pc[3])

    imp = _nt_dot(ovt_ref[...], psum, precision=lax.Precision.HIGHEST)
    jb = lax.broadcasted_iota(jnp.int32, (nblk, tq), 0)
    tt = qi * tq + lax.broadcasted_iota(jnp.int32, (nblk, tq), 1)
    jt = tt // SEL_LEN
    forced = (jb == 0) | (jb == jt) | (jb == jt - 1)
    valid = jb * SEL_LEN <= tt
    key = jnp.where(valid, jnp.where(forced, imp + FORCE_BONUS, imp), MASK_VALUE)
    key_ref[...] = key
    rank = jnp.zeros((nblk, tq), F32)
    for jp in range(nblk):
        other = key_ref[pl.ds(jp, 1), :]
        ahead = (other > key) | ((other == key) & (jp < jb))
        rank = rank + jnp.where(ahead, 1.0, 0.0)
    sel_t = jnp.where(rank < float(min(N_SEL, nblk)), 1.0, 0.0)
    if nblk < 128:
        sel_t = jnp.concatenate([sel_t, jnp.zeros((128 - nblk, tq), F32)], axis=0)
    sel = jnp.transpose(sel_t).astype(BF16)

    n_kt = qi // (tk // tq) + 1
    for kt in range(seq // tk):
        @pl.when(kt < n_kt)
        def _(kt=kt):
            jj = lax.broadcasted_iota(jnp.int32, (128, tk), 0)
            kk = lax.broadcasted_iota(jnp.int32, (128, tk), 1)
            expand = jnp.where(jj == kt * (tk // SEL_LEN) + kk // SEL_LEN, 1.0, 0.0).astype(BF16)
            mask_ref[kt] = _dot(sel, expand).astype(BF16)

    def slc_step(kt, carry):
        start = pl.multiple_of(kt * tk, tk)
        s = jnp.concatenate([_nt_dot(q_even, ksw_ref[pl.ds(start, tk), :]),
                             _nt_dot(q_odd, kws_ref[pl.ds(start, tk), :])], axis=0)
        kpos = start + lax.broadcasted_iota(jnp.int32, (tq, tk), 1)
        ok = (mask_ref[kt] > 0.5) & (kpos <= t_col)
        s = jnp.where(ok[None], s.reshape(4, tq, tk), MASK_VALUE).reshape(4 * tq, tk)
        return _flash_update(s, *carry, vss_ref[pl.ds(start, tk), :])

    init = (jnp.full((4 * tq, 1), MASK_VALUE, F32), jnp.zeros((4 * tq, 1), F32), jnp.zeros((4 * tq, 128), F32))
    _, l_s, acc_s = lax.fori_loop(0, n_kt, slc_step, init)
    o_slc = acc_s / l_s

    span = tq + WINDOW
    ws = pl.multiple_of(jnp.maximum(qi * tq - WINDOW, 0), tq)
    s = jnp.concatenate([_nt_dot(q_even, kws_ref[pl.ds(ws, span), :]),
                         _nt_dot(q_odd, ksw_ref[pl.ds(ws, span), :])], axis=0)
    kpos = ws + lax.broadcasted_iota(jnp.int32, (tq, span), 1)
    ok = (kpos <= t_col) & (kpos > t_col - WINDOW)
    s = jnp.where(ok[None], s.reshape(4, tq, span), MASK_VALUE).reshape(4 * tq, span)
    m_w = jnp.max(s, axis=-1, keepdims=True)
    p_w = jnp.exp(s - m_w)
    l_w = jnp.sum(p_w, axis=-1, keepdims=True)
    o_win = _dot(p_w.astype(BF16), vww_ref[pl.ds(ws, span), :]) / l_w

    g = misc_ref[...]

    def gated(h, slot):
        r = slice(slot * tq, (slot + 1) * tq)
        return (g[:, h:h + 1] * o_cmp[r] + g[:, 4 + h:5 + h] * o_slc[r] + g[:, 8 + h:9 + h] * o_win[r])

    o_ref[:, 0:128] = jnp.where(left, gated(0, 0), gated(1, 2))
    o_ref[:, 128:256] = jnp.where(left, gated(2, 1), gated(3, 3))


def _nsa_call(q, ksw, kws, vss, vww, kc, vc, misc, ovt):
    B, T, _ = q.shape
    nblk, ncmp = ovt.shape
    tq = TQ_NSA
    full = lambda width: pl.BlockSpec((None, T, width), lambda b, i: (b, 0, 0))
    return pl.pallas_call(
        functools.partial(_nsa_kernel, seq=T, nblk=nblk, ncmp=ncmp),
        grid=(B, T // tq),
        in_specs=[pl.BlockSpec((None, tq, 256), lambda b, i: (b, i, 0)),
                  full(128), full(128), full(128), full(128),
                  pl.BlockSpec((None, ncmp, 128), lambda b, i: (b, 0, 0)),
                  pl.BlockSpec((None, ncmp, 128), lambda b, i: (b, 0, 0)),
                  pl.BlockSpec((None, tq, 128), lambda b, i: (b, i, 0)),
                  pl.BlockSpec((nblk, ncmp), lambda b, i: (0, 0))],
        out_specs=pl.BlockSpec((None, tq, 256), lambda b, i: (b, i, 0)),
        out_shape=jax.ShapeDtypeStruct((B, T, 256), F32),
        scratch_shapes=[pltpu.VMEM((nblk, tq), F32), pltpu.VMEM((T // TK_NSA, tq, TK_NSA), BF16)],
        compiler_params=pltpu.CompilerParams(dimension_semantics=("arbitrary", "arbitrary"),
                                             vmem_limit_bytes=VMEM_LIMIT),
        name="nsa",
    )(q, ksw, kws, vss, vww, kc, vc, misc, ovt)


def _fox_kernel(q_ref, k_ref, v_ref, ct_ref, o_ref):
    qi = pl.program_id(1)
    tq = tk = TQ_FOX
    lane = lax.broadcasted_iota(jnp.int32, (tq, 128), 1)
    left = lane < 64
    zero = jnp.zeros((tq, 128), BF16)
    t_col = qi * tq + lax.broadcasted_iota(jnp.int32, (tq, 1), 0)

    for pair in range(2):
        cols = slice(pair * 128, (pair + 1) * 128)
        qp = q_ref[:, cols]
        outs = []
        for half in range(2):
            head = 2 * pair + half
            qh = jnp.where(left, qp, zero) if half == 0 else jnp.where(left, zero, qp)

            def scores(kt, qh=qh, head=head, cols=cols):
                start = pl.multiple_of(kt * tk, tk)
                s = _nt_dot(qh, k_ref[pl.ds(start, tk), cols])
                return s - ct_ref[kt][4 + head:5 + head, :], start

            def full_step(kt, carry, scores=scores, cols=cols):
                s, start = scores(kt)
                return _flash_update(s, *carry, v_ref[pl.ds(start, tk), cols])

            init = (jnp.full((tq, 1), MASK_VALUE, F32), jnp.zeros((tq, 1), F32), jnp.zeros((tq, 128), F32))
            carry = lax.fori_loop(0, qi, full_step, init)
            s, start = scores(qi)
            kpos = start + lax.broadcasted_iota(jnp.int32, (tq, tk), 1)
            s = jnp.where(kpos <= t_col, s, MASK_VALUE)
            _, l, acc = _flash_update(s, *carry, v_ref[pl.ds(start, tk), cols])
            outs.append(acc / l)
        o_ref[:, cols] = jnp.where(left, outs[0], outs[1])


def _fox_call(fqkv, ct):
    B, T, _ = fqkv.shape
    tq = TQ_FOX
    nt = T // tq
    return pl.pallas_call(
        _fox_kernel,
        grid=(B, nt),
        in_specs=[pl.BlockSpec((None, tq, 256), lambda b, i: (b, i, 0)),
                  pl.BlockSpec((None, T, 256), lambda b, i: (b, 0, 1)),
                  pl.BlockSpec((None, T, 256), lambda b, i: (b, 0, 2)),
                  pl.BlockSpec((None, nt, 8, tq), lambda b, i: (b, 0, 0, 0))],
        out_specs=pl.BlockSpec((None, tq, 256), lambda b, i: (b, i, 0)),
        out_shape=jax.ShapeDtypeStruct((B, T, 256), F32),
        compiler_params=pltpu.CompilerParams(dimension_semantics=("arbitrary", "arbitrary"),
                                             vmem_limit_bytes=VMEM_LIMIT),
        name="fox",
    )(fqkv, fqkv, fqkv, ct)


def _merge_kernel(x_ref, g_ref, wg_ref, wm_ref, wb_ref, wo_ref, fg_ref, o0_ref, o1_ref, o2_ref, o3_ref,
                  out_ref, *, last):
    x = x_ref[...]
    h = _rmsnorm(x, g_ref[...]).astype(BF16)
    acc = None
    for i, o_ref in enumerate((o0_ref, o1_ref, o2_ref, o3_ref)):
        gate = _dot(h, wg_ref[:, i * BRANCH_WIDTH:(i + 1) * BRANCH_WIDTH])
        a = (o_ref[...] * (gate * _sigmoid(gate))).astype(BF16)
        branch = _dot(a, wb_ref[i])
        merge = _dot(h, wm_ref[:, i * D_MODEL:(i + 1) * D_MODEL])
        term = _sigmoid(merge) * branch
        acc = term if acc is None else acc + term
    y = x + _dot(acc.astype(BF16), wo_ref[...])
    if last:
        y = _rmsnorm(y, fg_ref[...])
    out_ref[...] = y


def _merge_call(x, g, wg, wm, wb, wo, fg, o_nsa, o_pool, o_conv, o_fox, last):
    B, T, _ = x.shape
    tm = TM_MERGE
    tok = lambda width: pl.BlockSpec((None, tm, width), lambda b, t: (b, t, 0))
    const = lambda shape: pl.BlockSpec(shape, lambda b, t: (0,) * len(shape))
    return pl.pallas_call(
        functools.partial(_merge_kernel, last=last),
        grid=(B, T // tm),
        in_specs=[tok(D_MODEL), const((1, D_MODEL)), const((D_MODEL, N_BRANCH * BRANCH_WIDTH)),
                  const((D_MODEL, N_BRANCH * D_MODEL)), const((N_BRANCH, BRANCH_WIDTH, D_MODEL)),
                  const((D_MODEL, D_MODEL)), const((1, D_MODEL)), tok(256), tok(256), tok(256), tok(256)],
        out_specs=tok(D_MODEL),
        out_shape=jax.ShapeDtypeStruct((B, T, D_MODEL), F32),
        compiler_params=pltpu.CompilerParams(dimension_semantics=("arbitrary", "arbitrary"),
                                             vmem_limit_bytes=VMEM_LIMIT),
        name="merge",
    )(x, g, wg, wm, wb, wo, fg, o_nsa, o_pool, o_conv, o_fox)


def _rope_tables(pos):
    n = pos.shape[0]
    inv = ROPE_THETA ** (-jnp.arange(ROPE_HALF, dtype=F32) / ROPE_HALF)
    ang = pos.astype(F32)[:, None] * inv[None, :]
    cos, sin = jnp.cos(ang), jnp.sin(ang)
    rest = HEAD_DIM - 2 * ROPE_HALF
    c = jnp.concatenate([cos, cos, jnp.ones((n, rest), F32)], axis=1)
    s = jnp.concatenate([-sin, sin, jnp.zeros((n, rest), F32)], axis=1)
    return jnp.concatenate([c, c], axis=1), jnp.concatenate([s, s], axis=1)


def _proj_weight(w_in):
    col = lambda name, width: w_in[:, _OFF[name]:_OFF[name] + width]
    k_s, k_w, v_s, v_w = col("k_s", 64), col("k_w", 64), col("v_s", 64), col("v_w", 64)
    misc = jnp.concatenate([col("nsa_g", 12), col("fox_f", 4), jnp.zeros((D_MODEL, 112), F32)], axis=1)
    w = jnp.concatenate([col("nsa_q", 256), k_s, k_w, k_w, k_s, v_s, v_s, v_w, v_w, col("k_c", 128), misc,
                         col("fox", 768), col("pool", 256), col("conv", 768)], axis=1)
    assert w.shape[1] == _P_COLS
    return w.astype(BF16)


def _compress_weights(cmp_pos, cmp_w1, cmp_b1, cmp_w2, cmp_b2):
    half = CMP_LEN // 2

    def w1_half(lo):
        wk = cmp_w1[0, lo * HEAD_DIM:(lo + half) * HEAD_DIM].reshape(half, HEAD_DIM, CMP_HIDDEN)
        wv = cmp_w1[1, lo * HEAD_DIM:(lo + half) * HEAD_DIM].reshape(half, HEAD_DIM, CMP_HIDDEN)
        z = jnp.zeros_like(wk)
        top = jnp.concatenate([wk, z], axis=2)
        bot = jnp.concatenate([z, wv], axis=2)
        return jnp.concatenate([top, bot], axis=1).reshape(half * 2 * HEAD_DIM, 2 * CMP_HIDDEN).astype(BF16)

    def pos_half(lo):
        return jnp.concatenate([cmp_pos[0, lo:lo + half], cmp_pos[1, lo:lo + half]], axis=1).reshape(1, -1)

    lane = np.arange(HEAD_DIM)
    perm = np.where(lane < ROPE_HALF, lane + ROPE_HALF, np.where(lane < 2 * ROPE_HALF, lane - ROPE_HALF, lane))
    w2k, w2v = cmp_w2[0], cmp_w2[1]
    zk = jnp.zeros_like(w2k)
    w2 = jnp.concatenate([
        jnp.concatenate([w2k, w2k, w2k[:, perm], w2k[:, perm], zk, zk], axis=1),
        jnp.concatenate([zk, zk, zk, zk, w2v, w2v], axis=1)], axis=0).astype(BF16)
    b2k, b2v = cmp_b2[0], cmp_b2[1]
    b2 = jnp.concatenate([b2k, b2k, b2k[perm], b2k[perm], b2v, b2v])[None, :]
    b1 = jnp.concatenate([cmp_b1[0], cmp_b1[1]])[None, :]
    return pos_half(0), pos_half(half), w1_half(0), w1_half(half), b1, w2, b2


def _block_diag(pool_w):
    n, c, _ = pool_w.shape
    out = jnp.zeros((n * c, n * c), pool_w.dtype)
    for i in range(n):
        out = out.at[i * c:(i + 1) * c, i * c:(i + 1) * c].set(pool_w[i])
    return out


def kernel(x, norm_g, w_in, fox_f_bias, cmp_pos, cmp_w1, cmp_b1, cmp_w2, cmp_b2, pool_w, pool_scale, conv_w,
           w_branch, w_out, final_norm_g):
    B, T, _ = x.shape
    depth = norm_g.shape[0]
    assert T % TM_PROJ == 0 and T % TK_NSA == 0 and T >= TQ_NSA + WINDOW
    rows = T // CMP_STRIDE
    nblk = T // SEL_LEN

    cc_tok, ss_tok = _rope_tables(jnp.arange(T))
    cmp_end = jnp.arange(rows) * CMP_STRIDE + CMP_LEN - 1
    cc_cmp, ss_cmp = _rope_tables(cmp_end)
    ci = np.arange(rows)[:, None] * CMP_STRIDE
    sj = np.arange(nblk)[None, :] * SEL_LEN
    overlap_t = jnp.asarray(((ci < sj + SEL_LEN) & (ci + CMP_LEN > sj)).T, F32)
    tri = jnp.asarray(np.tril(np.ones((TM_PROJ, TM_PROJ), np.float32)))

    for l in range(depth):
        w1 = _proj_weight(w_in[l])
        fb = jnp.zeros((1, 128), F32).at[0, 12:16].set(fox_f_bias[l])
        (q, ksw, kws, vss, vww, kvc, misc, ct, fqkv, o_pool, o_conv) = _proj_call(
            x, norm_g[l][None, :], w1, cc_tok, ss_tok, fb, conv_w[l], _block_diag(pool_w[l]).astype(BF16),
            pool_scale[l][None, :], tri)
        x16 = kvc.reshape(B, rows, CMP_STRIDE * 2 * HEAD_DIM)
        kc, vc = _compress_call(x16, *_compress_weights(cmp_pos[l], cmp_w1[l], cmp_b1[l], cmp_w2[l], cmp_b2[l]),
                                cc_cmp, ss_cmp)
        o_nsa = _nsa_call(q, ksw, kws, vss, vww, kc, vc, misc, overlap_t)
        o_fox = _fox_call(fqkv, ct)
        wg = w_in[l][:, _OFF["gate"]:_OFF["merge"]].astype(BF16)
        wm = w_in[l][:, _OFF["merge"]:_OFF["end"]].astype(BF16)
        x = _merge_call(x, norm_g[l][None, :], wg, wm, w_branch[l].astype(BF16), w_out[l].astype(BF16),
                        final_norm_g[None, :], o_nsa, o_pool, o_conv, o_fox, last=(l == depth - 1))
    return x
```

```python
import functools

import numpy as np
import jax
import jax.numpy as jnp
from jax import lax
from jax.experimental import pallas as pl
from jax.experimental.pallas import tpu as pltpu

F32 = jnp.float32
BF16 = jnp.bfloat16

D_MODEL = 1024
N_BRANCH = 4
BRANCH_WIDTH = 256
HEAD_DIM = 64
N_HEADS = 4
CMP_LEN = 32
CMP_STRIDE = 16
CMP_HIDDEN = 128
SEL_LEN = 64
N_SEL = 16
WINDOW = 512
FORCE_BONUS = 1.0e4
ROPE_THETA = 500000.0
ROPE_HALF = 8
NORM_EPS = 1e-6
MASK_VALUE = -1e30

_OFF = dict(nsa_q=0, k_c=256, v_c=320, k_s=384, v_s=448, k_w=512, v_w=576, nsa_g=640, pool=652,
            conv=908, fox_q=1676, fox_k=1932, fox_v=2188, fox_f=2444, gate=2448, merge=3472, end=7568)

_P_ROPE = 0
_P_VSW = 512
_P_KVC = 640
_P_MISC = 768
_P_FOXQ = 896
_P_FOXK = 1152
_P_FOXV = 1664
_P_POOL = 1920
_P_CONV = 2176
_P_COLS = 2944

CHUNK = 128
V_ROWS = HEAD_DIM + 16
LOG2E = 1.4426950408889634
Q_SCALE = HEAD_DIM ** -0.5 * LOG2E
TM_PROJ = 512
KV_TILE = 512
TQ_FOX = 512
TM_MERGE = 512
N_MISC_T = 16
POOL_HALO = 16
CONV_HALO = 8
VMEM_LIMIT = 56 * 1024 * 1024


def _dot(a, b, precision=None):
    return jnp.dot(a, b, preferred_element_type=F32, precision=precision)


def _sigmoid(x):
    return 1.0 / (1.0 + jnp.exp(-x))


def _rmsnorm(x, g):
    return x * lax.rsqrt(jnp.mean(x * x, axis=-1, keepdims=True) + NORM_EPS) * g


def _flash_t(s, m, l, acc, v_t):
    m_new = jnp.maximum(m, jnp.max(s, axis=0, keepdims=True))
    alpha = jnp.exp2(m - m_new)
    pv = _dot(v_t, jnp.exp2(s - m_new).astype(BF16))
    l = alpha * l + pv[HEAD_DIM:HEAD_DIM + 1]
    acc = alpha * acc + pv[0:HEAD_DIM]
    return m_new, l, acc


def _attend_chunks(streams, n):
    nxt = [st[0](0) for st in streams]
    carries = [st[2] for st in streams]
    for c in range(n):
        cur = nxt
        if c + 1 < n:
            nxt = [st[0](c + 1) for st in streams]
        carries = [_flash_t(cur[i], *carries[i], streams[i][1](c)) for i in range(len(streams))]
    return carries


def _flash_init(n):
    return (jnp.full((1, n), MASK_VALUE, F32), jnp.zeros((1, n), F32), jnp.zeros((HEAD_DIM, n), F32))


def _proj_kernel(x_ref, g_ref, w_ref, cc_ref, ss_ref, fb_ref, cw_ref, pw_ref, ps_ref, tri_ref,
                 q_ref, ks_ref, kw_ref, vst_ref, vwt_ref, kvc_ref, mt_ref, fq_ref, fk_ref, fvt_ref,
                 opool_ref, oconv_ref, pext, cext, ccarry, *, tm):
    ti = pl.program_id(1)
    n_chunk = tm // CHUNK
    x = x_ref[...]
    h = _rmsnorm(x, g_ref[...]).astype(BF16)

    def seg(lo, width):
        return _dot(h, w_ref[:, lo:lo + width])

    lane = lax.broadcasted_iota(jnp.int32, (tm, 128), 1)
    row = lax.broadcasted_iota(jnp.int32, (tm, 128), 0) + ti * tm
    first_half = (lane % HEAD_DIM) < ROPE_HALF
    cc = cc_ref[...]
    ss = ss_ref[...]

    def rope(xc):
        partner = jnp.where(first_half, pltpu.roll(xc, 128 - ROPE_HALF, 1), pltpu.roll(xc, ROPE_HALF, 1))
        return xc * cc + partner * ss

    q_ref[:, 0:128] = (rope(seg(_P_ROPE, 128)) * Q_SCALE).astype(BF16)
    q_ref[:, 128:256] = (rope(seg(_P_ROPE + 128, 128)) * Q_SCALE).astype(BF16)
    block_onehot = jnp.where(lane - HEAD_DIM == row // SEL_LEN, 1.0, 0.0)
    ks_ref[...] = (rope(seg(_P_ROPE + 256, 128)) + block_onehot).astype(BF16)
    kw_ref[...] = rope(seg(_P_ROPE + 384, 128)).astype(BF16)
    vsw_t = jnp.transpose(seg(_P_VSW, 128)).astype(BF16)
    ones_rows = jnp.ones((V_ROWS - HEAD_DIM, CHUNK), BF16)
    for c in range(n_chunk):
        cols = slice(c * CHUNK, (c + 1) * CHUNK)
        vst_ref[c] = jnp.concatenate([vsw_t[0:HEAD_DIM, cols], ones_rows], axis=0)
        vwt_ref[c] = jnp.concatenate([vsw_t[HEAD_DIM:128, cols], ones_rows], axis=0)
    kvc_ref[...] = seg(_P_KVC, 128)

    z = seg(_P_MISC, 128)
    zb = z + fb_ref[...]
    logf = jnp.minimum(zb, 0.0) - jnp.log(1.0 + jnp.exp(-jnp.abs(zb)))
    logf = jnp.where((lane >= 12) & (lane < 16), logf, 0.0)

    @pl.when(ti == 0)
    def _():
        ccarry[...] = jnp.zeros_like(ccarry)
        pext[0:POOL_HALO, :] = jnp.zeros((POOL_HALO, BRANCH_WIDTH), F32)
        cext[0:CONV_HALO, :] = jnp.zeros((CONV_HALO, BRANCH_WIDTH), F32)

    csum = _dot(tri_ref[...], logf, precision=lax.Precision.HIGHEST) + ccarry[0:1, :]
    ccarry[0:1, :] = csum[tm - 1:tm, :]
    misc = jnp.where(lane < 12, _sigmoid(z), csum)
    mt_ref[...] = jnp.transpose(misc)[0:N_MISC_T, :]

    fq_ref[...] = (seg(_P_FOXQ, 256) * Q_SCALE).astype(BF16)
    neg = csum * (-LOG2E)
    hi = neg.astype(BF16).astype(F32)
    mid = (neg - hi).astype(BF16).astype(F32)
    lo = (neg - hi) - mid
    for hd in range(N_HEADS):
        src = 12 + hd
        extra = (jnp.where(lane == HEAD_DIM, pltpu.roll(hi, HEAD_DIM - src, 1), 0.0)
                 + jnp.where(lane == HEAD_DIM + 1, pltpu.roll(mid, HEAD_DIM + 1 - src, 1), 0.0)
                 + jnp.where(lane == HEAD_DIM + 2, pltpu.roll(lo, HEAD_DIM + 2 - src, 1), 0.0))
        fk_ref[:, hd * 128:(hd + 1) * 128] = (seg(_P_FOXK + hd * 128, 128) + extra).astype(BF16)
    fv_t = jnp.transpose(seg(_P_FOXV, 256)).astype(BF16)
    for c in range(n_chunk):
        cols = slice(c * CHUNK, (c + 1) * CHUNK)
        fvt_ref[c] = jnp.concatenate(
            [piece for hd in range(N_HEADS) for piece in (fv_t[hd * HEAD_DIM:(hd + 1) * HEAD_DIM, cols], ones_rows)],
            axis=0)

    pext[POOL_HALO:POOL_HALO + tm, :] = seg(_P_POOL, 256)

    def pld(k, c):
        return pext[pl.ds(POOL_HALO - k, tm), c * 128:(c + 1) * 128]

    left = lane < 64
    rowp1 = (row + 1).astype(F32)

    def cnt(w):
        return jnp.minimum(rowp1, float(w))

    e0 = pld(0, 0)
    s2 = e0 + pld(1, 0)
    s4 = s2 + pld(2, 0) + pld(3, 0)
    p0 = jnp.where(left, s2 / cnt(2), s4 / cnt(4)) - e0
    f0 = pld(0, 1)
    s8 = f0
    for k in range(1, 8):
        s8 = s8 + pld(k, 1)
    s16 = s8
    for k in range(8, 16):
        s16 = s16 + pld(k, 1)
    p1 = jnp.where(left, s8 / cnt(8), s16 / cnt(16)) - f0
    pw = pw_ref[...]
    mixed = _dot(p0.astype(BF16), pw[0:128, :]) + _dot(p1.astype(BF16), pw[128:256, :])
    opool_ref[...] = mixed * ps_ref[...]
    pext[0:POOL_HALO, :] = pext[tm:tm + POOL_HALO, :]

    cv = seg(_P_CONV, 768)
    u = cv[:, 512:768] * cv[:, 0:256]
    cext[CONV_HALO:CONV_HALO + tm, :] = u
    y = (cext[pl.ds(CONV_HALO - 2, tm), :] * cw_ref[0:1, :]
         + cext[pl.ds(CONV_HALO - 1, tm), :] * cw_ref[1:2, :]
         + u * cw_ref[2:3, :])
    oconv_ref[...] = cv[:, 256:512] * y
    cext[0:CONV_HALO, :] = cext[tm:tm + CONV_HALO, :]


def _proj_call(x, g, w1, cc, ss, fb, cw, pw, ps, tri):
    B, T, _ = x.shape
    tm = TM_PROJ
    nt = T // tm
    nck = tm // CHUNK
    tok = lambda width: pl.BlockSpec((None, tm, width), lambda b, t: (b, t, 0))
    chunked = lambda rows: pl.BlockSpec((None, nck, rows, CHUNK), lambda b, t: (b, t, 0, 0))
    const = lambda shape: pl.BlockSpec(shape, lambda b, t: (0,) * len(shape))
    out_shape = [
        jax.ShapeDtypeStruct((B, T, 256), BF16),
        jax.ShapeDtypeStruct((B, T, 128), BF16),
        jax.ShapeDtypeStruct((B, T, 128), BF16),
        jax.ShapeDtypeStruct((B, T // CHUNK, V_ROWS, CHUNK), BF16),
        jax.ShapeDtypeStruct((B, T // CHUNK, V_ROWS, CHUNK), BF16),
        jax.ShapeDtypeStruct((B, T, 128), F32),
        jax.ShapeDtypeStruct((B, nt, N_MISC_T, tm), F32),
        jax.ShapeDtypeStruct((B, T, 256), BF16),
        jax.ShapeDtypeStruct((B, T, 512), BF16),
        jax.ShapeDtypeStruct((B, T // CHUNK, N_HEADS * V_ROWS, CHUNK), BF16),
        jax.ShapeDtypeStruct((B, T, 256), F32),
        jax.ShapeDtypeStruct((B, T, 256), F32),
    ]
    out_specs = [tok(256), tok(128), tok(128), chunked(V_ROWS), chunked(V_ROWS), tok(128),
                 pl.BlockSpec((None, None, N_MISC_T, tm), lambda b, t: (b, t, 0, 0)),
                 tok(256), tok(512), chunked(N_HEADS * V_ROWS), tok(256), tok(256)]
    in_specs = [tok(D_MODEL), const((1, D_MODEL)), const((D_MODEL, _P_COLS)),
                pl.BlockSpec((tm, 128), lambda b, t: (t, 0)), pl.BlockSpec((tm, 128), lambda b, t: (t, 0)),
                const((1, 128)), const((3, 256)), const((256, 256)), const((1, 256)), const((tm, tm))]
    return pl.pallas_call(
        functools.partial(_proj_kernel, tm=tm),
        grid=(B, nt),
        in_specs=in_specs,
        out_specs=out_specs,
        out_shape=out_shape,
        scratch_shapes=[pltpu.VMEM((tm + POOL_HALO, 256), F32), pltpu.VMEM((tm + CONV_HALO, 256), F32),
                        pltpu.VMEM((8, 128), F32)],
        compiler_params=pltpu.CompilerParams(dimension_semantics=("arbitrary", "arbitrary"),
                                             vmem_limit_bytes=VMEM_LIMIT),
        name="proj",
    )(x, g, w1, cc, ss, fb, cw, pw, ps, tri)


def _compress_kernel(x_ref, pt_ref, pb_ref, wt_ref, wb_ref, b1_ref, w2_ref, b2_ref, cc_ref, ss_ref,
                     kc_ref, vct_ref, *, rows):
    x = x_ref[...]
    a = _dot((x + pt_ref[...]).astype(BF16), wt_ref[...])
    b = _dot((x + pb_ref[...]).astype(BF16), wb_ref[...])
    hid = a + pltpu.roll(b, rows - 1, 0) + b1_ref[...]
    act = hid * _sigmoid(hid)
    out = _dot(act.astype(BF16), w2_ref[...]) + b2_ref[...]
    lane = lax.broadcasted_iota(jnp.int32, (rows, 128), 1)
    roped = out[:, 0:128] * cc_ref[...] + out[:, 128:256] * ss_ref[...]
    kc_ref[...] = jnp.where(lane < HEAD_DIM, roped, 0.0).astype(BF16)
    vct_ref[...] = jnp.transpose(out[:, 256:384])[0:HEAD_DIM, :].astype(BF16)


def _compress_call(x16, pt, pb, wt, wb, b1, w2, b2, cc, ss):
    B, rows, width = x16.shape
    const = lambda shape: pl.BlockSpec(shape, lambda b: (0,) * len(shape))
    return pl.pallas_call(
        functools.partial(_compress_kernel, rows=rows),
        grid=(B,),
        in_specs=[pl.BlockSpec((None, rows, width), lambda b: (b, 0, 0)),
                  const((1, width)), const((1, width)), const((width, 256)), const((width, 256)),
                  const((1, 256)), const((256, 384)), const((1, 384)), const((rows, 128)), const((rows, 128))],
        out_specs=[pl.BlockSpec((None, rows, 128), lambda b: (b, 0, 0)),
                   pl.BlockSpec((None, HEAD_DIM, rows), lambda b: (b, 0, 0))],
        out_shape=[jax.ShapeDtypeStruct((B, rows, 128), BF16), jax.ShapeDtypeStruct((B, HEAD_DIM, rows), BF16)],
        compiler_params=pltpu.CompilerParams(dimension_semantics=("arbitrary",),
                                             vmem_limit_bytes=VMEM_LIMIT),
        name="compress",
    )(x16, pt, pb, wt, wb, b1, w2, b2, cc, ss)


def _nsa_kernel(q_ref, ks_ref, kw_ref, vst_ref, vwt_ref, kc_ref, vct_ref, mt_ref, ovt_ref,
                o_ref, key_ref, sbuf, *, nblk, ncmp, seq):
    qi = pl.program_id(1)
    tq = CHUNK
    nq = N_HEADS * tq
    q_local = lax.broadcasted_iota(jnp.int32, (1, nq), 1) % tq
    t_lane = qi * tq + q_local

    q_t = jnp.transpose(q_ref[...].astype(F32))
    q4 = jnp.concatenate([q_t[HEAD_DIM * h:HEAD_DIM * (h + 1)] for h in range(N_HEADS)], axis=1)
    qt_plain = jnp.concatenate([q4, jnp.zeros((HEAD_DIM, nq), F32)], axis=0).astype(BF16)

    sc = _dot(kc_ref[...], qt_plain)
    n_sub = lax.broadcasted_iota(jnp.int32, (ncmp, nq), 0)
    cmask = (n_sub * CMP_STRIDE + (CMP_LEN - 1)) <= t_lane
    sc = jnp.where(cmask, sc, MASK_VALUE)
    mc = jnp.max(sc, axis=0, keepdims=True)
    pc = jnp.where(cmask, jnp.exp2(sc - mc), 0.0)
    lc = jnp.sum(pc, axis=0, keepdims=True)
    pc = pc / jnp.where(lc > 0.0, lc, 1.0)
    o_cmp = _dot(vct_ref[...], pc.astype(BF16))
    psum = (pc[:, 0:tq] + pc[:, tq:2 * tq]) + (pc[:, 2 * tq:3 * tq] + pc[:, 3 * tq:4 * tq])

    imp = _dot(ovt_ref[...], psum, precision=lax.Precision.HIGHEST)
    jb = lax.broadcasted_iota(jnp.int32, (nblk, tq), 0)
    tt = qi * tq + lax.broadcasted_iota(jnp.int32, (nblk, tq), 1)
    jt = tt // SEL_LEN
    forced = (jb == 0) | (jb == jt) | (jb == jt - 1)
    valid = jb * SEL_LEN <= tt
    key = jnp.where(valid, jnp.where(forced, imp + FORCE_BONUS, imp), MASK_VALUE)
    key_ref[...] = key
    rank = jnp.zeros((nblk, tq), F32)
    for jp in range(nblk):
        other = key_ref[pl.ds(jp, 1), :]
        ahead = (other > key) | ((other == key) & (jp < jb))
        rank = rank + jnp.where(ahead, 1.0, 0.0)
    selected = rank < float(min(N_SEL, nblk))
    sel_bias = jnp.where(selected & (jb < qi * (tq // SEL_LEN)), 0.0, MASK_VALUE)
    if nblk < HEAD_DIM:
        sel_bias = jnp.concatenate([sel_bias, jnp.zeros((HEAD_DIM - nblk, tq), F32)], axis=0)
    qt_sel = jnp.concatenate([q4, jnp.concatenate([sel_bias] * N_HEADS, axis=1)], axis=0).astype(BF16)

    hw = nq // 2
    halves = (slice(0, hw), slice(hw, nq))
    k_local = lax.broadcasted_iota(jnp.int32, (tq, hw), 0)
    q_loc = q_local[:, 0:hw]
    causal = k_local <= q_loc
    dstart = pl.multiple_of(qi * tq, tq)
    n_win = WINDOW // tq

    s_diag = [jnp.where(causal, _dot(ks_ref[pl.ds(dstart, tq), :], qt_plain[:, hs]), MASK_VALUE) for hs in halves]

    def win_stream(hs):
        def score(c):
            if c == 0:
                return jnp.where(causal, _dot(kw_ref[pl.ds(dstart, tq), :], qt_plain[:, hs]), MASK_VALUE)
            ci = qi - n_win + (c - 1)
            start = pl.multiple_of(jnp.maximum(ci, 0) * tq, tq)
            s = _dot(kw_ref[pl.ds(start, tq), :], qt_plain[:, hs])
            ok = ((k_local > q_loc) & (ci >= 0)) if c == 1 else jnp.broadcast_to(ci >= 0, (tq, hw))
            return jnp.where(ok, s, MASK_VALUE)

        def value(c):
            return vwt_ref[qi] if c == 0 else vwt_ref[jnp.maximum(qi - n_win + (c - 1), 0)]

        return score, value, _flash_init(hw)

    carries_w = _attend_chunks([win_stream(hs) for hs in halves], n_win + 1)
    carries_s = tuple(_flash_t(s_diag[i], *_flash_init(hw), vst_ref[qi]) for i in range(2))
    o_win = jnp.concatenate([acc / l for _, l, acc in carries_w], axis=1)

    n_c = KV_TILE // tq

    def slc_scores(kt, c):
        start = pl.multiple_of((kt * n_c + c) * tq, tq)
        return _dot(ks_ref[pl.ds(start, tq), :], qt_sel)

    for c in range(n_c):
        sbuf[c] = slc_scores(0, c)

    def slc_tile(kt, carries):
        nxt = jnp.minimum(kt + 1, seq // KV_TILE - 1)
        for c in range(n_c):
            s = sbuf[c]
            sbuf[c] = slc_scores(nxt, c)
            carries = tuple(_flash_t(s[:, hs], *carries[i], vst_ref[kt * n_c + c]) for i, hs in enumerate(halves))
        return carries

    n_past = (qi * tq + KV_TILE - 1) // KV_TILE
    carries_s = lax.fori_loop(0, n_past, slc_tile, carries_s)
    o_slc = jnp.concatenate([acc / l for _, l, acc in carries_s], axis=1)

    g = mt_ref[...]
    heads = []
    for h in range(N_HEADS):
        cols = slice(h * tq, (h + 1) * tq)
        heads.append(g[h:h + 1] * o_cmp[:, cols] + g[4 + h:5 + h] * o_slc[:, cols]
                     + g[8 + h:9 + h] * o_win[:, cols])
    o_ref[...] = jnp.transpose(jnp.concatenate(heads, axis=0))


def _nsa_call(q, ks, kw, vst, vwt, kc, vct, mt, ovt):
    B, T, _ = q.shape
    nblk, ncmp = ovt.shape
    tq = CHUNK
    per_mt = TM_PROJ // tq
    full = lambda width: pl.BlockSpec((None, T, width), lambda b, i: (b, 0, 0))
    vfull = pl.BlockSpec((None, T // CHUNK, V_ROWS, CHUNK), lambda b, i: (b, 0, 0, 0))
    return pl.pallas_call(
        functools.partial(_nsa_kernel, nblk=nblk, ncmp=ncmp, seq=T),
        grid=(B, T // tq),
        in_specs=[pl.BlockSpec((None, tq, 256), lambda b, i: (b, i, 0)),
                  full(128), full(128), vfull, vfull,
                  pl.BlockSpec((None, ncmp, 128), lambda b, i: (b, 0, 0)),
                  pl.BlockSpec((None, HEAD_DIM, ncmp), lambda b, i: (b, 0, 0)),
                  pl.BlockSpec((None, None, N_MISC_T, tq), lambda b, i: (b, i // per_mt, 0, i % per_mt)),
                  pl.BlockSpec((nblk, ncmp), lambda b, i: (0, 0))],
        out_specs=pl.BlockSpec((None, tq, 256), lambda b, i: (b, i, 0)),
        out_shape=jax.ShapeDtypeStruct((B, T, 256), F32),
        scratch_shapes=[pltpu.VMEM((nblk, tq), F32), pltpu.VMEM((KV_TILE // CHUNK, CHUNK, N_HEADS * tq), F32)],
        compiler_params=pltpu.CompilerParams(dimension_semantics=("arbitrary", "arbitrary"),
                                             vmem_limit_bytes=VMEM_LIMIT),
        name="nsa",
    )(q, ks, kw, vst, vwt, kc, vct, mt, ovt)


def _fox_kernel(q_ref, k_ref, vt_ref, o_ref, sbuf, *, seq):
    qi = pl.program_id(1)
    nq = TQ_FOX
    n_chunk = KV_TILE // CHUNK
    t_lane = qi * nq + lax.broadcasted_iota(jnp.int32, (1, nq), 1)
    k_local = lax.broadcasted_iota(jnp.int32, (CHUNK, nq), 0)
    q_t = jnp.transpose(q_ref[...].astype(F32))
    ones_rows = jnp.where(lax.broadcasted_iota(jnp.int32, (HEAD_DIM, nq), 0) < 3, 1.0, 0.0)

    qts = [jnp.concatenate([q_t[HEAD_DIM * h:HEAD_DIM * (h + 1)], ones_rows], axis=0).astype(BF16)
           for h in range(N_HEADS)]

    def scores(kt, c, h):
        start = pl.multiple_of((kt * n_chunk + c) * CHUNK, CHUNK)
        return _dot(k_ref[pl.ds(start, CHUNK), h * 128:(h + 1) * 128], qts[h]), start

    def value(kt, c, h):
        return vt_ref[kt * n_chunk + c, h * V_ROWS:(h + 1) * V_ROWS, :]

    def diag_stream(h):
        def score(c):
            s, start = scores(qi, c, h)
            return jnp.where(start + k_local <= t_lane, s, MASK_VALUE)

        return score, lambda c: value(qi, c, h), _flash_init(nq)

    carries = tuple(_attend_chunks([diag_stream(h) for h in range(N_HEADS)], n_chunk))

    for c in range(n_chunk):
        for h in range(N_HEADS):
            sbuf[h, c] = scores(0, c, h)[0]

    def tile(kt, carries):
        nxt = jnp.minimum(kt + 1, seq // KV_TILE - 1)
        carries = list(carries)
        for c in range(n_chunk):
            for h in range(N_HEADS):
                s = sbuf[h, c]
                sbuf[h, c] = scores(nxt, c, h)[0]
                carries[h] = _flash_t(s, *carries[h], value(kt, c, h))
        return tuple(carries)

    carries = lax.fori_loop(0, qi, tile, carries)
    o_ref[...] = jnp.transpose(jnp.concatenate([acc / l for _, l, acc in carries], axis=0))


def _fox_call(fq, fk, fvt):
    B, T, _ = fq.shape
    tq = TQ_FOX
    return pl.pallas_call(
        functools.partial(_fox_kernel, seq=T),
        grid=(B, T // tq),
        in_specs=[pl.BlockSpec((None, tq, 256), lambda b, i: (b, i, 0)),
                  pl.BlockSpec((None, T, 512), lambda b, i: (b, 0, 0)),
                  pl.BlockSpec((None, T // CHUNK, N_HEADS * V_ROWS, CHUNK), lambda b, i: (b, 0, 0, 0))],
        out_specs=pl.BlockSpec((None, tq, 256), lambda b, i: (b, i, 0)),
        out_shape=jax.ShapeDtypeStruct((B, T, 256), F32),
        scratch_shapes=[pltpu.VMEM((N_HEADS, KV_TILE // CHUNK, CHUNK, tq), F32)],
        compiler_params=pltpu.CompilerParams(dimension_semantics=("arbitrary", "arbitrary"),
                                             vmem_limit_bytes=VMEM_LIMIT),
        name="fox",
    )(fq, fk, fvt)


def _merge_kernel(x_ref, g_ref, wg_ref, wm_ref, wb_ref, wo_ref, fg_ref, o0_ref, o1_ref, o2_ref, o3_ref,
                  out_ref, *, last):
    x = x_ref[...]
    h = _rmsnorm(x, g_ref[...]).astype(BF16)
    acc = None
    for i, o_ref in enumerate((o0_ref, o1_ref, o2_ref, o3_ref)):
        gate = _dot(h, wg_ref[:, i * BRANCH_WIDTH:(i + 1) * BRANCH_WIDTH])
        a = (o_ref[...] * (gate * _sigmoid(gate))).astype(BF16)
        branch = _dot(a, wb_ref[i])
        merge = _dot(h, wm_ref[:, i * D_MODEL:(i + 1) * D_MODEL])
        term = _sigmoid(merge) * branch
        acc = term if acc is None else acc + term
    y = x + _dot(acc.astype(BF16), wo_ref[...])
    if last:
        y = _rmsnorm(y, fg_ref[...])
    out_ref[...] = y


def _merge_call(x, g, wg, wm, wb, wo, fg, o_nsa, o_pool, o_conv, o_fox, last):
    B, T, _ = x.shape
    tm = TM_MERGE
    tok = lambda width: pl.BlockSpec((None, tm, width), lambda b, t: (b, t, 0))
    const = lambda shape: pl.BlockSpec(shape, lambda b, t: (0,) * len(shape))
    return pl.pallas_call(
        functools.partial(_merge_kernel, last=last),
        grid=(B, T // tm),
        in_specs=[tok(D_MODEL), const((1, D_MODEL)), const((D_MODEL, N_BRANCH * BRANCH_WIDTH)),
                  const((D_MODEL, N_BRANCH * D_MODEL)), const((N_BRANCH, BRANCH_WIDTH, D_MODEL)),
                  const((D_MODEL, D_MODEL)), const((1, D_MODEL)), tok(256), tok(256), tok(256), tok(256)],
        out_specs=tok(D_MODEL),
        out_shape=jax.ShapeDtypeStruct((B, T, D_MODEL), F32),
        compiler_params=pltpu.CompilerParams(dimension_semantics=("arbitrary", "arbitrary"),
                                             vmem_limit_bytes=VMEM_LIMIT),
        name="merge",
    )(x, g, wg, wm, wb, wo, fg, o_nsa, o_pool, o_conv, o_fox)


def _rope_tables(pos):
    n = pos.shape[0]
    inv = ROPE_THETA ** (-jnp.arange(ROPE_HALF, dtype=F32) / ROPE_HALF)
    ang = pos.astype(F32)[:, None] * inv[None, :]
    cos, sin = jnp.cos(ang), jnp.sin(ang)
    rest = HEAD_DIM - 2 * ROPE_HALF
    c = jnp.concatenate([cos, cos, jnp.ones((n, rest), F32)], axis=1)
    s = jnp.concatenate([-sin, sin, jnp.zeros((n, rest), F32)], axis=1)
    return jnp.concatenate([c, c], axis=1), jnp.concatenate([s, s], axis=1)


def _proj_weight(w_in):
    col = lambda name, width: w_in[:, _OFF[name]:_OFF[name] + width]
    z64 = jnp.zeros((D_MODEL, HEAD_DIM), F32)
    misc = jnp.concatenate([col("nsa_g", 12), col("fox_f", 4), jnp.zeros((D_MODEL, 112), F32)], axis=1)
    fox_k = []
    for h in range(N_HEADS):
        fox_k += [w_in[:, _OFF["fox_k"] + h * HEAD_DIM:_OFF["fox_k"] + (h + 1) * HEAD_DIM], z64]
    w = jnp.concatenate([col("nsa_q", 256), col("k_s", 64), z64, col("k_w", 64), z64,
                         col("v_s", 64), col("v_w", 64), col("k_c", 128), misc,
                         col("fox_q", 256)] + fox_k + [col("fox_v", 256), col("pool", 256), col("conv", 768)],
                        axis=1)
    assert w.shape[1] == _P_COLS
    return w.astype(BF16)


def _compress_weights(cmp_pos, cmp_w1, cmp_b1, cmp_w2, cmp_b2):
    half = CMP_LEN // 2

    def w1_half(lo):
        wk = cmp_w1[0, lo * HEAD_DIM:(lo + half) * HEAD_DIM].reshape(half, HEAD_DIM, CMP_HIDDEN)
        wv = cmp_w1[1, lo * HEAD_DIM:(lo + half) * HEAD_DIM].reshape(half, HEAD_DIM, CMP_HIDDEN)
        z = jnp.zeros_like(wk)
        top = jnp.concatenate([wk, z], axis=2)
        bot = jnp.concatenate([z, wv], axis=2)
        return jnp.concatenate([top, bot], axis=1).reshape(half * 2 * HEAD_DIM, 2 * CMP_HIDDEN).astype(BF16)

    def pos_half(lo):
        return jnp.concatenate([cmp_pos[0, lo:lo + half], cmp_pos[1, lo:lo + half]], axis=1).reshape(1, -1)

    lane = np.arange(HEAD_DIM)
    perm = np.where(lane < ROPE_HALF, lane + ROPE_HALF, np.where(lane < 2 * ROPE_HALF, lane - ROPE_HALF, lane))
    w2k, w2v = cmp_w2[0], cmp_w2[1]
    zk = jnp.zeros_like(w2k)
    w2 = jnp.concatenate([
        jnp.concatenate([w2k, w2k, w2k[:, perm], w2k[:, perm], zk, zk], axis=1),
        jnp.concatenate([zk, zk, zk, zk, w2v, w2v], axis=1)], axis=0).astype(BF16)
    b2k, b2v = cmp_b2[0], cmp_b2[1]
    b2 = jnp.concatenate([b2k, b2k, b2k[perm], b2k[perm], b2v, b2v])[None, :]
    b1 = jnp.concatenate([cmp_b1[0], cmp_b1[1]])[None, :]
    return pos_half(0), pos_half(half), w1_half(0), w1_half(half), b1, w2, b2


def _block_diag(pool_w):
    n, c, _ = pool_w.shape
    out = jnp.zeros((n * c, n * c), pool_w.dtype)
    for i in range(n):
        out = out.at[i * c:(i + 1) * c, i * c:(i + 1) * c].set(pool_w[i])
    return out


def kernel(x, norm_g, w_in, fox_f_bias, cmp_pos, cmp_w1, cmp_b1, cmp_w2, cmp_b2, pool_w, pool_scale, conv_w,
           w_branch, w_out, final_norm_g):
    B, T, _ = x.shape
    depth = norm_g.shape[0]
    assert T % TM_PROJ == 0 and T % KV_TILE == 0 and T // SEL_LEN <= HEAD_DIM and T >= WINDOW + CHUNK
    rows = T // CMP_STRIDE
    nblk = T // SEL_LEN

    cc_tok, ss_tok = _rope_tables(jnp.arange(T))
    cmp_end = jnp.arange(rows) * CMP_STRIDE + CMP_LEN - 1
    cc_cmp, ss_cmp = _rope_tables(cmp_end)
    ci = np.arange(rows)[:, None] * CMP_STRIDE
    sj = np.arange(nblk)[None, :] * SEL_LEN
    overlap_t = jnp.asarray(((ci < sj + SEL_LEN) & (ci + CMP_LEN > sj)).T, F32)
    tri = jnp.asarray(np.tril(np.ones((TM_PROJ, TM_PROJ), np.float32)))

    for l in range(depth):
        w1 = _proj_weight(w_in[l])
        fb = jnp.zeros((1, 128), F32).at[0, 12:16].set(fox_f_bias[l])
        (q, ks, kw, vst, vwt, kvc, mt, fq, fk, fvt, o_pool, o_conv) = _proj_call(
            x, norm_g[l][None, :], w1, cc_tok, ss_tok, fb, conv_w[l], _block_diag(pool_w[l]).astype(BF16),
            pool_scale[l][None, :], tri)
        x16 = kvc.reshape(B, rows, CMP_STRIDE * 2 * HEAD_DIM)
        kc, vct = _compress_call(x16, *_compress_weights(cmp_pos[l], cmp_w1[l], cmp_b1[l], cmp_w2[l], cmp_b2[l]),
                                 cc_cmp, ss_cmp)
        o_nsa = _nsa_call(q, ks, kw, vst, vwt, kc, vct, mt, overlap_t)
        o_fox = _fox_call(fq, fk, fvt)
        wg = w_in[l][:, _OFF["gate"]:_OFF["merge"]].astype(BF16)
        wm = w_in[l][:, _OFF["merge"]:_OFF["end"]].astype(BF16)
        x = _merge_call(x, norm_g[l][None, :], wg, wm, w_branch[l].astype(BF16), w_out[l].astype(BF16),
                        final_norm_g[None, :], o_nsa, o_pool, o_conv, o_fox, last=(l == depth - 1))
    return x
```

```python
import functools

import numpy as np
import jax
import jax.numpy as jnp
from jax import lax
from jax.experimental import pallas as pl
from jax.experimental.pallas import tpu as pltpu

F32 = jnp.float32
BF16 = jnp.bfloat16

D_MODEL = 1024
N_BRANCH = 4
BRANCH_WIDTH = 256
HEAD_DIM = 64
N_HEADS = 4
CMP_LEN = 32
CMP_STRIDE = 16
CMP_HIDDEN = 128
SEL_LEN = 64
N_SEL = 16
WINDOW = 512
FORCE_BONUS = 1.0e4
ROPE_THETA = 500000.0
ROPE_HALF = 8
NORM_EPS = 1e-6
MASK_VALUE = -1e30

_OFF = dict(nsa_q=0, k_c=256, v_c=320, k_s=384, v_s=448, k_w=512, v_w=576, nsa_g=640, pool=652,
            conv=908, fox_q=1676, fox_k=1932, fox_v=2188, fox_f=2444, gate=2448, merge=3472, end=7568)

_P_ROPE = 0
_P_VSW = 512
_P_FOXQ = 896
_P_FOXK = 1152
_P_FOXV = 1664
_P_POOL = 1920
_P_CONV = 2176
_P_COLS = 2944

CHUNK = 128
V_ROWS = HEAD_DIM + 16
LOG2E = 1.4426950408889634
Q_SCALE = HEAD_DIM ** -0.5 * LOG2E
TM_PROJ = 512
KV_TILE = 512
NSA_SUB = 2
TQ_FOX = 512
TM_MERGE = 512
N_MISC_T = 16
POOL_HALO = 16
CONV_HALO = 8
VMEM_LIMIT = 56 * 1024 * 1024


def _dot(a, b, precision=None):
    return jnp.dot(a, b, preferred_element_type=F32, precision=precision)


def _sigmoid(x):
    return 1.0 / (1.0 + jnp.exp(-x))


def _rmsnorm(x, g):
    return x * lax.rsqrt(jnp.mean(x * x, axis=-1, keepdims=True) + NORM_EPS) * g


def _flash_t(s, m, l, acc, v_t):
    m_new = jnp.maximum(m, jnp.max(s, axis=0, keepdims=True))
    alpha = jnp.exp2(m - m_new)
    pv = _dot(v_t, jnp.exp2(s - m_new).astype(BF16))
    l = alpha * l + pv[HEAD_DIM:HEAD_DIM + 1]
    acc = alpha * acc + pv[0:HEAD_DIM]
    return m_new, l, acc


def _attend_chunks(streams, n):
    nxt = [st[0](0) for st in streams]
    carries = [st[2] for st in streams]
    for c in range(n):
        cur = nxt
        if c + 1 < n:
            nxt = [st[0](c + 1) for st in streams]
        carries = [_flash_t(cur[i], *carries[i], streams[i][1](c)) for i in range(len(streams))]
    return carries


def _flash_init(n):
    return (jnp.full((1, n), MASK_VALUE, F32), jnp.zeros((1, n), F32), jnp.zeros((HEAD_DIM, n), F32))


def _proj_kernel(x_ref, g_ref, w_ref, cc_ref, ss_ref, fb_ref, cw_ref, pw_ref, ps_ref, tri_ref,
                 q_ref, ks_ref, kw_ref, vst_ref, vwt_ref, kvc_ref, mt_ref, fq_ref, fk_ref, fvt_ref,
                 opool_ref, oconv_ref, pext, cext, ccarry, *, tm):
    ti = pl.program_id(1)
    n_chunk = tm // CHUNK
    x = x_ref[...]
    h = _rmsnorm(x, g_ref[...]).astype(BF16)

    def seg(lo, width):
        return _dot(h, w_ref[:, lo:lo + width])

    lane = lax.broadcasted_iota(jnp.int32, (tm, 128), 1)
    row = lax.broadcasted_iota(jnp.int32, (tm, 128), 0) + ti * tm
    first_half = (lane % HEAD_DIM) < ROPE_HALF
    cc = cc_ref[...]
    ss = ss_ref[...]

    def rope(xc):
        partner = jnp.where(first_half, pltpu.roll(xc, 128 - ROPE_HALF, 1), pltpu.roll(xc, ROPE_HALF, 1))
        return xc * cc + partner * ss

    @pl.when(ti == 0)
    def _():
        ccarry[...] = jnp.zeros_like(ccarry)
        pext[0:POOL_HALO, :] = jnp.zeros((POOL_HALO, BRANCH_WIDTH), F32)
        cext[0:CONV_HALO, :] = jnp.zeros((CONV_HALO, BRANCH_WIDTH), F32)

    g_rope = seg(_P_ROPE, 512)
    g_small = seg(_P_VSW, 384)
    g_fq = seg(_P_FOXQ, 256)
    g_fk = seg(_P_FOXK, 512)
    g_fv = seg(_P_FOXV, 256)
    g_pool = seg(_P_POOL, 256)
    cv = seg(_P_CONV, 768)

    q_ref[:, 0:128] = (rope(g_rope[:, 0:128]) * Q_SCALE).astype(BF16)
    q_ref[:, 128:256] = (rope(g_rope[:, 128:256]) * Q_SCALE).astype(BF16)
    block_onehot = jnp.where(lane - HEAD_DIM == row // SEL_LEN, 1.0, 0.0)
    ks_ref[...] = (rope(g_rope[:, 256:384]) + block_onehot).astype(BF16)
    kw_ref[...] = rope(g_rope[:, 384:512]).astype(BF16)
    vsw_t = jnp.transpose(g_small[:, 0:128]).astype(BF16)
    ones_rows = jnp.ones((V_ROWS - HEAD_DIM, CHUNK), BF16)
    for c in range(n_chunk):
        cols = slice(c * CHUNK, (c + 1) * CHUNK)
        vst_ref[c] = jnp.concatenate([vsw_t[0:HEAD_DIM, cols], ones_rows], axis=0)
        vwt_ref[c] = jnp.concatenate([vsw_t[HEAD_DIM:128, cols], ones_rows], axis=0)
    kvc_ref[...] = g_small[:, 128:256]

    z_t = jnp.transpose(g_small[:, 256:384])[0:N_MISC_T, :]
    gates = _sigmoid(z_t)
    row8 = lax.broadcasted_iota(jnp.int32, (8, tm), 0)
    zb = z_t[8:16] + fb_ref[...]
    logf = jnp.where(row8 >= 4, jnp.minimum(zb, 0.0) - jnp.log(1.0 + jnp.exp(-jnp.abs(zb))), 0.0)

    def split3(v):
        hi = v.astype(BF16).astype(F32)
        mid = (v - hi).astype(BF16).astype(F32)
        return hi, mid, (v - hi) - mid

    parts = jnp.concatenate(list(split3(logf)) + [jnp.zeros((8, tm), F32)], axis=0).astype(BF16)
    part_sums = _dot(parts, tri_ref[...])
    csum = (part_sums[0:8] + part_sums[8:16]) + part_sums[16:24] + ccarry[:, 0:1]
    ccarry[...] = jnp.broadcast_to(csum[:, tm - 1:tm], (8, 128))
    mt_ref[0:8, :] = gates[0:8]
    mt_ref[8:16, :] = jnp.where(row8 < 4, gates[8:16], csum)

    fq_ref[...] = (g_fq * Q_SCALE).astype(BF16)
    terms = split3(csum * (-LOG2E))
    row16 = lax.broadcasted_iota(jnp.int32, (16, tm), 0)
    extra_t = jnp.zeros((16, tm), F32)
    for hd in range(N_HEADS):
        for j in range(3):
            src = jnp.broadcast_to(terms[j][4 + hd:5 + hd, :], (16, tm))
            extra_t = jnp.where(row16 == 3 * hd + j, src, extra_t)
    extra = jnp.transpose(jnp.concatenate(
        [jnp.zeros((HEAD_DIM, tm), F32), extra_t, jnp.zeros((128 - HEAD_DIM - 16, tm), F32)], axis=0))
    for hd in range(N_HEADS):
        fk_ref[:, hd * 128:(hd + 1) * 128] = (g_fk[:, hd * 128:(hd + 1) * 128] + extra).astype(BF16)
    fv_t = jnp.transpose(g_fv).astype(BF16)
    for c in range(n_chunk):
        cols = slice(c * CHUNK, (c + 1) * CHUNK)
        fvt_ref[c] = jnp.concatenate(
            [piece for hd in range(N_HEADS) for piece in (fv_t[hd * HEAD_DIM:(hd + 1) * HEAD_DIM, cols], ones_rows)],
            axis=0)

    pext[POOL_HALO:POOL_HALO + tm, :] = g_pool

    def pld(k, c):
        return pext[pl.ds(POOL_HALO - k, tm), c * 128:(c + 1) * 128]

    left = lane < 64
    rowp1 = (row + 1).astype(F32)

    def cnt(w):
        return jnp.minimum(rowp1, float(w))

    e0 = pld(0, 0)
    s2 = e0 + pld(1, 0)
    s4 = s2 + pld(2, 0) + pld(3, 0)
    p0 = jnp.where(left, s2 / cnt(2), s4 / cnt(4)) - e0
    f0 = pld(0, 1)
    s8 = f0
    for k in range(1, 8):
        s8 = s8 + pld(k, 1)
    s16 = s8
    for k in range(8, 16):
        s16 = s16 + pld(k, 1)
    p1 = jnp.where(left, s8 / cnt(8), s16 / cnt(16)) - f0
    pw = pw_ref[...]
    mixed = _dot(p0.astype(BF16), pw[0:128, :]) + _dot(p1.astype(BF16), pw[128:256, :])
    opool_ref[...] = mixed * ps_ref[...]
    pext[0:POOL_HALO, :] = pext[tm:tm + POOL_HALO, :]

    u = cv[:, 512:768] * cv[:, 0:256]
    cext[CONV_HALO:CONV_HALO + tm, :] = u
    y = (cext[pl.ds(CONV_HALO - 2, tm), :] * cw_ref[0:1, :]
         + cext[pl.ds(CONV_HALO - 1, tm), :] * cw_ref[1:2, :]
         + u * cw_ref[2:3, :])
    oconv_ref[...] = cv[:, 256:512] * y
    cext[0:CONV_HALO, :] = cext[tm:tm + CONV_HALO, :]


def _proj_call(x, g, w1, cc, ss, fb, cw, pw, ps, tri):
    B, T, _ = x.shape
    tm = TM_PROJ
    nt = T // tm
    nck = tm // CHUNK
    tok = lambda width: pl.BlockSpec((None, tm, width), lambda b, t: (b, t, 0))
    chunked = lambda rows: pl.BlockSpec((None, nck, rows, CHUNK), lambda b, t: (b, t, 0, 0))
    const = lambda shape: pl.BlockSpec(shape, lambda b, t: (0,) * len(shape))
    out_shape = [
        jax.ShapeDtypeStruct((B, T, 256), BF16),
        jax.ShapeDtypeStruct((B, T, 128), BF16),
        jax.ShapeDtypeStruct((B, T, 128), BF16),
        jax.ShapeDtypeStruct((B, T // CHUNK, V_ROWS, CHUNK), BF16),
        jax.ShapeDtypeStruct((B, T // CHUNK, V_ROWS, CHUNK), BF16),
        jax.ShapeDtypeStruct((B, T, 128), F32),
        jax.ShapeDtypeStruct((B, nt, N_MISC_T, tm), F32),
        jax.ShapeDtypeStruct((B, T, 256), BF16),
        jax.ShapeDtypeStruct((B, T, 512), BF16),
        jax.ShapeDtypeStruct((B, T // CHUNK, N_HEADS * V_ROWS, CHUNK), BF16),
        jax.ShapeDtypeStruct((B, T, 256), F32),
        jax.ShapeDtypeStruct((B, T, 256), F32),
    ]
    out_specs = [tok(256), tok(128), tok(128), chunked(V_ROWS), chunked(V_ROWS), tok(128),
                 pl.BlockSpec((None, None, N_MISC_T, tm), lambda b, t: (b, t, 0, 0)),
                 tok(256), tok(512), chunked(N_HEADS * V_ROWS), tok(256), tok(256)]
    in_specs = [tok(D_MODEL), const((1, D_MODEL)), const((D_MODEL, _P_COLS)),
                pl.BlockSpec((tm, 128), lambda b, t: (t, 0)), pl.BlockSpec((tm, 128), lambda b, t: (t, 0)),
                const((8, 1)), const((3, 256)), const((256, 256)), const((1, 256)), const((tm, tm))]
    return pl.pallas_call(
        functools.partial(_proj_kernel, tm=tm),
        grid=(B, nt),
        in_specs=in_specs,
        out_specs=out_specs,
        out_shape=out_shape,
        scratch_shapes=[pltpu.VMEM((tm + POOL_HALO, 256), F32), pltpu.VMEM((tm + CONV_HALO, 256), F32),
                        pltpu.VMEM((8, 128), F32)],
        compiler_params=pltpu.CompilerParams(dimension_semantics=("arbitrary", "arbitrary"),
                                             vmem_limit_bytes=VMEM_LIMIT),
        name="proj",
    )(x, g, w1, cc, ss, fb, cw, pw, ps, tri)


def _compress_kernel(x_ref, pt_ref, pb_ref, wt_ref, wb_ref, b1_ref, w2_ref, b2_ref, cc_ref, ss_ref,
                     kc_ref, vct_ref, *, rows):
    x = x_ref[...]
    a = _dot((x + pt_ref[...]).astype(BF16), wt_ref[...])
    b = _dot((x + pb_ref[...]).astype(BF16), wb_ref[...])
    hid = a + pltpu.roll(b, rows - 1, 0) + b1_ref[...]
    act = hid * _sigmoid(hid)
    out = _dot(act.astype(BF16), w2_ref[...]) + b2_ref[...]
    lane = lax.broadcasted_iota(jnp.int32, (rows, 128), 1)
    roped = out[:, 0:128] * cc_ref[...] + out[:, 128:256] * ss_ref[...]
    kc_ref[...] = jnp.where(lane < HEAD_DIM, roped, 0.0).astype(BF16)
    vct_ref[...] = jnp.transpose(out[:, 256:384])[0:HEAD_DIM, :].astype(BF16)


def _compress_call(x16, pt, pb, wt, wb, b1, w2, b2, cc, ss):
    B, rows, width = x16.shape
    const = lambda shape: pl.BlockSpec(shape, lambda b: (0,) * len(shape))
    return pl.pallas_call(
        functools.partial(_compress_kernel, rows=rows),
        grid=(B,),
        in_specs=[pl.BlockSpec((None, rows, width), lambda b: (b, 0, 0)),
                  const((1, width)), const((1, width)), const((width, 256)), const((width, 256)),
                  const((1, 256)), const((256, 384)), const((1, 384)), const((rows, 128)), const((rows, 128))],
        out_specs=[pl.BlockSpec((None, rows, 128), lambda b: (b, 0, 0)),
                   pl.BlockSpec((None, HEAD_DIM, rows), lambda b: (b, 0, 0))],
        out_shape=[jax.ShapeDtypeStruct((B, rows, 128), BF16), jax.ShapeDtypeStruct((B, HEAD_DIM, rows), BF16)],
        compiler_params=pltpu.CompilerParams(dimension_semantics=("arbitrary",),
                                             vmem_limit_bytes=VMEM_LIMIT),
        name="compress",
    )(x16, pt, pb, wt, wb, b1, w2, b2, cc, ss)


def _nsa_kernel(q_ref, ks_ref, kw_ref, vst_ref, vwt_ref, kc_ref, vct_ref, mt_ref, ovt_ref,
                o_ref, key_ref, sbuf, *, nblk, ncmp, seq):
    step = pl.program_id(1)
    tq = CHUNK
    nq = N_HEADS * tq
    hw = nq // 2
    halves = (slice(0, hw), slice(hw, nq))
    n_c = KV_TILE // tq
    n_win = WINDOW // tq
    subs = range(NSA_SUB)
    qis = [step * NSA_SUB + u for u in subs]
    q_local = lax.broadcasted_iota(jnp.int32, (1, nq), 1) % tq
    k_local = lax.broadcasted_iota(jnp.int32, (tq, hw), 0)
    q_loc = q_local[:, 0:hw]
    causal = k_local <= q_loc

    q4, qt_plain, sc, s_diag = [], [], [], []
    for u in subs:
        q_t = jnp.transpose(q_ref[u * tq:(u + 1) * tq, :].astype(F32))
        q4.append(jnp.concatenate([q_t[HEAD_DIM * h:HEAD_DIM * (h + 1)] for h in range(N_HEADS)], axis=1))
        qt_plain.append(jnp.concatenate([q4[u], jnp.zeros((HEAD_DIM, nq), F32)], axis=0).astype(BF16))
        sc.append(_dot(kc_ref[...], qt_plain[u]))
    for u in subs:
        own = ks_ref[pl.ds(pl.multiple_of(qis[u] * tq, tq), tq), :]
        s_diag.append([jnp.where(causal, _dot(own, qt_plain[u][:, hs]), MASK_VALUE) for hs in halves])

    o_cmp, imp = [], []
    n_sub = lax.broadcasted_iota(jnp.int32, (ncmp, nq), 0)
    for u in subs:
        cmask = (n_sub * CMP_STRIDE + (CMP_LEN - 1)) <= qis[u] * tq + q_local
        s = jnp.where(cmask, sc[u], MASK_VALUE)
        mc = jnp.max(s, axis=0, keepdims=True)
        pc = jnp.where(cmask, jnp.exp2(s - mc), 0.0)
        lc = jnp.sum(pc, axis=0, keepdims=True)
        pc = pc / jnp.where(lc > 0.0, lc, 1.0)
        o_cmp.append(_dot(vct_ref[...], pc.astype(BF16)))
        psum = (pc[:, 0:tq] + pc[:, tq:2 * tq]) + (pc[:, 2 * tq:3 * tq] + pc[:, 3 * tq:4 * tq])
        imp.append(_dot(ovt_ref[...], psum, precision=lax.Precision.HIGHEST))

    def win_stream(u, hs):
        qi = qis[u]

        def score(c):
            if c == 0:
                own = kw_ref[pl.ds(pl.multiple_of(qi * tq, tq), tq), :]
                return jnp.where(causal, _dot(own, qt_plain[u][:, hs]), MASK_VALUE)
            ci = qi - n_win + (c - 1)
            start = pl.multiple_of(jnp.maximum(ci, 0) * tq, tq)
            s = _dot(kw_ref[pl.ds(start, tq), :], qt_plain[u][:, hs])
            ok = ((k_local > q_loc) & (ci >= 0)) if c == 1 else jnp.broadcast_to(ci >= 0, (tq, hw))
            return jnp.where(ok, s, MASK_VALUE)

        def value(c):
            return vwt_ref[qi] if c == 0 else vwt_ref[jnp.maximum(qi - n_win + (c - 1), 0)]

        return score, value, _flash_init(hw)

    carries_w = _attend_chunks([win_stream(u, hs) for u in subs for hs in halves], n_win + 1)
    o_win = [jnp.concatenate([acc / l for _, l, acc in carries_w[2 * u:2 * u + 2]], axis=1) for u in subs]

    jb = lax.broadcasted_iota(jnp.int32, (nblk, tq), 0)
    sub8 = lax.broadcasted_iota(jnp.int32, (8, tq), 0)
    qt_sel = []
    for u in subs:
        tt = qis[u] * tq + lax.broadcasted_iota(jnp.int32, (nblk, tq), 1)
        jt = tt // SEL_LEN
        forced = (jb == 0) | (jb == jt) | (jb == jt - 1)
        valid = jb * SEL_LEN <= tt
        key = jnp.where(valid, jnp.where(forced, imp[u] + FORCE_BONUS, imp[u]), MASK_VALUE)
        key_ref[u] = key
        key_rows = [key[8 * v:8 * v + 8] for v in range(nblk // 8)]
        ranks = [jnp.zeros((8, tq), F32) for _ in key_rows]
        for jp in range(nblk):
            other = jnp.broadcast_to(key_ref[u, pl.ds(jp, 1), :], (8, tq))
            for v, kv in enumerate(key_rows):
                if 8 * v > jp:
                    ahead = other >= kv
                elif 8 * v + 7 <= jp:
                    ahead = other > kv
                else:
                    ahead = (other > kv) | ((other == kv) & (sub8 > jp - 8 * v))
                ranks[v] = ranks[v] + jnp.where(ahead, 1.0, 0.0)
        selected = jnp.concatenate(ranks, axis=0) < float(min(N_SEL, nblk))
        sel_bias = jnp.where(selected & (jb < qis[u] * (tq // SEL_LEN)), 0.0, MASK_VALUE)
        if nblk < HEAD_DIM:
            sel_bias = jnp.concatenate([sel_bias, jnp.zeros((HEAD_DIM - nblk, tq), F32)], axis=0)
        qt_sel.append(jnp.concatenate([q4[u], jnp.concatenate([sel_bias] * N_HEADS, axis=1)], axis=0).astype(BF16))

    def slc_scores(u, kt, c):
        start = pl.multiple_of((kt * n_c + c) * tq, tq)
        return _dot(ks_ref[pl.ds(start, tq), :], qt_sel[u])

    for c in range(n_c):
        for u in subs:
            sbuf[u, c] = slc_scores(u, 0, c)
    carries = tuple(tuple(_flash_t(s_diag[u][i], *_flash_init(hw), vst_ref[qis[u]]) for i in range(2)) for u in subs)

    def slc_tile(kt, carries):
        nxt = jnp.minimum(kt + 1, seq // KV_TILE - 1)
        carries = list(carries)
        for c in range(n_c):
            for u in subs:
                s = sbuf[u, c]
                sbuf[u, c] = slc_scores(u, nxt, c)
                carries[u] = tuple(_flash_t(s[:, hs], *carries[u][i], vst_ref[kt * n_c + c])
                                   for i, hs in enumerate(halves))
        return tuple(carries)

    n_past = (qis[-1] * tq + KV_TILE - 1) // KV_TILE
    carries = lax.fori_loop(0, n_past, slc_tile, carries)

    for u in subs:
        o_slc = jnp.concatenate([acc / l for _, l, acc in carries[u]], axis=1)
        g = mt_ref[:, u * tq:(u + 1) * tq]
        heads = []
        for h in range(N_HEADS):
            cols = slice(h * tq, (h + 1) * tq)
            heads.append(g[h:h + 1] * o_cmp[u][:, cols] + g[4 + h:5 + h] * o_slc[:, cols]
                         + g[8 + h:9 + h] * o_win[u][:, cols])
        o_ref[u * tq:(u + 1) * tq, :] = jnp.transpose(jnp.concatenate(heads, axis=0))


def _nsa_call(q, ks, kw, vst, vwt, kc, vct, mt, ovt):
    B, T, _ = q.shape
    nblk, ncmp = ovt.shape
    tq = CHUNK * NSA_SUB
    per_mt = TM_PROJ // tq
    full = lambda width: pl.BlockSpec((None, T, width), lambda b, i: (b, 0, 0))
    vfull = pl.BlockSpec((None, T // CHUNK, V_ROWS, CHUNK), lambda b, i: (b, 0, 0, 0))
    return pl.pallas_call(
        functools.partial(_nsa_kernel, nblk=nblk, ncmp=ncmp, seq=T),
        grid=(B, T // tq),
        in_specs=[pl.BlockSpec((None, tq, 256), lambda b, i: (b, i, 0)),
                  full(128), full(128), vfull, vfull,
                  pl.BlockSpec((None, ncmp, 128), lambda b, i: (b, 0, 0)),
                  pl.BlockSpec((None, HEAD_DIM, ncmp), lambda b, i: (b, 0, 0)),
                  pl.BlockSpec((None, None, N_MISC_T, tq), lambda b, i: (b, i // per_mt, 0, i % per_mt)),
                  pl.BlockSpec((nblk, ncmp), lambda b, i: (0, 0))],
        out_specs=pl.BlockSpec((None, tq, 256), lambda b, i: (b, i, 0)),
        out_shape=jax.ShapeDtypeStruct((B, T, 256), F32),
        scratch_shapes=[pltpu.VMEM((NSA_SUB, nblk, CHUNK), F32),
                        pltpu.VMEM((NSA_SUB, KV_TILE // CHUNK, CHUNK, N_HEADS * CHUNK), F32)],
        compiler_params=pltpu.CompilerParams(dimension_semantics=("arbitrary", "arbitrary"),
                                             vmem_limit_bytes=VMEM_LIMIT),
        name="nsa",
    )(q, ks, kw, vst, vwt, kc, vct, mt, ovt)


def _fox_kernel(q_ref, k_ref, vt_ref, o_ref, sbuf, *, seq):
    qi = pl.program_id(1)
    nq = TQ_FOX
    n_chunk = KV_TILE // CHUNK
    t_lane = qi * nq + lax.broadcasted_iota(jnp.int32, (1, nq), 1)
    k_local = lax.broadcasted_iota(jnp.int32, (CHUNK, nq), 0)
    q_t = jnp.transpose(q_ref[...].astype(F32))
    extra_row = lax.broadcasted_iota(jnp.int32, (HEAD_DIM, nq), 0)

    def ones_rows(h):
        return jnp.where((extra_row >= 3 * h) & (extra_row < 3 * h + 3), 1.0, 0.0)

    qts = [jnp.concatenate([q_t[HEAD_DIM * h:HEAD_DIM * (h + 1)], ones_rows(h)], axis=0).astype(BF16)
           for h in range(N_HEADS)]

    def scores(kt, c, h):
        start = pl.multiple_of((kt * n_chunk + c) * CHUNK, CHUNK)
        return _dot(k_ref[pl.ds(start, CHUNK), h * 128:(h + 1) * 128], qts[h]), start

    def value(kt, c, h):
        return vt_ref[kt * n_chunk + c, h * V_ROWS:(h + 1) * V_ROWS, :]

    def diag_stream(h):
        def score(c):
            s, start = scores(qi, c, h)
            return jnp.where(start + k_local <= t_lane, s, MASK_VALUE)

        return score, lambda c: value(qi, c, h), _flash_init(nq)

    carries = tuple(_attend_chunks([diag_stream(h) for h in range(N_HEADS)], n_chunk))

    for c in range(n_chunk):
        for h in range(N_HEADS):
            sbuf[h, c] = scores(0, c, h)[0]

    def tile(kt, carries):
        nxt = jnp.minimum(kt + 1, seq // KV_TILE - 1)
        carries = list(carries)
        for c in range(n_chunk):
            for h in range(N_HEADS):
                s = sbuf[h, c]
                sbuf[h, c] = scores(nxt, c, h)[0]
                carries[h] = _flash_t(s, *carries[h], value(kt, c, h))
        return tuple(carries)

    carries = lax.fori_loop(0, qi, tile, carries)
    o_ref[...] = jnp.transpose(jnp.concatenate([acc / l for _, l, acc in carries], axis=0))


def _fox_call(fq, fk, fvt):
    B, T, _ = fq.shape
    tq = TQ_FOX
    return pl.pallas_call(
        functools.partial(_fox_kernel, seq=T),
        grid=(B, T // tq),
        in_specs=[pl.BlockSpec((None, tq, 256), lambda b, i: (b, i, 0)),
                  pl.BlockSpec((None, T, 512), lambda b, i: (b, 0, 0)),
                  pl.BlockSpec((None, T // CHUNK, N_HEADS * V_ROWS, CHUNK), lambda b, i: (b, 0, 0, 0))],
        out_specs=pl.BlockSpec((None, tq, 256), lambda b, i: (b, i, 0)),
        out_shape=jax.ShapeDtypeStruct((B, T, 256), F32),
        scratch_shapes=[pltpu.VMEM((N_HEADS, KV_TILE // CHUNK, CHUNK, tq), F32)],
        compiler_params=pltpu.CompilerParams(dimension_semantics=("arbitrary", "arbitrary"),
                                             vmem_limit_bytes=VMEM_LIMIT),
        name="fox",
    )(fq, fk, fvt)


def _merge_kernel(x_ref, g_ref, wg_ref, wm_ref, wb_ref, wo_ref, fg_ref, o0_ref, o1_ref, o2_ref, o3_ref,
                  out_ref, *, last):
    x = x_ref[...]
    h = _rmsnorm(x, g_ref[...]).astype(BF16)
    acc = None
    for i, o_ref in enumerate((o0_ref, o1_ref, o2_ref, o3_ref)):
        gate = _dot(h, wg_ref[:, i * BRANCH_WIDTH:(i + 1) * BRANCH_WIDTH])
        a = (o_ref[...] * (gate * _sigmoid(gate))).astype(BF16)
        branch = _dot(a, wb_ref[i])
        merge = _dot(h, wm_ref[:, i * D_MODEL:(i + 1) * D_MODEL])
        term = _sigmoid(merge) * branch
        acc = term if acc is None else acc + term
    y = x + _dot(acc.astype(BF16), wo_ref[...])
    if last:
        y = _rmsnorm(y, fg_ref[...])
    out_ref[...] = y


def _merge_call(x, g, wg, wm, wb, wo, fg, o_nsa, o_pool, o_conv, o_fox, last):
    B, T, _ = x.shape
    tm = TM_MERGE
    tok = lambda width: pl.BlockSpec((None, tm, width), lambda b, t: (b, t, 0))
    const = lambda shape: pl.BlockSpec(shape, lambda b, t: (0,) * len(shape))
    return pl.pallas_call(
        functools.partial(_merge_kernel, last=last),
        grid=(B, T // tm),
        in_specs=[tok(D_MODEL), const((1, D_MODEL)), const((D_MODEL, N_BRANCH * BRANCH_WIDTH)),
                  const((D_MODEL, N_BRANCH * D_MODEL)), const((N_BRANCH, BRANCH_WIDTH, D_MODEL)),
                  const((D_MODEL, D_MODEL)), const((1, D_MODEL)), tok(256), tok(256), tok(256), tok(256)],
        out_specs=tok(D_MODEL),
        out_shape=jax.ShapeDtypeStruct((B, T, D_MODEL), F32),
        compiler_params=pltpu.CompilerParams(dimension_semantics=("arbitrary", "arbitrary"),
                                             vmem_limit_bytes=VMEM_LIMIT),
        name="merge",
    )(x, g, wg, wm, wb, wo, fg, o_nsa, o_pool, o_conv, o_fox)


def _rope_tables(pos):
    n = pos.shape[0]
    inv = ROPE_THETA ** (-jnp.arange(ROPE_HALF, dtype=F32) / ROPE_HALF)
    ang = pos.astype(F32)[:, None] * inv[None, :]
    cos, sin = jnp.cos(ang), jnp.sin(ang)
    rest = HEAD_DIM - 2 * ROPE_HALF
    c = jnp.concatenate([cos, cos, jnp.ones((n, rest), F32)], axis=1)
    s = jnp.concatenate([-sin, sin, jnp.zeros((n, rest), F32)], axis=1)
    return jnp.concatenate([c, c], axis=1), jnp.concatenate([s, s], axis=1)


def _proj_weight(w_in):
    col = lambda name, width: w_in[:, _OFF[name]:_OFF[name] + width]
    z64 = jnp.zeros((D_MODEL, HEAD_DIM), F32)
    misc = jnp.concatenate([col("nsa_g", 12), col("fox_f", 4), jnp.zeros((D_MODEL, 112), F32)], axis=1)
    fox_k = []
    for h in range(N_HEADS):
        fox_k += [w_in[:, _OFF["fox_k"] + h * HEAD_DIM:_OFF["fox_k"] + (h + 1) * HEAD_DIM], z64]
    w = jnp.concatenate([col("nsa_q", 256), col("k_s", 64), z64, col("k_w", 64), z64,
                         col("v_s", 64), col("v_w", 64), col("k_c", 128), misc,
                         col("fox_q", 256)] + fox_k + [col("fox_v", 256), col("pool", 256), col("conv", 768)],
                        axis=1)
    assert w.shape[1] == _P_COLS
    return w.astype(BF16)


def _compress_weights(cmp_pos, cmp_w1, cmp_b1, cmp_w2, cmp_b2):
    half = CMP_LEN // 2

    def w1_half(lo):
        wk = cmp_w1[0, lo * HEAD_DIM:(lo + half) * HEAD_DIM].reshape(half, HEAD_DIM, CMP_HIDDEN)
        wv = cmp_w1[1, lo * HEAD_DIM:(lo + half) * HEAD_DIM].reshape(half, HEAD_DIM, CMP_HIDDEN)
        z = jnp.zeros_like(wk)
        top = jnp.concatenate([wk, z], axis=2)
        bot = jnp.concatenate([z, wv], axis=2)
        return jnp.concatenate([top, bot], axis=1).reshape(half * 2 * HEAD_DIM, 2 * CMP_HIDDEN).astype(BF16)

    def pos_half(lo):
        return jnp.concatenate([cmp_pos[0, lo:lo + half], cmp_pos[1, lo:lo + half]], axis=1).reshape(1, -1)

    lane = np.arange(HEAD_DIM)
    perm = np.where(lane < ROPE_HALF, lane + ROPE_HALF, np.where(lane < 2 * ROPE_HALF, lane - ROPE_HALF, lane))
    w2k, w2v = cmp_w2[0], cmp_w2[1]
    zk = jnp.zeros_like(w2k)
    w2 = jnp.concatenate([
        jnp.concatenate([w2k, w2k, w2k[:, perm], w2k[:, perm], zk, zk], axis=1),
        jnp.concatenate([zk, zk, zk, zk, w2v, w2v], axis=1)], axis=0).astype(BF16)
    b2k, b2v = cmp_b2[0], cmp_b2[1]
    b2 = jnp.concatenate([b2k, b2k, b2k[perm], b2k[perm], b2v, b2v])[None, :]
    b1 = jnp.concatenate([cmp_b1[0], cmp_b1[1]])[None, :]
    return pos_half(0), pos_half(half), w1_half(0), w1_half(half), b1, w2, b2


def _block_diag(pool_w):
    n, c, _ = pool_w.shape
    out = jnp.zeros((n * c, n * c), pool_w.dtype)
    for i in range(n):
        out = out.at[i * c:(i + 1) * c, i * c:(i + 1) * c].set(pool_w[i])
    return out


def kernel(x, norm_g, w_in, fox_f_bias, cmp_pos, cmp_w1, cmp_b1, cmp_w2, cmp_b2, pool_w, pool_scale, conv_w,
           w_branch, w_out, final_norm_g):
    B, T, _ = x.shape
    depth = norm_g.shape[0]
    assert T % TM_PROJ == 0 and T % KV_TILE == 0 and T // SEL_LEN <= HEAD_DIM and T >= WINDOW + CHUNK
    rows = T // CMP_STRIDE
    nblk = T // SEL_LEN

    cc_tok, ss_tok = _rope_tables(jnp.arange(T))
    cmp_end = jnp.arange(rows) * CMP_STRIDE + CMP_LEN - 1
    cc_cmp, ss_cmp = _rope_tables(cmp_end)
    ci = np.arange(rows)[:, None] * CMP_STRIDE
    sj = np.arange(nblk)[None, :] * SEL_LEN
    overlap_t = jnp.asarray(((ci < sj + SEL_LEN) & (ci + CMP_LEN > sj)).T, F32)
    tri = jnp.asarray(np.triu(np.ones((TM_PROJ, TM_PROJ), np.float32)), BF16)

    for l in range(depth):
        w1 = _proj_weight(w_in[l])
        fb = jnp.zeros((8, 1), F32).at[4:8, 0].set(fox_f_bias[l])
        (q, ks, kw, vst, vwt, kvc, mt, fq, fk, fvt, o_pool, o_conv) = _proj_call(
            x, norm_g[l][None, :], w1, cc_tok, ss_tok, fb, conv_w[l], _block_diag(pool_w[l]).astype(BF16),
            pool_scale[l][None, :], tri)
        x16 = kvc.reshape(B, rows, CMP_STRIDE * 2 * HEAD_DIM)
        kc, vct = _compress_call(x16, *_compress_weights(cmp_pos[l], cmp_w1[l], cmp_b1[l], cmp_w2[l], cmp_b2[l]),
                                 cc_cmp, ss_cmp)
        o_nsa = _nsa_call(q, ks, kw, vst, vwt, kc, vct, mt, overlap_t)
        o_fox = _fox_call(fq, fk, fvt)
        wg = w_in[l][:, _OFF["gate"]:_OFF["merge"]].astype(BF16)
        wm = w_in[l][:, _OFF["merge"]:_OFF["end"]].astype(BF16)
        x = _merge_call(x, norm_g[l][None, :], wg, wm, w_branch[l].astype(BF16), w_out[l].astype(BF16),
                        final_norm_g[None, :], o_nsa, o_pool, o_conv, o_fox, last=(l == depth - 1))
    return x
```

```python
import functools

import numpy as np
import jax
import jax.numpy as jnp
from jax import lax
from jax.experimental import pallas as pl
from jax.experimental.pallas import tpu as pltpu

F32 = jnp.float32
BF16 = jnp.bfloat16

D_MODEL = 1024
N_BRANCH = 4
BRANCH_WIDTH = 256
HEAD_DIM = 64
N_HEADS = 4
CMP_LEN = 32
CMP_STRIDE = 16
CMP_HIDDEN = 128
SEL_LEN = 64
N_SEL = 16
WINDOW = 512
FORCE_BONUS = 1.0e4
ROPE_THETA = 500000.0
ROPE_HALF = 8
NORM_EPS = 1e-6
MASK_VALUE = -1e30

_OFF = dict(nsa_q=0, k_c=256, v_c=320, k_s=384, v_s=448, k_w=512, v_w=576, nsa_g=640, pool=652,
            conv=908, fox_q=1676, fox_k=1932, fox_v=2188, fox_f=2444, gate=2448, merge=3472, end=7568)

_P_ROPE = 0
_P_VSW = 512
_P_FOXQ = 896
_P_FOXK = 1152
_P_FOXV = 1664
_P_POOL = 1920
_P_CONV = 2176
_P_COLS = 2944

CHUNK = 128
UPD = 256
V_ROWS = HEAD_DIM + 16
LOG2E = 1.4426950408889634
Q_SCALE = HEAD_DIM ** -0.5 * LOG2E
TM_PROJ = 512
KV_TILE = 512
NSA_SUB = 2
TQ_FOX = 512
TM_MERGE = 512
N_MISC_T = 16
POOL_HALO = 16
CONV_HALO = 8
VMEM_LIMIT = 56 * 1024 * 1024


def _dot(a, b, precision=None):
    return jnp.dot(a, b, preferred_element_type=F32, precision=precision)


def _sigmoid(x):
    return 1.0 / (1.0 + jnp.exp(-x))


def _rmsnorm(x, g):
    return x * lax.rsqrt(jnp.mean(x * x, axis=-1, keepdims=True) + NORM_EPS) * g


def _flash_t(s, m, l, acc, v_t):
    m_new = jnp.maximum(m, jnp.max(s, axis=0, keepdims=True))
    alpha = jnp.exp2(m - m_new)
    pv = _dot(v_t, jnp.exp2(s - m_new).astype(BF16))
    l = alpha * l + pv[HEAD_DIM:HEAD_DIM + 1]
    acc = alpha * acc + pv[0:HEAD_DIM]
    return m_new, l, acc


def _attend_chunks(streams, n):
    nxt = [st[0](0) for st in streams]
    carries = [st[2] for st in streams]
    for c in range(n):
        cur = nxt
        if c + 1 < n:
            nxt = [st[0](c + 1) for st in streams]
        carries = [_flash_t(cur[i], *carries[i], streams[i][1](c)) for i in range(len(streams))]
    return carries


def _flash_init(n):
    return (jnp.full((1, n), MASK_VALUE, F32), jnp.zeros((1, n), F32), jnp.zeros((HEAD_DIM, n), F32))


def _proj_kernel(x_ref, g_ref, w_ref, cc_ref, ss_ref, fb_ref, cw_ref, pw_ref, ps_ref, tri_ref,
                 q_ref, ks_ref, kw_ref, vst_ref, vwt_ref, kvc_ref, mt_ref, fq_ref, fk_ref, fvt_ref,
                 opool_ref, oconv_ref, pext, cext, ccarry, *, tm):
    ti = pl.program_id(1)
    x = x_ref[...]
    h = _rmsnorm(x, g_ref[...]).astype(BF16)

    def seg(lo, width):
        return _dot(h, w_ref[:, lo:lo + width])

    lane = lax.broadcasted_iota(jnp.int32, (tm, 128), 1)
    row = lax.broadcasted_iota(jnp.int32, (tm, 128), 0) + ti * tm
    first_half = (lane % HEAD_DIM) < ROPE_HALF
    cc = cc_ref[...]
    ss = ss_ref[...]

    def rope(xc):
        partner = jnp.where(first_half, pltpu.roll(xc, 128 - ROPE_HALF, 1), pltpu.roll(xc, ROPE_HALF, 1))
        return xc * cc + partner * ss

    @pl.when(ti == 0)
    def _():
        ccarry[...] = jnp.zeros_like(ccarry)
        pext[0:POOL_HALO, :] = jnp.zeros((POOL_HALO, BRANCH_WIDTH), F32)
        cext[0:CONV_HALO, :] = jnp.zeros((CONV_HALO, BRANCH_WIDTH), F32)

    g_rope = seg(_P_ROPE, 512)
    g_small = seg(_P_VSW, 384)
    g_fq = seg(_P_FOXQ, 256)
    g_fk = seg(_P_FOXK, 512)
    g_fv = seg(_P_FOXV, 256)
    g_pool = seg(_P_POOL, 256)
    cv = seg(_P_CONV, 768)

    q_ref[:, 0:128] = (rope(g_rope[:, 0:128]) * Q_SCALE).astype(BF16)
    q_ref[:, 128:256] = (rope(g_rope[:, 128:256]) * Q_SCALE).astype(BF16)
    block_onehot = jnp.where(lane - HEAD_DIM == row // SEL_LEN, 1.0, 0.0)
    ks_ref[...] = (rope(g_rope[:, 256:384]) + block_onehot).astype(BF16)
    kw_ref[...] = rope(g_rope[:, 384:512]).astype(BF16)
    vsw_t = jnp.transpose(g_small[:, 0:128]).astype(BF16)
    ones_upd = jnp.ones((V_ROWS - HEAD_DIM, UPD), BF16)
    for c in range(tm // UPD):
        vst_ref[c] = jnp.concatenate([vsw_t[0:HEAD_DIM, c * UPD:(c + 1) * UPD], ones_upd], axis=0)
    for c in range(tm // CHUNK):
        vwt_ref[c] = jnp.concatenate([vsw_t[HEAD_DIM:128, c * CHUNK:(c + 1) * CHUNK], ones_upd[:, 0:CHUNK]], axis=0)
    kvc_ref[...] = g_small[:, 128:256]

    z_t = jnp.transpose(g_small[:, 256:384])[0:N_MISC_T, :]
    gates = _sigmoid(z_t)
    row8 = lax.broadcasted_iota(jnp.int32, (8, tm), 0)
    zb = z_t[8:16] + fb_ref[...]
    logf = jnp.where(row8 >= 4, jnp.minimum(zb, 0.0) - jnp.log(1.0 + jnp.exp(-jnp.abs(zb))), 0.0)

    def split3(v):
        hi = v.astype(BF16).astype(F32)
        mid = (v - hi).astype(BF16).astype(F32)
        return hi, mid, (v - hi) - mid

    parts = jnp.concatenate(list(split3(logf)) + [jnp.zeros((8, tm), F32)], axis=0).astype(BF16)
    part_sums = _dot(parts, tri_ref[...])
    csum = (part_sums[0:8] + part_sums[8:16]) + part_sums[16:24] + ccarry[:, 0:1]
    ccarry[...] = jnp.broadcast_to(csum[:, tm - 1:tm], (8, 128))
    mt_ref[0:8, :] = gates[0:8]
    mt_ref[8:16, :] = jnp.where(row8 < 4, gates[8:16], csum)

    fq_ref[...] = (g_fq * Q_SCALE).astype(BF16)
    terms = split3(csum * (-LOG2E))
    row16 = lax.broadcasted_iota(jnp.int32, (16, tm), 0)
    extra_t = jnp.zeros((16, tm), F32)
    for hd in range(N_HEADS):
        for j in range(3):
            src = jnp.broadcast_to(terms[j][4 + hd:5 + hd, :], (16, tm))
            extra_t = jnp.where(row16 == 3 * hd + j, src, extra_t)
    extra = jnp.transpose(jnp.concatenate(
        [jnp.zeros((HEAD_DIM, tm), F32), extra_t, jnp.zeros((128 - HEAD_DIM - 16, tm), F32)], axis=0))
    for hd in range(N_HEADS):
        fk_ref[:, hd * 128:(hd + 1) * 128] = (g_fk[:, hd * 128:(hd + 1) * 128] + extra).astype(BF16)
    fv_t = jnp.transpose(g_fv).astype(BF16)
    for c in range(tm // UPD):
        cols = slice(c * UPD, (c + 1) * UPD)
        fvt_ref[c] = jnp.concatenate(
            [piece for hd in range(N_HEADS) for piece in (fv_t[hd * HEAD_DIM:(hd + 1) * HEAD_DIM, cols], ones_upd)],
            axis=0)

    pext[POOL_HALO:POOL_HALO + tm, :] = g_pool

    def pld(k, c):
        return pext[pl.ds(POOL_HALO - k, tm), c * 128:(c + 1) * 128]

    left = lane < 64
    rowp1 = (row + 1).astype(F32)

    def cnt(w):
        return jnp.minimum(rowp1, float(w))

    e0 = pld(0, 0)
    s2 = e0 + pld(1, 0)
    s4 = s2 + pld(2, 0) + pld(3, 0)
    p0 = jnp.where(left, s2 / cnt(2), s4 / cnt(4)) - e0
    f0 = pld(0, 1)
    s8 = f0
    for k in range(1, 8):
        s8 = s8 + pld(k, 1)
    s16 = s8
    for k in range(8, 16):
        s16 = s16 + pld(k, 1)
    p1 = jnp.where(left, s8 / cnt(8), s16 / cnt(16)) - f0
    pw = pw_ref[...]
    mixed = _dot(p0.astype(BF16), pw[0:128, :]) + _dot(p1.astype(BF16), pw[128:256, :])
    opool_ref[...] = mixed * ps_ref[...]
    pext[0:POOL_HALO, :] = pext[tm:tm + POOL_HALO, :]

    u = cv[:, 512:768] * cv[:, 0:256]
    cext[CONV_HALO:CONV_HALO + tm, :] = u
    y = (cext[pl.ds(CONV_HALO - 2, tm), :] * cw_ref[0:1, :]
         + cext[pl.ds(CONV_HALO - 1, tm), :] * cw_ref[1:2, :]
         + u * cw_ref[2:3, :])
    oconv_ref[...] = cv[:, 256:512] * y
    cext[0:CONV_HALO, :] = cext[tm:tm + CONV_HALO, :]


def _proj_call(x, g, w1, cc, ss, fb, cw, pw, ps, tri):
    B, T, _ = x.shape
    tm = TM_PROJ
    nt = T // tm
    tok = lambda width: pl.BlockSpec((None, tm, width), lambda b, t: (b, t, 0))
    chunked = lambda rows, keys: pl.BlockSpec((None, tm // keys, rows, keys), lambda b, t: (b, t, 0, 0))
    const = lambda shape: pl.BlockSpec(shape, lambda b, t: (0,) * len(shape))
    out_shape = [
        jax.ShapeDtypeStruct((B, T, 256), BF16),
        jax.ShapeDtypeStruct((B, T, 128), BF16),
        jax.ShapeDtypeStruct((B, T, 128), BF16),
        jax.ShapeDtypeStruct((B, T // UPD, V_ROWS, UPD), BF16),
        jax.ShapeDtypeStruct((B, T // CHUNK, V_ROWS, CHUNK), BF16),
        jax.ShapeDtypeStruct((B, T, 128), F32),
        jax.ShapeDtypeStruct((B, nt, N_MISC_T, tm), F32),
        jax.ShapeDtypeStruct((B, T, 256), BF16),
        jax.ShapeDtypeStruct((B, T, 512), BF16),
        jax.ShapeDtypeStruct((B, T // UPD, N_HEADS * V_ROWS, UPD), BF16),
        jax.ShapeDtypeStruct((B, T, 256), F32),
        jax.ShapeDtypeStruct((B, T, 256), F32),
    ]
    out_specs = [tok(256), tok(128), tok(128), chunked(V_ROWS, UPD), chunked(V_ROWS, CHUNK), tok(128),
                 pl.BlockSpec((None, None, N_MISC_T, tm), lambda b, t: (b, t, 0, 0)),
                 tok(256), tok(512), chunked(N_HEADS * V_ROWS, UPD), tok(256), tok(256)]
    in_specs = [tok(D_MODEL), const((1, D_MODEL)), const((D_MODEL, _P_COLS)),
                pl.BlockSpec((tm, 128), lambda b, t: (t, 0)), pl.BlockSpec((tm, 128), lambda b, t: (t, 0)),
                const((8, 1)), const((3, 256)), const((256, 256)), const((1, 256)), const((tm, tm))]
    return pl.pallas_call(
        functools.partial(_proj_kernel, tm=tm),
        grid=(B, nt),
        in_specs=in_specs,
        out_specs=out_specs,
        out_shape=out_shape,
        scratch_shapes=[pltpu.VMEM((tm + POOL_HALO, 256), F32), pltpu.VMEM((tm + CONV_HALO, 256), F32),
                        pltpu.VMEM((8, 128), F32)],
        compiler_params=pltpu.CompilerParams(dimension_semantics=("arbitrary", "arbitrary"),
                                             vmem_limit_bytes=VMEM_LIMIT),
        name="proj",
    )(x, g, w1, cc, ss, fb, cw, pw, ps, tri)


def _compress_kernel(x_ref, pt_ref, pb_ref, wt_ref, wb_ref, b1_ref, w2_ref, b2_ref, cc_ref, ss_ref,
                     kc_ref, vct_ref, *, rows):
    x = x_ref[...]
    a = _dot((x + pt_ref[...]).astype(BF16), wt_ref[...])
    b = _dot((x + pb_ref[...]).astype(BF16), wb_ref[...])
    hid = a + pltpu.roll(b, rows - 1, 0) + b1_ref[...]
    act = hid * _sigmoid(hid)
    out = _dot(act.astype(BF16), w2_ref[...]) + b2_ref[...]
    lane = lax.broadcasted_iota(jnp.int32, (rows, 128), 1)
    roped = out[:, 0:128] * cc_ref[...] + out[:, 128:256] * ss_ref[...]
    kc_ref[...] = jnp.where(lane < HEAD_DIM, roped, 0.0).astype(BF16)
    vct_ref[...] = jnp.transpose(out[:, 256:384])[0:HEAD_DIM, :].astype(BF16)


def _compress_call(x16, pt, pb, wt, wb, b1, w2, b2, cc, ss):
    B, rows, width = x16.shape
    const = lambda shape: pl.BlockSpec(shape, lambda b: (0,) * len(shape))
    return pl.pallas_call(
        functools.partial(_compress_kernel, rows=rows),
        grid=(B,),
        in_specs=[pl.BlockSpec((None, rows, width), lambda b: (b, 0, 0)),
                  const((1, width)), const((1, width)), const((width, 256)), const((width, 256)),
                  const((1, 256)), const((256, 384)), const((1, 384)), const((rows, 128)), const((rows, 128))],
        out_specs=[pl.BlockSpec((None, rows, 128), lambda b: (b, 0, 0)),
                   pl.BlockSpec((None, HEAD_DIM, rows), lambda b: (b, 0, 0))],
        out_shape=[jax.ShapeDtypeStruct((B, rows, 128), BF16), jax.ShapeDtypeStruct((B, HEAD_DIM, rows), BF16)],
        compiler_params=pltpu.CompilerParams(dimension_semantics=("arbitrary",),
                                             vmem_limit_bytes=VMEM_LIMIT),
        name="compress",
    )(x16, pt, pb, wt, wb, b1, w2, b2, cc, ss)


def _nsa_kernel(q_ref, ks_ref, kw_ref, vst_ref, vwt_ref, kc_ref, vct_ref, mt_ref, ovt_ref,
                o_ref, key_ref, sbuf, *, nblk, ncmp, seq):
    step = pl.program_id(1)
    tq = CHUNK
    nq = N_HEADS * tq
    hw = nq // 2
    halves = (slice(0, hw), slice(hw, nq))
    n_c = KV_TILE // UPD
    n_win = WINDOW // tq
    subs = range(NSA_SUB)
    qis = [step * NSA_SUB + u for u in subs]
    q_local = lax.broadcasted_iota(jnp.int32, (1, nq), 1) % tq
    k_local = lax.broadcasted_iota(jnp.int32, (tq, hw), 0)
    q_loc = q_local[:, 0:hw]
    causal = k_local <= q_loc

    q4, qt_plain, sc, s_diag = [], [], [], []
    for u in subs:
        q_t = jnp.transpose(q_ref[u * tq:(u + 1) * tq, :].astype(F32))
        q4.append(jnp.concatenate([q_t[HEAD_DIM * h:HEAD_DIM * (h + 1)] for h in range(N_HEADS)], axis=1))
        qt_plain.append(jnp.concatenate([q4[u], jnp.zeros((HEAD_DIM, nq), F32)], axis=0).astype(BF16))
        sc.append(_dot(kc_ref[...], qt_plain[u]))
    for u in subs:
        own = ks_ref[pl.ds(pl.multiple_of(qis[u] * tq, tq), tq), :]
        s_diag.append([jnp.where(causal, _dot(own, qt_plain[u][:, hs]), MASK_VALUE) for hs in halves])

    o_cmp, imp = [], []
    n_sub = lax.broadcasted_iota(jnp.int32, (ncmp, nq), 0)
    for u in subs:
        cmask = (n_sub * CMP_STRIDE + (CMP_LEN - 1)) <= qis[u] * tq + q_local
        s = jnp.where(cmask, sc[u], MASK_VALUE)
        mc = jnp.max(s, axis=0, keepdims=True)
        pc = jnp.where(cmask, jnp.exp2(s - mc), 0.0)
        lc = jnp.sum(pc, axis=0, keepdims=True)
        pc = pc / jnp.where(lc > 0.0, lc, 1.0)
        o_cmp.append(_dot(vct_ref[...], pc.astype(BF16)))
        psum = (pc[:, 0:tq] + pc[:, tq:2 * tq]) + (pc[:, 2 * tq:3 * tq] + pc[:, 3 * tq:4 * tq])
        imp.append(_dot(ovt_ref[...], psum, precision=lax.Precision.HIGHEST))

    def win_stream(u, hs):
        qi = qis[u]

        def score(c):
            if c == 0:
                own = kw_ref[pl.ds(pl.multiple_of(qi * tq, tq), tq), :]
                return jnp.where(causal, _dot(own, qt_plain[u][:, hs]), MASK_VALUE)
            ci = qi - n_win + (c - 1)
            start = pl.multiple_of(jnp.maximum(ci, 0) * tq, tq)
            s = _dot(kw_ref[pl.ds(start, tq), :], qt_plain[u][:, hs])
            ok = ((k_local > q_loc) & (ci >= 0)) if c == 1 else jnp.broadcast_to(ci >= 0, (tq, hw))
            return jnp.where(ok, s, MASK_VALUE)

        def value(c):
            return vwt_ref[qi] if c == 0 else vwt_ref[jnp.maximum(qi - n_win + (c - 1), 0)]

        return score, value, _flash_init(hw)

    carries_w = _attend_chunks([win_stream(u, hs) for u in subs for hs in halves], n_win + 1)
    o_win = [jnp.concatenate([acc / l for _, l, acc in carries_w[2 * u:2 * u + 2]], axis=1) for u in subs]

    jb = lax.broadcasted_iota(jnp.int32, (nblk, tq), 0)
    sub8 = lax.broadcasted_iota(jnp.int32, (8, tq), 0)
    qt_sel = []
    for u in subs:
        tt = qis[u] * tq + lax.broadcasted_iota(jnp.int32, (nblk, tq), 1)
        jt = tt // SEL_LEN
        forced = (jb == 0) | (jb == jt) | (jb == jt - 1)
        valid = jb * SEL_LEN <= tt
        key = jnp.where(valid, jnp.where(forced, imp[u] + FORCE_BONUS, imp[u]), MASK_VALUE)
        key_ref[u] = key
        key_rows = [key[8 * v:8 * v + 8] for v in range(nblk // 8)]
        ranks = [jnp.zeros((8, tq), F32) for _ in key_rows]
        for jp in range(nblk):
            other = jnp.broadcast_to(key_ref[u, pl.ds(jp, 1), :], (8, tq))
            for v, kv in enumerate(key_rows):
                if 8 * v > jp:
                    ahead = other >= kv
                elif 8 * v + 7 <= jp:
                    ahead = other > kv
                else:
                    ahead = (other > kv) | ((other == kv) & (sub8 > jp - 8 * v))
                ranks[v] = ranks[v] + jnp.where(ahead, 1.0, 0.0)
        selected = jnp.concatenate(ranks, axis=0) < float(min(N_SEL, nblk))
        sel_bias = jnp.where(selected & (jb < qis[u] * (tq // SEL_LEN)), 0.0, MASK_VALUE)
        if nblk < HEAD_DIM:
            sel_bias = jnp.concatenate([sel_bias, jnp.zeros((HEAD_DIM - nblk, tq), F32)], axis=0)
        qt_sel.append(jnp.concatenate([q4[u], jnp.concatenate([sel_bias] * N_HEADS, axis=1)], axis=0).astype(BF16))

    def slc_scores(u, kt, c):
        start = pl.multiple_of((kt * n_c + c) * UPD, UPD)
        return _dot(ks_ref[pl.ds(start, UPD), :], qt_sel[u])

    for c in range(n_c):
        for u in subs:
            sbuf[u, c] = slc_scores(u, 0, c)
    carries = tuple(tuple(_flash_t(s_diag[u][i], *_flash_init(hw), vst_ref[step][:, u * tq:(u + 1) * tq])
                          for i in range(2)) for u in subs)

    def slc_tile(kt, carries):
        nxt = jnp.minimum(kt + 1, seq // KV_TILE - 1)
        carries = list(carries)
        for c in range(n_c):
            for u in subs:
                s = sbuf[u, c]
                sbuf[u, c] = slc_scores(u, nxt, c)
                carries[u] = tuple(_flash_t(s[:, hs], *carries[u][i], vst_ref[kt * n_c + c])
                                   for i, hs in enumerate(halves))
        return tuple(carries)

    n_past = (qis[-1] * tq + KV_TILE - 1) // KV_TILE
    carries = lax.fori_loop(0, n_past, slc_tile, carries)

    for u in subs:
        o_slc = jnp.concatenate([acc / l for _, l, acc in carries[u]], axis=1)
        g = mt_ref[:, u * tq:(u + 1) * tq]
        heads = []
        for h in range(N_HEADS):
            cols = slice(h * tq, (h + 1) * tq)
            heads.append(g[h:h + 1] * o_cmp[u][:, cols] + g[4 + h:5 + h] * o_slc[:, cols]
                         + g[8 + h:9 + h] * o_win[u][:, cols])
        o_ref[u * tq:(u + 1) * tq, :] = jnp.transpose(jnp.concatenate(heads, axis=0))


def _nsa_call(q, ks, kw, vst, vwt, kc, vct, mt, ovt):
    B, T, _ = q.shape
    nblk, ncmp = ovt.shape
    tq = CHUNK * NSA_SUB
    per_mt = TM_PROJ // tq
    full = lambda width: pl.BlockSpec((None, T, width), lambda b, i: (b, 0, 0))
    vfull = lambda keys: pl.BlockSpec((None, T // keys, V_ROWS, keys), lambda b, i: (b, 0, 0, 0))
    return pl.pallas_call(
        functools.partial(_nsa_kernel, nblk=nblk, ncmp=ncmp, seq=T),
        grid=(B, T // tq),
        in_specs=[pl.BlockSpec((None, tq, 256), lambda b, i: (b, i, 0)),
                  full(128), full(128), vfull(UPD), vfull(CHUNK),
                  pl.BlockSpec((None, ncmp, 128), lambda b, i: (b, 0, 0)),
                  pl.BlockSpec((None, HEAD_DIM, ncmp), lambda b, i: (b, 0, 0)),
                  pl.BlockSpec((None, None, N_MISC_T, tq), lambda b, i: (b, i // per_mt, 0, i % per_mt)),
                  pl.BlockSpec((nblk, ncmp), lambda b, i: (0, 0))],
        out_specs=pl.BlockSpec((None, tq, 256), lambda b, i: (b, i, 0)),
        out_shape=jax.ShapeDtypeStruct((B, T, 256), F32),
        scratch_shapes=[pltpu.VMEM((NSA_SUB, nblk, CHUNK), F32),
                        pltpu.VMEM((NSA_SUB, KV_TILE // UPD, UPD, N_HEADS * CHUNK), F32)],
        compiler_params=pltpu.CompilerParams(dimension_semantics=("arbitrary", "arbitrary"),
                                             vmem_limit_bytes=VMEM_LIMIT),
        name="nsa",
    )(q, ks, kw, vst, vwt, kc, vct, mt, ovt)


def _fox_kernel(q_ref, k_ref, vt_ref, o_ref, sbuf, *, seq):
    qi = pl.program_id(1)
    nq = TQ_FOX
    n_chunk = KV_TILE // UPD
    t_lane = qi * nq + lax.broadcasted_iota(jnp.int32, (1, nq), 1)
    k_local = lax.broadcasted_iota(jnp.int32, (UPD, nq), 0)
    q_t = jnp.transpose(q_ref[...].astype(F32))
    extra_row = lax.broadcasted_iota(jnp.int32, (HEAD_DIM, nq), 0)

    def ones_rows(h):
        return jnp.where((extra_row >= 3 * h) & (extra_row < 3 * h + 3), 1.0, 0.0)

    qts = [jnp.concatenate([q_t[HEAD_DIM * h:HEAD_DIM * (h + 1)], ones_rows(h)], axis=0).astype(BF16)
           for h in range(N_HEADS)]

    def scores(kt, c, h):
        start = pl.multiple_of((kt * n_chunk + c) * UPD, UPD)
        return _dot(k_ref[pl.ds(start, UPD), h * 128:(h + 1) * 128], qts[h]), start

    def value(kt, c, h):
        return vt_ref[kt * n_chunk + c, h * V_ROWS:(h + 1) * V_ROWS, :]

    def diag_stream(h):
        def score(c):
            s, start = scores(qi, c, h)
            return jnp.where(start + k_local <= t_lane, s, MASK_VALUE)

        return score, lambda c: value(qi, c, h), _flash_init(nq)

    carries = tuple(_attend_chunks([diag_stream(h) for h in range(N_HEADS)], n_chunk))

    for c in range(n_chunk):
        for h in range(N_HEADS):
            sbuf[h, c] = scores(0, c, h)[0]

    def tile(kt, carries):
        nxt = jnp.minimum(kt + 1, seq // KV_TILE - 1)
        carries = list(carries)
        for c in range(n_chunk):
            for h in range(N_HEADS):
                s = sbuf[h, c]
                sbuf[h, c] = scores(nxt, c, h)[0]
                carries[h] = _flash_t(s, *carries[h], value(kt, c, h))
        return tuple(carries)

    carries = lax.fori_loop(0, qi, tile, carries)
    o_ref[...] = jnp.transpose(jnp.concatenate([acc / l for _, l, acc in carries], axis=0))


def _fox_call(fq, fk, fvt):
    B, T, _ = fq.shape
    tq = TQ_FOX
    return pl.pallas_call(
        functools.partial(_fox_kernel, seq=T),
        grid=(B, T // tq),
        in_specs=[pl.BlockSpec((None, tq, 256), lambda b, i: (b, i, 0)),
                  pl.BlockSpec((None, T, 512), lambda b, i: (b, 0, 0)),
                  pl.BlockSpec((None, T // UPD, N_HEADS * V_ROWS, UPD), lambda b, i: (b, 0, 0, 0))],
        out_specs=pl.BlockSpec((None, tq, 256), lambda b, i: (b, i, 0)),
        out_shape=jax.ShapeDtypeStruct((B, T, 256), F32),
        scratch_shapes=[pltpu.VMEM((N_HEADS, KV_TILE // UPD, UPD, tq), F32)],
        compiler_params=pltpu.CompilerParams(dimension_semantics=("arbitrary", "arbitrary"),
                                             vmem_limit_bytes=VMEM_LIMIT),
        name="fox",
    )(fq, fk, fvt)


def _merge_kernel(x_ref, g_ref, wg_ref, wm_ref, wb_ref, wo_ref, fg_ref, o0_ref, o1_ref, o2_ref, o3_ref,
                  out_ref, *, last, tm):
    rows = tm // 2
    halves = [slice(u * rows, (u + 1) * rows) for u in range(2)]
    xs = [x_ref[r, :] for r in halves]
    hs = [_rmsnorm(x, g_ref[...]).astype(BF16) for x in xs]
    accs = []
    for u, r in enumerate(halves):
        gates = _dot(hs[u], wg_ref[...])
        acc = None
        for i, o_ref in enumerate((o0_ref, o1_ref, o2_ref, o3_ref)):
            merge = _dot(hs[u], wm_ref[:, i * D_MODEL:(i + 1) * D_MODEL])
            gate = gates[:, i * BRANCH_WIDTH:(i + 1) * BRANCH_WIDTH]
            a = (o_ref[r, :] * (gate * _sigmoid(gate))).astype(BF16)
            term = _sigmoid(merge) * _dot(a, wb_ref[i])
            acc = term if acc is None else acc + term
        accs.append(acc)
    for u, r in enumerate(halves):
        y = xs[u] + _dot(accs[u].astype(BF16), wo_ref[...])
        if last:
            y = _rmsnorm(y, fg_ref[...])
        out_ref[r, :] = y


def _merge_call(x, g, wg, wm, wb, wo, fg, o_nsa, o_pool, o_conv, o_fox, last):
    B, T, _ = x.shape
    tm = TM_MERGE
    tok = lambda width: pl.BlockSpec((None, tm, width), lambda b, t: (b, t, 0))
    const = lambda shape: pl.BlockSpec(shape, lambda b, t: (0,) * len(shape))
    return pl.pallas_call(
        functools.partial(_merge_kernel, last=last, tm=tm),
        grid=(B, T // tm),
        in_specs=[tok(D_MODEL), const((1, D_MODEL)), const((D_MODEL, N_BRANCH * BRANCH_WIDTH)),
                  const((D_MODEL, N_BRANCH * D_MODEL)), const((N_BRANCH, BRANCH_WIDTH, D_MODEL)),
                  const((D_MODEL, D_MODEL)), const((1, D_MODEL)), tok(256), tok(256), tok(256), tok(256)],
        out_specs=tok(D_MODEL),
        out_shape=jax.ShapeDtypeStruct((B, T, D_MODEL), F32),
        compiler_params=pltpu.CompilerParams(dimension_semantics=("arbitrary", "arbitrary"),
                                             vmem_limit_bytes=VMEM_LIMIT),
        name="merge",
    )(x, g, wg, wm, wb, wo, fg, o_nsa, o_pool, o_conv, o_fox)


def _rope_tables(pos):
    n = pos.shape[0]
    inv = ROPE_THETA ** (-jnp.arange(ROPE_HALF, dtype=F32) / ROPE_HALF)
    ang = pos.astype(F32)[:, None] * inv[None, :]
    cos, sin = jnp.cos(ang), jnp.sin(ang)
    rest = HEAD_DIM - 2 * ROPE_HALF
    c = jnp.concatenate([cos, cos, jnp.ones((n, rest), F32)], axis=1)
    s = jnp.concatenate([-sin, sin, jnp.zeros((n, rest), F32)], axis=1)
    return jnp.concatenate([c, c], axis=1), jnp.concatenate([s, s], axis=1)


def _proj_weight(w_in):
    col = lambda name, width: w_in[:, _OFF[name]:_OFF[name] + width]
    z64 = jnp.zeros((D_MODEL, HEAD_DIM), F32)
    misc = jnp.concatenate([col("nsa_g", 12), col("fox_f", 4), jnp.zeros((D_MODEL, 112), F32)], axis=1)
    fox_k = []
    for h in range(N_HEADS):
        fox_k += [w_in[:, _OFF["fox_k"] + h * HEAD_DIM:_OFF["fox_k"] + (h + 1) * HEAD_DIM], z64]
    w = jnp.concatenate([col("nsa_q", 256), col("k_s", 64), z64, col("k_w", 64), z64,
                         col("v_s", 64), col("v_w", 64), col("k_c", 128), misc,
                         col("fox_q", 256)] + fox_k + [col("fox_v", 256), col("pool", 256), col("conv", 768)],
                        axis=1)
    assert w.shape[1] == _P_COLS
    return w.astype(BF16)


def _compress_weights(cmp_pos, cmp_w1, cmp_b1, cmp_w2, cmp_b2):
    half = CMP_LEN // 2

    def w1_half(lo):
        wk = cmp_w1[0, lo * HEAD_DIM:(lo + half) * HEAD_DIM].reshape(half, HEAD_DIM, CMP_HIDDEN)
        wv = cmp_w1[1, lo * HEAD_DIM:(lo + half) * HEAD_DIM].reshape(half, HEAD_DIM, CMP_HIDDEN)
        z = jnp.zeros_like(wk)
        top = jnp.concatenate([wk, z], axis=2)
        bot = jnp.concatenate([z, wv], axis=2)
        return jnp.concatenate([top, bot], axis=1).reshape(half * 2 * HEAD_DIM, 2 * CMP_HIDDEN).astype(BF16)

    def pos_half(lo):
        return jnp.concatenate([cmp_pos[0, lo:lo + half], cmp_pos[1, lo:lo + half]], axis=1).reshape(1, -1)

    lane = np.arange(HEAD_DIM)
    perm = np.where(lane < ROPE_HALF, lane + ROPE_HALF, np.where(lane < 2 * ROPE_HALF, lane - ROPE_HALF, lane))
    w2k, w2v = cmp_w2[0], cmp_w2[1]
    zk = jnp.zeros_like(w2k)
    w2 = jnp.concatenate([
        jnp.concatenate([w2k, w2k, w2k[:, perm], w2k[:, perm], zk, zk], axis=1),
        jnp.concatenate([zk, zk, zk, zk, w2v, w2v], axis=1)], axis=0).astype(BF16)
    b2k, b2v = cmp_b2[0], cmp_b2[1]
    b2 = jnp.concatenate([b2k, b2k, b2k[perm], b2k[perm], b2v, b2v])[None, :]
    b1 = jnp.concatenate([cmp_b1[0], cmp_b1[1]])[None, :]
    return pos_half(0), pos_half(half), w1_half(0), w1_half(half), b1, w2, b2


def _block_diag(pool_w):
    n, c, _ = pool_w.shape
    out = jnp.zeros((n * c, n * c), pool_w.dtype)
    for i in range(n):
        out = out.at[i * c:(i + 1) * c, i * c:(i + 1) * c].set(pool_w[i])
    return out


def kernel(x, norm_g, w_in, fox_f_bias, cmp_pos, cmp_w1, cmp_b1, cmp_w2, cmp_b2, pool_w, pool_scale, conv_w,
           w_branch, w_out, final_norm_g):
    B, T, _ = x.shape
    depth = norm_g.shape[0]
    assert T % TM_PROJ == 0 and T % KV_TILE == 0 and T // SEL_LEN <= HEAD_DIM and T >= WINDOW + CHUNK
    assert NSA_SUB * CHUNK == UPD
    rows = T // CMP_STRIDE
    nblk = T // SEL_LEN

    cc_tok, ss_tok = _rope_tables(jnp.arange(T))
    cmp_end = jnp.arange(rows) * CMP_STRIDE + CMP_LEN - 1
    cc_cmp, ss_cmp = _rope_tables(cmp_end)
    ci = np.arange(rows)[:, None] * CMP_STRIDE
    sj = np.arange(nblk)[None, :] * SEL_LEN
    overlap_t = jnp.asarray(((ci < sj + SEL_LEN) & (ci + CMP_LEN > sj)).T, F32)
    tri = jnp.asarray(np.triu(np.ones((TM_PROJ, TM_PROJ), np.float32)), BF16)

    for l in range(depth):
        w1 = _proj_weight(w_in[l])
        fb = jnp.zeros((8, 1), F32).at[4:8, 0].set(fox_f_bias[l])
        (q, ks, kw, vst, vwt, kvc, mt, fq, fk, fvt, o_pool, o_conv) = _proj_call(
            x, norm_g[l][None, :], w1, cc_tok, ss_tok, fb, conv_w[l], _block_diag(pool_w[l]).astype(BF16),
            pool_scale[l][None, :], tri)
        x16 = kvc.reshape(B, rows, CMP_STRIDE * 2 * HEAD_DIM)
        kc, vct = _compress_call(x16, *_compress_weights(cmp_pos[l], cmp_w1[l], cmp_b1[l], cmp_w2[l], cmp_b2[l]),
                                 cc_cmp, ss_cmp)
        o_nsa = _nsa_call(q, ks, kw, vst, vwt, kc, vct, mt, overlap_t)
        o_fox = _fox_call(fq, fk, fvt)
        wg = w_in[l][:, _OFF["gate"]:_OFF["merge"]].astype(BF16)
        wm = w_in[l][:, _OFF["merge"]:_OFF["end"]].astype(BF16)
        x = _merge_call(x, norm_g[l][None, :], wg, wm, w_branch[l].astype(BF16), w_out[l].astype(BF16),
                        final_norm_g[None, :], o_nsa, o_pool, o_conv, o_fox, last=(l == depth - 1))
    return x
```

```python
import functools

import numpy as np
import jax
import jax.numpy as jnp
from jax import lax
from jax.experimental import pallas as pl
from jax.experimental.pallas import tpu as pltpu

F32 = jnp.float32
BF16 = jnp.bfloat16

D_MODEL = 1024
N_BRANCH = 4
BRANCH_WIDTH = 256
HEAD_DIM = 64
N_HEADS = 4
CMP_LEN = 32
CMP_STRIDE = 16
CMP_HIDDEN = 128
SEL_LEN = 64
N_SEL = 16
WINDOW = 512
FORCE_BONUS = 1.0e4
ROPE_THETA = 500000.0
ROPE_HALF = 8
NORM_EPS = 1e-6
MASK_VALUE = -1e30

_OFF = dict(nsa_q=0, k_c=256, v_c=320, k_s=384, v_s=448, k_w=512, v_w=576, nsa_g=640, pool=652,
            conv=908, fox_q=1676, fox_k=1932, fox_v=2188, fox_f=2444, gate=2448, merge=3472, end=7568)

_P_ROPE = 0
_P_VSW = 512
_P_FOXQ = 896
_P_FOXK = 1152
_P_FOXV = 1664
_P_POOL = 1920
_P_CONV = 2176
_P_COLS = 2944

CHUNK = 128
UPD = 256
V_ROWS = HEAD_DIM + 16
LOG2E = 1.4426950408889634
Q_SCALE = HEAD_DIM ** -0.5 * LOG2E
TM_PROJ = 512
KV_TILE = 512
NSA_SUB = 4
TQ_FOX = 512
TM_MERGE = 512
N_MISC_T = 16
POOL_HALO = 16
CONV_HALO = 8
VMEM_LIMIT = 56 * 1024 * 1024


def _dot(a, b, precision=None):
    return jnp.dot(a, b, preferred_element_type=F32, precision=precision)


def _sigmoid(x):
    return 1.0 / (1.0 + jnp.exp(-x))


def _rmsnorm(x, g):
    return x * lax.rsqrt(jnp.mean(x * x, axis=-1, keepdims=True) + NORM_EPS) * g


def _flash_t(s, m, l, acc, v_t):
    m_new = jnp.maximum(m, jnp.max(s, axis=0, keepdims=True))
    alpha = jnp.exp2(m - m_new)
    pv = _dot(v_t, jnp.exp2(s - m_new).astype(BF16))
    l = alpha * l + pv[HEAD_DIM:HEAD_DIM + 1]
    acc = alpha * acc + pv[0:HEAD_DIM]
    return m_new, l, acc


def _attend_chunks(streams, n):
    nxt = [st[0](0) for st in streams]
    carries = [st[2] for st in streams]
    for c in range(n):
        cur = nxt
        if c + 1 < n:
            nxt = [st[0](c + 1) for st in streams]
        carries = [_flash_t(cur[i], *carries[i], streams[i][1](c)) for i in range(len(streams))]
    return carries


def _flash_init(n):
    return (jnp.full((1, n), MASK_VALUE, F32), jnp.zeros((1, n), F32), jnp.zeros((HEAD_DIM, n), F32))


def _proj_kernel(x_ref, g_ref, w_ref, cc_ref, ss_ref, fb_ref, cw_ref, pw_ref, ps_ref, tri_ref,
                 q_ref, ks_ref, kw_ref, vst_ref, vwt_ref, kvc_ref, mt_ref, fq_ref, fk_ref, fvt_ref,
                 opool_ref, oconv_ref, pext, cext, ccarry, *, tm):
    ti = pl.program_id(1)
    x = x_ref[...]
    h = _rmsnorm(x, g_ref[...]).astype(BF16)

    def seg(lo, width):
        return _dot(h, w_ref[:, lo:lo + width])

    lane = lax.broadcasted_iota(jnp.int32, (tm, 128), 1)
    row = lax.broadcasted_iota(jnp.int32, (tm, 128), 0) + ti * tm
    first_half = (lane % HEAD_DIM) < ROPE_HALF
    cc = cc_ref[...]
    ss = ss_ref[...]

    def rope(xc):
        partner = jnp.where(first_half, pltpu.roll(xc, 128 - ROPE_HALF, 1), pltpu.roll(xc, ROPE_HALF, 1))
        return xc * cc + partner * ss

    @pl.when(ti == 0)
    def _():
        ccarry[...] = jnp.zeros_like(ccarry)
        pext[0:POOL_HALO, :] = jnp.zeros((POOL_HALO, BRANCH_WIDTH), F32)
        cext[0:CONV_HALO, :] = jnp.zeros((CONV_HALO, BRANCH_WIDTH), F32)

    g_rope = seg(_P_ROPE, 512)
    g_small = seg(_P_VSW, 384)
    g_fq = seg(_P_FOXQ, 256)
    g_fk = seg(_P_FOXK, 512)
    g_fv = seg(_P_FOXV, 256)
    g_pool = seg(_P_POOL, 256)
    cv = seg(_P_CONV, 768)

    q_ref[:, 0:128] = (rope(g_rope[:, 0:128]) * Q_SCALE).astype(BF16)
    q_ref[:, 128:256] = (rope(g_rope[:, 128:256]) * Q_SCALE).astype(BF16)
    block_onehot = jnp.where(lane - HEAD_DIM == row // SEL_LEN, 1.0, 0.0)
    ks_ref[...] = (rope(g_rope[:, 256:384]) + block_onehot).astype(BF16)
    kw_ref[...] = rope(g_rope[:, 384:512]).astype(BF16)
    vsw_t = jnp.transpose(g_small[:, 0:128]).astype(BF16)
    ones_upd = jnp.ones((V_ROWS - HEAD_DIM, UPD), BF16)
    for c in range(tm // UPD):
        vst_ref[c] = jnp.concatenate([vsw_t[0:HEAD_DIM, c * UPD:(c + 1) * UPD], ones_upd], axis=0)
    for c in range(tm // CHUNK):
        vwt_ref[c] = jnp.concatenate([vsw_t[HEAD_DIM:128, c * CHUNK:(c + 1) * CHUNK], ones_upd[:, 0:CHUNK]], axis=0)
    kvc_ref[...] = g_small[:, 128:256]

    z_t = jnp.transpose(g_small[:, 256:384])[0:N_MISC_T, :]
    gates = _sigmoid(z_t)
    row8 = lax.broadcasted_iota(jnp.int32, (8, tm), 0)
    zb = z_t[8:16] + fb_ref[...]
    logf = jnp.where(row8 >= 4, jnp.minimum(zb, 0.0) - jnp.log(1.0 + jnp.exp(-jnp.abs(zb))), 0.0)

    def split3(v):
        hi = v.astype(BF16).astype(F32)
        mid = (v - hi).astype(BF16).astype(F32)
        return hi, mid, (v - hi) - mid

    parts = jnp.concatenate(list(split3(logf)) + [jnp.zeros((8, tm), F32)], axis=0).astype(BF16)
    part_sums = _dot(parts, tri_ref[...])
    csum = (part_sums[0:8] + part_sums[8:16]) + part_sums[16:24] + ccarry[:, 0:1]
    ccarry[...] = jnp.broadcast_to(csum[:, tm - 1:tm], (8, 128))
    mt_ref[0:8, :] = gates[0:8]
    mt_ref[8:16, :] = jnp.where(row8 < 4, gates[8:16], csum)

    fq_ref[...] = (g_fq * Q_SCALE).astype(BF16)
    terms = split3(csum * (-LOG2E))
    row16 = lax.broadcasted_iota(jnp.int32, (16, tm), 0)
    extra_t = jnp.zeros((16, tm), F32)
    for hd in range(N_HEADS):
        for j in range(3):
            src = jnp.broadcast_to(terms[j][4 + hd:5 + hd, :], (16, tm))
            extra_t = jnp.where(row16 == 3 * hd + j, src, extra_t)
    extra = jnp.transpose(jnp.concatenate(
        [jnp.zeros((HEAD_DIM, tm), F32), extra_t, jnp.zeros((128 - HEAD_DIM - 16, tm), F32)], axis=0))
    for hd in range(N_HEADS):
        fk_ref[:, hd * 128:(hd + 1) * 128] = (g_fk[:, hd * 128:(hd + 1) * 128] + extra).astype(BF16)
    fv_t = jnp.transpose(g_fv).astype(BF16)
    for c in range(tm // UPD):
        cols = slice(c * UPD, (c + 1) * UPD)
        fvt_ref[c] = jnp.concatenate(
            [piece for hd in range(N_HEADS) for piece in (fv_t[hd * HEAD_DIM:(hd + 1) * HEAD_DIM, cols], ones_upd)],
            axis=0)

    pext[POOL_HALO:POOL_HALO + tm, :] = g_pool

    def pld(k, c):
        return pext[pl.ds(POOL_HALO - k, tm), c * 128:(c + 1) * 128]

    left = lane < 64
    rowp1 = (row + 1).astype(F32)

    def cnt(w):
        return jnp.minimum(rowp1, float(w))

    e0 = pld(0, 0)
    s2 = e0 + pld(1, 0)
    s4 = s2 + pld(2, 0) + pld(3, 0)
    p0 = jnp.where(left, s2 / cnt(2), s4 / cnt(4)) - e0
    f0 = pld(0, 1)
    s8 = f0
    for k in range(1, 8):
        s8 = s8 + pld(k, 1)
    s16 = s8
    for k in range(8, 16):
        s16 = s16 + pld(k, 1)
    p1 = jnp.where(left, s8 / cnt(8), s16 / cnt(16)) - f0
    pw = pw_ref[...]
    mixed = _dot(p0.astype(BF16), pw[0:128, :]) + _dot(p1.astype(BF16), pw[128:256, :])
    opool_ref[...] = mixed * ps_ref[...]
    pext[0:POOL_HALO, :] = pext[tm:tm + POOL_HALO, :]

    u = cv[:, 512:768] * cv[:, 0:256]
    cext[CONV_HALO:CONV_HALO + tm, :] = u
    y = (cext[pl.ds(CONV_HALO - 2, tm), :] * cw_ref[0:1, :]
         + cext[pl.ds(CONV_HALO - 1, tm), :] * cw_ref[1:2, :]
         + u * cw_ref[2:3, :])
    oconv_ref[...] = cv[:, 256:512] * y
    cext[0:CONV_HALO, :] = cext[tm:tm + CONV_HALO, :]


def _proj_call(x, g, w1, cc, ss, fb, cw, pw, ps, tri):
    B, T, _ = x.shape
    tm = TM_PROJ
    nt = T // tm
    tok = lambda width: pl.BlockSpec((None, tm, width), lambda b, t: (b, t, 0))
    chunked = lambda rows, keys: pl.BlockSpec((None, tm // keys, rows, keys), lambda b, t: (b, t, 0, 0))
    const = lambda shape: pl.BlockSpec(shape, lambda b, t: (0,) * len(shape))
    out_shape = [
        jax.ShapeDtypeStruct((B, T, 256), BF16),
        jax.ShapeDtypeStruct((B, T, 128), BF16),
        jax.ShapeDtypeStruct((B, T, 128), BF16),
        jax.ShapeDtypeStruct((B, T // UPD, V_ROWS, UPD), BF16),
        jax.ShapeDtypeStruct((B, T // CHUNK, V_ROWS, CHUNK), BF16),
        jax.ShapeDtypeStruct((B, T, 128), F32),
        jax.ShapeDtypeStruct((B, nt, N_MISC_T, tm), F32),
        jax.ShapeDtypeStruct((B, T, 256), BF16),
        jax.ShapeDtypeStruct((B, T, 512), BF16),
        jax.ShapeDtypeStruct((B, T // UPD, N_HEADS * V_ROWS, UPD), BF16),
        jax.ShapeDtypeStruct((B, T, 256), F32),
        jax.ShapeDtypeStruct((B, T, 256), F32),
    ]
    out_specs = [tok(256), tok(128), tok(128), chunked(V_ROWS, UPD), chunked(V_ROWS, CHUNK), tok(128),
                 pl.BlockSpec((None, None, N_MISC_T, tm), lambda b, t: (b, t, 0, 0)),
                 tok(256), tok(512), chunked(N_HEADS * V_ROWS, UPD), tok(256), tok(256)]
    in_specs = [tok(D_MODEL), const((1, D_MODEL)), const((D_MODEL, _P_COLS)),
                pl.BlockSpec((tm, 128), lambda b, t: (t, 0)), pl.BlockSpec((tm, 128), lambda b, t: (t, 0)),
                const((8, 1)), const((3, 256)), const((256, 256)), const((1, 256)), const((tm, tm))]
    return pl.pallas_call(
        functools.partial(_proj_kernel, tm=tm),
        grid=(B, nt),
        in_specs=in_specs,
        out_specs=out_specs,
        out_shape=out_shape,
        scratch_shapes=[pltpu.VMEM((tm + POOL_HALO, 256), F32), pltpu.VMEM((tm + CONV_HALO, 256), F32),
                        pltpu.VMEM((8, 128), F32)],
        compiler_params=pltpu.CompilerParams(dimension_semantics=("arbitrary", "arbitrary"),
                                             vmem_limit_bytes=VMEM_LIMIT),
        name="proj",
    )(x, g, w1, cc, ss, fb, cw, pw, ps, tri)


def _compress_kernel(x_ref, pt_ref, pb_ref, wt_ref, wb_ref, b1_ref, w2_ref, b2_ref, cc_ref, ss_ref,
                     kc_ref, vct_ref, *, rows):
    x = x_ref[...]
    a = _dot((x + pt_ref[...]).astype(BF16), wt_ref[...])
    b = _dot((x + pb_ref[...]).astype(BF16), wb_ref[...])
    hid = a + pltpu.roll(b, rows - 1, 0) + b1_ref[...]
    act = hid * _sigmoid(hid)
    out = _dot(act.astype(BF16), w2_ref[...]) + b2_ref[...]
    lane = lax.broadcasted_iota(jnp.int32, (rows, 128), 1)
    roped = out[:, 0:128] * cc_ref[...] + out[:, 128:256] * ss_ref[...]
    kc_ref[...] = jnp.where(lane < HEAD_DIM, roped, 0.0).astype(BF16)
    vct_ref[...] = jnp.transpose(out[:, 256:384])[0:HEAD_DIM, :].astype(BF16)


def _compress_call(x16, pt, pb, wt, wb, b1, w2, b2, cc, ss):
    B, rows, width = x16.shape
    const = lambda shape: pl.BlockSpec(shape, lambda b: (0,) * len(shape))
    return pl.pallas_call(
        functools.partial(_compress_kernel, rows=rows),
        grid=(B,),
        in_specs=[pl.BlockSpec((None, rows, width), lambda b: (b, 0, 0)),
                  const((1, width)), const((1, width)), const((width, 256)), const((width, 256)),
                  const((1, 256)), const((256, 384)), const((1, 384)), const((rows, 128)), const((rows, 128))],
        out_specs=[pl.BlockSpec((None, rows, 128), lambda b: (b, 0, 0)),
                   pl.BlockSpec((None, HEAD_DIM, rows), lambda b: (b, 0, 0))],
        out_shape=[jax.ShapeDtypeStruct((B, rows, 128), BF16), jax.ShapeDtypeStruct((B, HEAD_DIM, rows), BF16)],
        compiler_params=pltpu.CompilerParams(dimension_semantics=("arbitrary",),
                                             vmem_limit_bytes=VMEM_LIMIT),
        name="compress",
    )(x16, pt, pb, wt, wb, b1, w2, b2, cc, ss)


def _nsa_kernel(q_ref, ks_ref, kw_ref, vst_ref, vwt_ref, kc_ref, vct_ref, mt_ref, ovt_ref,
                o_ref, key_ref, sbuf, *, nblk, ncmp):
    step = pl.program_id(1)
    tq = CHUNK
    nq = N_HEADS * tq
    hw = nq // 2
    halves = (slice(0, hw), slice(hw, nq))
    n_c = KV_TILE // UPD
    n_win = WINDOW // tq
    subs = range(NSA_SUB)
    qis = [step * NSA_SUB + u for u in subs]
    q_local = lax.broadcasted_iota(jnp.int32, (1, nq), 1) % tq
    k_local = lax.broadcasted_iota(jnp.int32, (tq, hw), 0)
    q_loc = q_local[:, 0:hw]
    causal = k_local <= q_loc

    q4, qt_plain, sc = [], [], []
    for u in subs:
        q_t = jnp.transpose(q_ref[u * tq:(u + 1) * tq, :].astype(F32))
        q4.append(jnp.concatenate([q_t[HEAD_DIM * h:HEAD_DIM * (h + 1)] for h in range(N_HEADS)], axis=1))
        qt_plain.append(jnp.concatenate([q4[u], jnp.zeros((HEAD_DIM, nq), F32)], axis=0).astype(BF16))
        sc.append(_dot(kc_ref[...], qt_plain[u]))

    o_cmp, imp = [], []
    n_sub = lax.broadcasted_iota(jnp.int32, (ncmp, nq), 0)
    for u in subs:
        cmask = (n_sub * CMP_STRIDE + (CMP_LEN - 1)) <= qis[u] * tq + q_local
        s = jnp.where(cmask, sc[u], MASK_VALUE)
        mc = jnp.max(s, axis=0, keepdims=True)
        pc = jnp.where(cmask, jnp.exp2(s - mc), 0.0)
        lc = jnp.sum(pc, axis=0, keepdims=True)
        pc = pc / jnp.where(lc > 0.0, lc, 1.0)
        o_cmp.append(_dot(vct_ref[...], pc.astype(BF16)))
        psum = (pc[:, 0:tq] + pc[:, tq:2 * tq]) + (pc[:, 2 * tq:3 * tq] + pc[:, 3 * tq:4 * tq])
        imp.append(_dot(ovt_ref[...], psum, precision=lax.Precision.HIGHEST))

    def win_stream(u, hs):
        qi = qis[u]

        def score(c):
            if c == 0:
                own = kw_ref[pl.ds(pl.multiple_of(qi * tq, tq), tq), :]
                return jnp.where(causal, _dot(own, qt_plain[u][:, hs]), MASK_VALUE)
            ci = qi - n_win + (c - 1)
            start = pl.multiple_of(jnp.maximum(ci, 0) * tq, tq)
            s = _dot(kw_ref[pl.ds(start, tq), :], qt_plain[u][:, hs])
            ok = ((k_local > q_loc) & (ci >= 0)) if c == 1 else jnp.broadcast_to(ci >= 0, (tq, hw))
            return jnp.where(ok, s, MASK_VALUE)

        def value(c):
            return vwt_ref[qi] if c == 0 else vwt_ref[jnp.maximum(qi - n_win + (c - 1), 0)]

        return score, value, _flash_init(hw)

    carries_w = _attend_chunks([win_stream(u, hs) for u in subs for hs in halves], n_win + 1)
    o_win = [jnp.concatenate([acc / l for _, l, acc in carries_w[2 * u:2 * u + 2]], axis=1) for u in subs]

    jb = lax.broadcasted_iota(jnp.int32, (nblk, tq), 0)
    sub8 = lax.broadcasted_iota(jnp.int32, (8, tq), 0)
    qt_sel = []
    for u in subs:
        tt = qis[u] * tq + lax.broadcasted_iota(jnp.int32, (nblk, tq), 1)
        jt = tt // SEL_LEN
        forced = (jb == 0) | (jb == jt) | (jb == jt - 1)
        valid = jb * SEL_LEN <= tt
        key = jnp.where(valid, jnp.where(forced, imp[u] + FORCE_BONUS, imp[u]), MASK_VALUE)
        key_ref[u] = key
        key_rows = [key[8 * v:8 * v + 8] for v in range(nblk // 8)]
        ranks = [jnp.zeros((8, tq), F32) for _ in key_rows]
        for jp in range(nblk):
            other = jnp.broadcast_to(key_ref[u, pl.ds(jp, 1), :], (8, tq))
            for v, kv in enumerate(key_rows):
                if 8 * v > jp:
                    ahead = other >= kv
                elif 8 * v + 7 <= jp:
                    ahead = other > kv
                else:
                    ahead = (other > kv) | ((other == kv) & (sub8 > jp - 8 * v))
                ranks[v] = ranks[v] + jnp.where(ahead, 1.0, 0.0)
        selected = jnp.concatenate(ranks, axis=0) < float(min(N_SEL, nblk))
        sel_bias = jnp.where(selected, 0.0, MASK_VALUE)
        if nblk < HEAD_DIM:
            sel_bias = jnp.concatenate([sel_bias, jnp.zeros((HEAD_DIM - nblk, tq), F32)], axis=0)
        qt_sel.append(jnp.concatenate([q4[u], jnp.concatenate([sel_bias] * N_HEADS, axis=1)], axis=0).astype(BF16))

    def slc_scores(u, kt, c):
        start = pl.multiple_of((kt * n_c + c) * UPD, UPD)
        return _dot(ks_ref[pl.ds(start, UPD), :], qt_sel[u])

    def slc_update(u, carry, s, kt, c):
        return tuple(_flash_t(s[:, hs], *carry[i], vst_ref[kt * n_c + c]) for i, hs in enumerate(halves))

    for c in range(n_c):
        for u in subs:
            sbuf[u, c] = slc_scores(u, 0, c)

    def slc_tile(kt, carries):
        carries = list(carries)
        for c in range(n_c):
            for u in subs:
                s = sbuf[u, c]
                sbuf[u, c] = slc_scores(u, kt + 1, c)
                carries[u] = slc_update(u, carries[u], s, kt, c)
        return tuple(carries)

    carries = list(lax.fori_loop(0, step, slc_tile, tuple((_flash_init(hw), _flash_init(hw)) for u in subs)))
    k_upd = lax.broadcasted_iota(jnp.int32, (UPD, nq), 0)
    for c in range(n_c):
        first, last = c * UPD // tq, (c + 1) * UPD // tq - 1
        for u in subs:
            if first > u:
                continue
            s = sbuf[u, c]
            if last >= u:
                s = jnp.where(c * UPD + k_upd <= u * tq + q_local, s, MASK_VALUE)
            carries[u] = slc_update(u, carries[u], s, step, c)

    for u in subs:
        o_slc = jnp.concatenate([acc / l for _, l, acc in carries[u]], axis=1)
        g = mt_ref[:, u * tq:(u + 1) * tq]
        heads = []
        for h in range(N_HEADS):
            cols = slice(h * tq, (h + 1) * tq)
            heads.append(g[h:h + 1] * o_cmp[u][:, cols] + g[4 + h:5 + h] * o_slc[:, cols]
                         + g[8 + h:9 + h] * o_win[u][:, cols])
        o_ref[u * tq:(u + 1) * tq, :] = jnp.transpose(jnp.concatenate(heads, axis=0))


def _nsa_call(q, ks, kw, vst, vwt, kc, vct, mt, ovt):
    B, T, _ = q.shape
    nblk, ncmp = ovt.shape
    tq = CHUNK * NSA_SUB
    per_mt = TM_PROJ // tq
    full = lambda width: pl.BlockSpec((None, T, width), lambda b, i: (b, 0, 0))
    vfull = lambda keys: pl.BlockSpec((None, T // keys, V_ROWS, keys), lambda b, i: (b, 0, 0, 0))
    return pl.pallas_call(
        functools.partial(_nsa_kernel, nblk=nblk, ncmp=ncmp),
        grid=(B, T // tq),
        in_specs=[pl.BlockSpec((None, tq, 256), lambda b, i: (b, i, 0)),
                  full(128), full(128), vfull(UPD), vfull(CHUNK),
                  pl.BlockSpec((None, ncmp, 128), lambda b, i: (b, 0, 0)),
                  pl.BlockSpec((None, HEAD_DIM, ncmp), lambda b, i: (b, 0, 0)),
                  pl.BlockSpec((None, None, N_MISC_T, tq), lambda b, i: (b, i // per_mt, 0, i % per_mt)),
                  pl.BlockSpec((nblk, ncmp), lambda b, i: (0, 0))],
        out_specs=pl.BlockSpec((None, tq, 256), lambda b, i: (b, i, 0)),
        out_shape=jax.ShapeDtypeStruct((B, T, 256), F32),
        scratch_shapes=[pltpu.VMEM((NSA_SUB, nblk, CHUNK), F32),
                        pltpu.VMEM((NSA_SUB, KV_TILE // UPD, UPD, N_HEADS * CHUNK), F32)],
        compiler_params=pltpu.CompilerParams(dimension_semantics=("arbitrary", "arbitrary"),
                                             vmem_limit_bytes=VMEM_LIMIT),
        name="nsa",
    )(q, ks, kw, vst, vwt, kc, vct, mt, ovt)


def _fox_kernel(q_ref, k_ref, vt_ref, o_ref, sbuf):
    qi = pl.program_id(1)
    nq = TQ_FOX
    n_chunk = KV_TILE // UPD
    t_lane = qi * nq + lax.broadcasted_iota(jnp.int32, (1, nq), 1)
    k_local = lax.broadcasted_iota(jnp.int32, (UPD, nq), 0)
    q_t = jnp.transpose(q_ref[...].astype(F32))
    extra_row = lax.broadcasted_iota(jnp.int32, (HEAD_DIM, nq), 0)

    def ones_rows(h):
        return jnp.where((extra_row >= 3 * h) & (extra_row < 3 * h + 3), 1.0, 0.0)

    qts = [jnp.concatenate([q_t[HEAD_DIM * h:HEAD_DIM * (h + 1)], ones_rows(h)], axis=0).astype(BF16)
           for h in range(N_HEADS)]

    def scores(kt, c, h):
        start = pl.multiple_of((kt * n_chunk + c) * UPD, UPD)
        return _dot(k_ref[pl.ds(start, UPD), h * 128:(h + 1) * 128], qts[h])

    def value(kt, c, h):
        return vt_ref[kt * n_chunk + c, h * V_ROWS:(h + 1) * V_ROWS, :]

    for c in range(n_chunk):
        for h in range(N_HEADS):
            sbuf[h, c] = scores(0, c, h)

    def tile(kt, carries):
        carries = list(carries)
        for c in range(n_chunk):
            for h in range(N_HEADS):
                s = sbuf[h, c]
                sbuf[h, c] = scores(kt + 1, c, h)
                carries[h] = _flash_t(s, *carries[h], value(kt, c, h))
        return tuple(carries)

    carries = list(lax.fori_loop(0, qi, tile, tuple(_flash_init(nq) for _ in range(N_HEADS))))
    for c in range(n_chunk):
        causal = qi * nq + c * UPD + k_local <= t_lane
        for h in range(N_HEADS):
            carries[h] = _flash_t(jnp.where(causal, sbuf[h, c], MASK_VALUE), *carries[h], value(qi, c, h))
    o_ref[...] = jnp.transpose(jnp.concatenate([acc / l for _, l, acc in carries], axis=0))


def _fox_call(fq, fk, fvt):
    B, T, _ = fq.shape
    tq = TQ_FOX
    return pl.pallas_call(
        _fox_kernel,
        grid=(B, T // tq),
        in_specs=[pl.BlockSpec((None, tq, 256), lambda b, i: (b, i, 0)),
                  pl.BlockSpec((None, T, 512), lambda b, i: (b, 0, 0)),
                  pl.BlockSpec((None, T // UPD, N_HEADS * V_ROWS, UPD), lambda b, i: (b, 0, 0, 0))],
        out_specs=pl.BlockSpec((None, tq, 256), lambda b, i: (b, i, 0)),
        out_shape=jax.ShapeDtypeStruct((B, T, 256), F32),
        scratch_shapes=[pltpu.VMEM((N_HEADS, KV_TILE // UPD, UPD, tq), F32)],
        compiler_params=pltpu.CompilerParams(dimension_semantics=("arbitrary", "arbitrary"),
                                             vmem_limit_bytes=VMEM_LIMIT),
        name="fox",
    )(fq, fk, fvt)


def _merge_kernel(x_ref, g_ref, wg_ref, wm_ref, wb_ref, wo_ref, fg_ref, o0_ref, o1_ref, o2_ref, o3_ref,
                  out_ref, *, last, tm):
    rows = tm // 2
    halves = [slice(u * rows, (u + 1) * rows) for u in range(2)]
    xs = [x_ref[r, :] for r in halves]
    hs = [_rmsnorm(x, g_ref[...]).astype(BF16) for x in xs]
    accs = []
    for u, r in enumerate(halves):
        gates = _dot(hs[u], wg_ref[...])
        acc = None
        for i, o_ref in enumerate((o0_ref, o1_ref, o2_ref, o3_ref)):
            merge = _dot(hs[u], wm_ref[:, i * D_MODEL:(i + 1) * D_MODEL])
            gate = gates[:, i * BRANCH_WIDTH:(i + 1) * BRANCH_WIDTH]
            a = (o_ref[r, :] * (gate * _sigmoid(gate))).astype(BF16)
            term = _sigmoid(merge) * _dot(a, wb_ref[i])
            acc = term if acc is None else acc + term
        accs.append(acc)
    for u, r in enumerate(halves):
        y = xs[u] + _dot(accs[u].astype(BF16), wo_ref[...])
        if last:
            y = _rmsnorm(y, fg_ref[...])
        out_ref[r, :] = y


def _merge_call(x, g, wg, wm, wb, wo, fg, o_nsa, o_pool, o_conv, o_fox, last):
    B, T, _ = x.shape
    tm = TM_MERGE
    tok = lambda width: pl.BlockSpec((None, tm, width), lambda b, t: (b, t, 0))
    const = lambda shape: pl.BlockSpec(shape, lambda b, t: (0,) * len(shape))
    return pl.pallas_call(
        functools.partial(_merge_kernel, last=last, tm=tm),
        grid=(B, T // tm),
        in_specs=[tok(D_MODEL), const((1, D_MODEL)), const((D_MODEL, N_BRANCH * BRANCH_WIDTH)),
                  const((D_MODEL, N_BRANCH * D_MODEL)), const((N_BRANCH, BRANCH_WIDTH, D_MODEL)),
                  const((D_MODEL, D_MODEL)), const((1, D_MODEL)), tok(256), tok(256), tok(256), tok(256)],
        out_specs=tok(D_MODEL),
        out_shape=jax.ShapeDtypeStruct((B, T, D_MODEL), F32),
        compiler_params=pltpu.CompilerParams(dimension_semantics=("arbitrary", "arbitrary"),
                                             vmem_limit_bytes=VMEM_LIMIT),
        name="merge",
    )(x, g, wg, wm, wb, wo, fg, o_nsa, o_pool, o_conv, o_fox)


def _rope_tables(pos):
    n = pos.shape[0]
    inv = ROPE_THETA ** (-jnp.arange(ROPE_HALF, dtype=F32) / ROPE_HALF)
    ang = pos.astype(F32)[:, None] * inv[None, :]
    cos, sin = jnp.cos(ang), jnp.sin(ang)
    rest = HEAD_DIM - 2 * ROPE_HALF
    c = jnp.concatenate([cos, cos, jnp.ones((n, rest), F32)], axis=1)
    s = jnp.concatenate([-sin, sin, jnp.zeros((n, rest), F32)], axis=1)
    return jnp.concatenate([c, c], axis=1), jnp.concatenate([s, s], axis=1)


def _proj_weight(w_in):
    col = lambda name, width: w_in[:, _OFF[name]:_OFF[name] + width]
    z64 = jnp.zeros((D_MODEL, HEAD_DIM), F32)
    misc = jnp.concatenate([col("nsa_g", 12), col("fox_f", 4), jnp.zeros((D_MODEL, 112), F32)], axis=1)
    fox_k = []
    for h in range(N_HEADS):
        fox_k += [w_in[:, _OFF["fox_k"] + h * HEAD_DIM:_OFF["fox_k"] + (h + 1) * HEAD_DIM], z64]
    w = jnp.concatenate([col("nsa_q", 256), col("k_s", 64), z64, col("k_w", 64), z64,
                         col("v_s", 64), col("v_w", 64), col("k_c", 128), misc,
                         col("fox_q", 256)] + fox_k + [col("fox_v", 256), col("pool", 256), col("conv", 768)],
                        axis=1)
    assert w.shape[1] == _P_COLS
    return w.astype(BF16)


def _compress_weights(cmp_pos, cmp_w1, cmp_b1, cmp_w2, cmp_b2):
    half = CMP_LEN // 2

    def w1_half(lo):
        wk = cmp_w1[0, lo * HEAD_DIM:(lo + half) * HEAD_DIM].reshape(half, HEAD_DIM, CMP_HIDDEN)
        wv = cmp_w1[1, lo * HEAD_DIM:(lo + half) * HEAD_DIM].reshape(half, HEAD_DIM, CMP_HIDDEN)
        z = jnp.zeros_like(wk)
        top = jnp.concatenate([wk, z], axis=2)
        bot = jnp.concatenate([z, wv], axis=2)
        return jnp.concatenate([top, bot], axis=1).reshape(half * 2 * HEAD_DIM, 2 * CMP_HIDDEN).astype(BF16)

    def pos_half(lo):
        return jnp.concatenate([cmp_pos[0, lo:lo + half], cmp_pos[1, lo:lo + half]], axis=1).reshape(1, -1)

    lane = np.arange(HEAD_DIM)
    perm = np.where(lane < ROPE_HALF, lane + ROPE_HALF, np.where(lane < 2 * ROPE_HALF, lane - ROPE_HALF, lane))
    w2k, w2v = cmp_w2[0], cmp_w2[1]
    zk = jnp.zeros_like(w2k)
    w2 = jnp.concatenate([
        jnp.concatenate([w2k, w2k, w2k[:, perm], w2k[:, perm], zk, zk], axis=1),
        jnp.concatenate([zk, zk, zk, zk, w2v, w2v], axis=1)], axis=0).astype(BF16)
    b2k, b2v = cmp_b2[0], cmp_b2[1]
    b2 = jnp.concatenate([b2k, b2k, b2k[perm], b2k[perm], b2v, b2v])[None, :]
    b1 = jnp.concatenate([cmp_b1[0], cmp_b1[1]])[None, :]
    return pos_half(0), pos_half(half), w1_half(0), w1_half(half), b1, w2, b2


def _block_diag(pool_w):
    n, c, _ = pool_w.shape
    out = jnp.zeros((n * c, n * c), pool_w.dtype)
    for i in range(n):
        out = out.at[i * c:(i + 1) * c, i * c:(i + 1) * c].set(pool_w[i])
    return out


def kernel(x, norm_g, w_in, fox_f_bias, cmp_pos, cmp_w1, cmp_b1, cmp_w2, cmp_b2, pool_w, pool_scale, conv_w,
           w_branch, w_out, final_norm_g):
    B, T, _ = x.shape
    depth = norm_g.shape[0]
    assert T % TM_PROJ == 0 and T % KV_TILE == 0 and T // SEL_LEN <= HEAD_DIM and T >= WINDOW + CHUNK
    assert NSA_SUB * CHUNK == KV_TILE == TQ_FOX and TM_PROJ % KV_TILE == 0
    rows = T // CMP_STRIDE
    nblk = T // SEL_LEN

    cc_tok, ss_tok = _rope_tables(jnp.arange(T))
    cmp_end = jnp.arange(rows) * CMP_STRIDE + CMP_LEN - 1
    cc_cmp, ss_cmp = _rope_tables(cmp_end)
    ci = np.arange(rows)[:, None] * CMP_STRIDE
    sj = np.arange(nblk)[None, :] * SEL_LEN
    overlap_t = jnp.asarray(((ci < sj + SEL_LEN) & (ci + CMP_LEN > sj)).T, F32)
    tri = jnp.asarray(np.triu(np.ones((TM_PROJ, TM_PROJ), np.float32)), BF16)

    for l in range(depth):
        w1 = _proj_weight(w_in[l])
        fb = jnp.zeros((8, 1), F32).at[4:8, 0].set(fox_f_bias[l])
        (q, ks, kw, vst, vwt, kvc, mt, fq, fk, fvt, o_pool, o_conv) = _proj_call(
            x, norm_g[l][None, :], w1, cc_tok, ss_tok, fb, conv_w[l], _block_diag(pool_w[l]).astype(BF16),
            pool_scale[l][None, :], tri)
        x16 = kvc.reshape(B, rows, CMP_STRIDE * 2 * HEAD_DIM)
        kc, vct = _compress_call(x16, *_compress_weights(cmp_pos[l], cmp_w1[l], cmp_b1[l], cmp_w2[l], cmp_b2[l]),
                                 cc_cmp, ss_cmp)
        o_nsa = _nsa_call(q, ks, kw, vst, vwt, kc, vct, mt, overlap_t)
        o_fox = _fox_call(fq, fk, fvt)
        wg = w_in[l][:, _OFF["gate"]:_OFF["merge"]].astype(BF16)
        wm = w_in[l][:, _OFF["merge"]:_OFF["end"]].astype(BF16)
        x = _merge_call(x, norm_g[l][None, :], wg, wm, w_branch[l].astype(BF16), w_out[l].astype(BF16),
                        final_norm_g[None, :], o_nsa, o_pool, o_conv, o_fox, last=(l == depth - 1))
    return x
```

```python
import functools

import numpy as np
import jax
import jax.numpy as jnp
from jax import lax
from jax.experimental import pallas as pl
from jax.experimental.pallas import tpu as pltpu

F32 = jnp.float32
BF16 = jnp.bfloat16

D_MODEL = 1024
N_BRANCH = 4
BRANCH_WIDTH = 256
HEAD_DIM = 64
N_HEADS = 4
CMP_LEN = 32
CMP_STRIDE = 16
CMP_HIDDEN = 128
CMP_GROUP = 8
SEL_LEN = 64
N_SEL = 16
WINDOW = 512
FORCE_BONUS = 1.0e4
ROPE_THETA = 500000.0
ROPE_HALF = 8
NORM_EPS = 1e-6
MASK_VALUE = -1e30

_OFF = dict(nsa_q=0, k_c=256, v_c=320, k_s=384, v_s=448, k_w=512, v_w=576, nsa_g=640, pool=652,
            conv=908, fox_q=1676, fox_k=1932, fox_v=2188, fox_f=2444, gate=2448, merge=3472, end=7568)

_P_ROPE = 0
_P_VSW = 512
_P_FOXQ = 896
_P_FOXK = 1152
_P_FOXV = 1664
_P_POOL = 1920
_P_CONV = 2176
_P_COLS = 2944

CHUNK = 128
UPD = 256
V_ROWS = HEAD_DIM + 16
LOG2E = 1.4426950408889634
Q_SCALE = HEAD_DIM ** -0.5 * LOG2E
TM_PROJ = 512
KV_TILE = 512
NSA_SUB = 4
TQ_FOX = 512
TM_MERGE = 512
N_MISC_T = 16
POOL_HALO = 16
CONV_HALO = 8
VMEM_LIMIT = 56 * 1024 * 1024


def _dot(a, b, precision=None):
    return jnp.dot(a, b, preferred_element_type=F32, precision=precision)


def _sigmoid(x):
    return 1.0 / (1.0 + jnp.exp(-x))


def _rmsnorm(x, g):
    return x * lax.rsqrt(jnp.mean(x * x, axis=-1, keepdims=True) + NORM_EPS) * g


def _flash_t(s, m, l, acc, v_t):
    m_new = jnp.maximum(m, jnp.max(s, axis=0, keepdims=True))
    alpha = jnp.exp2(m - m_new)
    pv = _dot(v_t, jnp.exp2(s - m_new).astype(BF16))
    l = alpha * l + pv[HEAD_DIM:HEAD_DIM + 1]
    acc = alpha * acc + pv[0:HEAD_DIM]
    return m_new, l, acc


def _attend_chunks(streams, n):
    nxt = [st[0](0) for st in streams]
    carries = [st[2] for st in streams]
    for c in range(n):
        cur = nxt
        if c + 1 < n:
            nxt = [st[0](c + 1) for st in streams]
        carries = [_flash_t(cur[i], *carries[i], streams[i][1](c)) for i in range(len(streams))]
    return carries


def _flash_init(n):
    return (jnp.full((1, n), MASK_VALUE, F32), jnp.zeros((1, n), F32), jnp.zeros((HEAD_DIM, n), F32))


def _proj_kernel(x_ref, xn_ref, g_ref, w_ref, cc_ref, ss_ref, fb_ref, cw_ref, pw_ref, ps_ref, tri_ref,
                 q_ref, ks_ref, kw_ref, vst_ref, vwt_ref, kvc_ref, mt_ref, fq_ref, fk_ref, fvt_ref,
                 opool_ref, oconv_ref, pext, cext, ccarry, hbuf, *, tm):
    ti = pl.program_id(1)

    @pl.when((pl.program_id(0) == 0) & (ti == 0))
    def _():
        hbuf[...] = _rmsnorm(x_ref[...], g_ref[...]).astype(BF16)

    h = hbuf[...]

    def seg(lo, width):
        return _dot(h, w_ref[:, lo:lo + width])

    lane = lax.broadcasted_iota(jnp.int32, (tm, 128), 1)
    row = lax.broadcasted_iota(jnp.int32, (tm, 128), 0) + ti * tm
    first_half = (lane % HEAD_DIM) < ROPE_HALF
    cc = cc_ref[...]
    ss = ss_ref[...]

    def rope(xc):
        partner = jnp.where(first_half, pltpu.roll(xc, 128 - ROPE_HALF, 1), pltpu.roll(xc, ROPE_HALF, 1))
        return xc * cc + partner * ss

    @pl.when(ti == 0)
    def _():
        ccarry[...] = jnp.zeros_like(ccarry)
        pext[0:POOL_HALO, :] = jnp.zeros((POOL_HALO, BRANCH_WIDTH), F32)
        cext[0:CONV_HALO, :] = jnp.zeros((CONV_HALO, BRANCH_WIDTH), F32)

    g_small = seg(_P_VSW, 384)
    g_rope = seg(_P_ROPE, 512)

    z_t = jnp.transpose(g_small[:, 256:384])[0:N_MISC_T, :]
    gates = _sigmoid(z_t)
    row8 = lax.broadcasted_iota(jnp.int32, (8, tm), 0)
    zb = z_t[8:16] + fb_ref[...]
    logf = jnp.where(row8 >= 4, jnp.minimum(zb, 0.0) - jnp.log(1.0 + jnp.exp(-jnp.abs(zb))), 0.0)

    def split3(v):
        hi = v.astype(BF16).astype(F32)
        mid = (v - hi).astype(BF16).astype(F32)
        return hi, mid, (v - hi) - mid

    parts = jnp.concatenate(list(split3(logf)) + [jnp.zeros((8, tm), F32)], axis=0).astype(BF16)
    part_sums = _dot(parts, tri_ref[...])
    csum = (part_sums[0:8] + part_sums[8:16]) + part_sums[16:24] + ccarry[:, 0:1]
    ccarry[...] = jnp.broadcast_to(csum[:, tm - 1:tm], (8, 128))
    mt_ref[0:8, :] = gates[0:8]
    mt_ref[8:16, :] = jnp.where(row8 < 4, gates[8:16], csum)

    g_pool = seg(_P_POOL, 256)
    cv = seg(_P_CONV, 768)

    pext[POOL_HALO:POOL_HALO + tm, :] = g_pool

    def pld(k, c):
        return pext[pl.ds(POOL_HALO - k, tm), c * 128:(c + 1) * 128]

    left = lane < 64
    rowp1 = (row + 1).astype(F32)

    def cnt(w):
        return jnp.minimum(rowp1, float(w))

    e0 = pld(0, 0)
    s2 = e0 + pld(1, 0)
    s4 = s2 + pld(2, 0) + pld(3, 0)
    p0 = jnp.where(left, s2 / cnt(2), s4 / cnt(4)) - e0
    f0 = pld(0, 1)
    s8 = f0
    for k in range(1, 8):
        s8 = s8 + pld(k, 1)
    s16 = s8
    for k in range(8, 16):
        s16 = s16 + pld(k, 1)
    p1 = jnp.where(left, s8 / cnt(8), s16 / cnt(16)) - f0
    pw = pw_ref[...]
    mixed = _dot(p0.astype(BF16), pw[0:128, :]) + _dot(p1.astype(BF16), pw[128:256, :])
    opool_ref[...] = mixed * ps_ref[...]
    pext[0:POOL_HALO, :] = pext[tm:tm + POOL_HALO, :]

    g_fv = seg(_P_FOXV, 256)
    g_fk = seg(_P_FOXK, 512)
    g_fq = seg(_P_FOXQ, 256)

    q_ref[:, 0:128] = (rope(g_rope[:, 0:128]) * Q_SCALE).astype(BF16)
    q_ref[:, 128:256] = (rope(g_rope[:, 128:256]) * Q_SCALE).astype(BF16)
    block_onehot = jnp.where(lane - HEAD_DIM == row // SEL_LEN, 1.0, 0.0)
    ks_ref[...] = (rope(g_rope[:, 256:384]) + block_onehot).astype(BF16)
    kw_ref[...] = rope(g_rope[:, 384:512]).astype(BF16)
    vsw_t = jnp.transpose(g_small[:, 0:128]).astype(BF16)
    ones_upd = jnp.ones((V_ROWS - HEAD_DIM, UPD), BF16)
    for c in range(tm // UPD):
        vst_ref[c] = jnp.concatenate([vsw_t[0:HEAD_DIM, c * UPD:(c + 1) * UPD], ones_upd], axis=0)
    for c in range(tm // CHUNK):
        vwt_ref[c] = jnp.concatenate([vsw_t[HEAD_DIM:128, c * CHUNK:(c + 1) * CHUNK], ones_upd[:, 0:CHUNK]], axis=0)
    kvc_ref[...] = g_small[:, 128:256]

    fq_ref[...] = (g_fq * Q_SCALE).astype(BF16)
    terms = split3(csum * (-LOG2E))
    row16 = lax.broadcasted_iota(jnp.int32, (16, tm), 0)
    extra_t = jnp.zeros((16, tm), F32)
    for hd in range(N_HEADS):
        for j in range(3):
            src = jnp.broadcast_to(terms[j][4 + hd:5 + hd, :], (16, tm))
            extra_t = jnp.where(row16 == 3 * hd + j, src, extra_t)
    extra = jnp.transpose(jnp.concatenate(
        [jnp.zeros((HEAD_DIM, tm), F32), extra_t, jnp.zeros((128 - HEAD_DIM - 16, tm), F32)], axis=0))
    for hd in range(N_HEADS):
        fk_ref[:, hd * 128:(hd + 1) * 128] = (g_fk[:, hd * 128:(hd + 1) * 128] + extra).astype(BF16)
    fv_t = jnp.transpose(g_fv).astype(BF16)
    for c in range(tm // UPD):
        cols = slice(c * UPD, (c + 1) * UPD)
        fvt_ref[c] = jnp.concatenate(
            [piece for hd in range(N_HEADS) for piece in (fv_t[hd * HEAD_DIM:(hd + 1) * HEAD_DIM, cols], ones_upd)],
            axis=0)

    u = cv[:, 512:768] * cv[:, 0:256]
    cext[CONV_HALO:CONV_HALO + tm, :] = u
    y = (cext[pl.ds(CONV_HALO - 2, tm), :] * cw_ref[0:1, :]
         + cext[pl.ds(CONV_HALO - 1, tm), :] * cw_ref[1:2, :]
         + u * cw_ref[2:3, :])
    oconv_ref[...] = cv[:, 256:512] * y
    cext[0:CONV_HALO, :] = cext[tm:tm + CONV_HALO, :]

    hbuf[...] = _rmsnorm(xn_ref[...], g_ref[...]).astype(BF16)


def _proj_call(x, g, w1, cc, ss, fb, cw, pw, ps, tri):
    B, T, _ = x.shape
    tm = TM_PROJ
    nt = T // tm
    tok = lambda width: pl.BlockSpec((None, tm, width), lambda b, t: (b, t, 0))
    chunked = lambda rows, keys: pl.BlockSpec((None, tm // keys, rows, keys), lambda b, t: (b, t, 0, 0))
    const = lambda shape: pl.BlockSpec(shape, lambda b, t: (0,) * len(shape))
    out_shape = [
        jax.ShapeDtypeStruct((B, T, 256), BF16),
        jax.ShapeDtypeStruct((B, T, 128), BF16),
        jax.ShapeDtypeStruct((B, T, 128), BF16),
        jax.ShapeDtypeStruct((B, T // UPD, V_ROWS, UPD), BF16),
        jax.ShapeDtypeStruct((B, T // CHUNK, V_ROWS, CHUNK), BF16),
        jax.ShapeDtypeStruct((B, T, 128), F32),
        jax.ShapeDtypeStruct((B, nt, N_MISC_T, tm), F32),
        jax.ShapeDtypeStruct((B, T, 256), BF16),
        jax.ShapeDtypeStruct((B, T, 512), BF16),
        jax.ShapeDtypeStruct((B, T // UPD, N_HEADS * V_ROWS, UPD), BF16),
        jax.ShapeDtypeStruct((B, T, 256), F32),
        jax.ShapeDtypeStruct((B, T, 256), F32),
    ]
    out_specs = [tok(256), tok(128), tok(128), chunked(V_ROWS, UPD), chunked(V_ROWS, CHUNK), tok(128),
                 pl.BlockSpec((None, None, N_MISC_T, tm), lambda b, t: (b, t, 0, 0)),
                 tok(256), tok(512), chunked(N_HEADS * V_ROWS, UPD), tok(256), tok(256)]
    def next_tile(b, t):
        lin = jnp.minimum(b * nt + t + 1, B * nt - 1)
        return lin // nt, lin % nt, 0

    in_specs = [tok(D_MODEL), pl.BlockSpec((None, tm, D_MODEL), next_tile),
                const((1, D_MODEL)), const((D_MODEL, _P_COLS)),
                pl.BlockSpec((tm, 128), lambda b, t: (t, 0)), pl.BlockSpec((tm, 128), lambda b, t: (t, 0)),
                const((8, 1)), const((3, 256)), const((256, 256)), const((1, 256)), const((tm, tm))]
    return pl.pallas_call(
        functools.partial(_proj_kernel, tm=tm),
        grid=(B, nt),
        in_specs=in_specs,
        out_specs=out_specs,
        out_shape=out_shape,
        scratch_shapes=[pltpu.VMEM((tm + POOL_HALO, 256), F32), pltpu.VMEM((tm + CONV_HALO, 256), F32),
                        pltpu.VMEM((8, 128), F32), pltpu.VMEM((tm, D_MODEL), BF16)],
        compiler_params=pltpu.CompilerParams(dimension_semantics=("arbitrary", "arbitrary"),
                                             vmem_limit_bytes=VMEM_LIMIT),
        name="proj",
    )(x, x, g, w1, cc, ss, fb, cw, pw, ps, tri)


def _compress_kernel(x_ref, pt_ref, pb_ref, wt_ref, wb_ref, b1_ref, w2_ref, b2_ref, cc_ref, ss_ref,
                     kc_ref, vct_ref, *, rows):
    x = x_ref[...]
    a = _dot((x + pt_ref[...]).astype(BF16), wt_ref[...])
    b = _dot((x + pb_ref[...]).astype(BF16), wb_ref[...])
    hid = a + pltpu.roll(b, rows - 1, 0) + b1_ref[...]
    act = hid * _sigmoid(hid)
    out = _dot(act.astype(BF16), w2_ref[...]) + b2_ref[...]
    lane = lax.broadcasted_iota(jnp.int32, (rows, 128), 1)
    roped = out[:, 0:128] * cc_ref[...] + out[:, 128:256] * ss_ref[...]
    block = lax.broadcasted_iota(jnp.int32, (rows, 128), 0)
    group_onehot = jnp.where(lane - HEAD_DIM == block // CMP_GROUP, 1.0, 0.0)
    kc_ref[...] = jnp.where(lane < HEAD_DIM, roped, group_onehot).astype(BF16)
    vct_ref[...] = jnp.transpose(out[:, 256:384])[0:HEAD_DIM, :].astype(BF16)


def _compress_call(x16, pt, pb, wt, wb, b1, w2, b2, cc, ss):
    B, rows, width = x16.shape
    const = lambda shape: pl.BlockSpec(shape, lambda b: (0,) * len(shape))
    return pl.pallas_call(
        functools.partial(_compress_kernel, rows=rows),
        grid=(B,),
        in_specs=[pl.BlockSpec((None, rows, width), lambda b: (b, 0, 0)),
                  const((1, width)), const((1, width)), const((width, 256)), const((width, 256)),
                  const((1, 256)), const((256, 384)), const((1, 384)), const((rows, 128)), const((rows, 128))],
        out_specs=[pl.BlockSpec((None, rows, 128), lambda b: (b, 0, 0)),
                   pl.BlockSpec((None, HEAD_DIM, rows), lambda b: (b, 0, 0))],
        out_shape=[jax.ShapeDtypeStruct((B, rows, 128), BF16), jax.ShapeDtypeStruct((B, HEAD_DIM, rows), BF16)],
        compiler_params=pltpu.CompilerParams(dimension_semantics=("arbitrary",),
                                             vmem_limit_bytes=VMEM_LIMIT),
        name="compress",
    )(x16, pt, pb, wt, wb, b1, w2, b2, cc, ss)


def _nsa_kernel(q_ref, ks_ref, kw_ref, vst_ref, vwt_ref, kc_ref, vct_ref, mt_ref, ovt_ref,
                o_ref, key_ref, sbuf, cbuf, *, nblk, ncmp):
    step = pl.program_id(1)
    tq = CHUNK
    nq = N_HEADS * tq
    hw = nq // 2
    halves = (slice(0, hw), slice(hw, nq))
    n_c = KV_TILE // UPD
    n_win = WINDOW // tq
    subs = range(NSA_SUB)
    qis = [step * NSA_SUB + u for u in subs]
    q_local = lax.broadcasted_iota(jnp.int32, (1, nq), 1) % tq
    k_local = lax.broadcasted_iota(jnp.int32, (tq, hw), 0)
    q_loc = q_local[:, 0:hw]
    causal = k_local <= q_loc

    q4, qt_plain = [], []
    group = lax.broadcasted_iota(jnp.int32, (HEAD_DIM, nq), 0)
    slab = 2 * CMP_GROUP
    n_slab = lax.broadcasted_iota(jnp.int32, (slab, nq), 0)
    for u in subs:
        q_t = jnp.transpose(q_ref[u * tq:(u + 1) * tq, :].astype(F32))
        q4.append(jnp.concatenate([q_t[HEAD_DIM * h:HEAD_DIM * (h + 1)] for h in range(N_HEADS)], axis=1))
        qt_plain.append(jnp.concatenate([q4[u], jnp.zeros((HEAD_DIM, nq), F32)], axis=0).astype(BF16))
        group_bias = jnp.where(group > qis[u], MASK_VALUE, 0.0)
        qt_cmp = jnp.concatenate([q4[u], group_bias], axis=0).astype(BF16)
        cbuf[u] = _dot(kc_ref[...], qt_cmp)
        start = pl.multiple_of(jnp.maximum(qis[u] - 1, 0) * CMP_GROUP, CMP_GROUP)
        visible = ((start + n_slab) * CMP_STRIDE + (CMP_LEN - 1)) <= qis[u] * tq + q_local
        cbuf[u, pl.ds(start, slab), :] = jnp.where(visible, cbuf[u, pl.ds(start, slab), :], MASK_VALUE)

    o_cmp, imp = [], []
    for u in subs:
        s = cbuf[u]
        mc = jnp.max(s, axis=0, keepdims=True)
        mc = jnp.where(mc <= MASK_VALUE, 0.0, mc)
        pc = jnp.exp2(s - mc)
        lc = jnp.sum(pc, axis=0, keepdims=True)
        pc = pc * (1.0 / jnp.where(lc > 0.0, lc, 1.0))
        o_cmp.append(_dot(vct_ref[...], pc.astype(BF16)))
        psum = (pc[:, 0:tq] + pc[:, tq:2 * tq]) + (pc[:, 2 * tq:3 * tq] + pc[:, 3 * tq:4 * tq])
        imp.append(_dot(ovt_ref[...], psum, precision=lax.Precision.HIGHEST))

    def win_stream(u, hs):
        qi = qis[u]

        def score(c):
            if c == 0:
                own = kw_ref[pl.ds(pl.multiple_of(qi * tq, tq), tq), :]
                return jnp.where(causal, _dot(own, qt_plain[u][:, hs]), MASK_VALUE)
            ci = qi - n_win + (c - 1)
            start = pl.multiple_of(jnp.maximum(ci, 0) * tq, tq)
            s = _dot(kw_ref[pl.ds(start, tq), :], qt_plain[u][:, hs])
            ok = ((k_local > q_loc) & (ci >= 0)) if c == 1 else jnp.broadcast_to(ci >= 0, (tq, hw))
            return jnp.where(ok, s, MASK_VALUE)

        def value(c):
            return vwt_ref[qi] if c == 0 else vwt_ref[jnp.maximum(qi - n_win + (c - 1), 0)]

        return score, value, _flash_init(hw)

    carries_w = _attend_chunks([win_stream(u, hs) for u in subs for hs in halves], n_win + 1)
    o_win = [jnp.concatenate([acc / l for _, l, acc in carries_w[2 * u:2 * u + 2]], axis=1) for u in subs]

    jb = lax.broadcasted_iota(jnp.int32, (nblk, tq), 0)
    sub8 = lax.broadcasted_iota(jnp.int32, (8, tq), 0)
    later_rows = [sub8 > r for r in range(8)]
    qt_sel = []
    for u in subs:
        tt = qis[u] * tq + lax.broadcasted_iota(jnp.int32, (nblk, tq), 1)
        jt = tt // SEL_LEN
        forced = (jb == 0) | (jb == jt) | (jb == jt - 1)
        valid = jb * SEL_LEN <= tt
        key = jnp.where(valid, jnp.where(forced, imp[u] + FORCE_BONUS, imp[u]), MASK_VALUE)
        key_ref[u] = key
        key_rows = [key[8 * v:8 * v + 8] for v in range(nblk // 8)]
        ranks = [jnp.zeros((8, tq), F32) for _ in key_rows]
        for jp in range(nblk):
            other = jnp.broadcast_to(key_ref[u, pl.ds(jp, 1), :], (8, tq))
            for v, kv in enumerate(key_rows):
                if 8 * v > jp:
                    ahead = other >= kv
                elif 8 * v + 7 <= jp:
                    ahead = other > kv
                else:
                    ahead = (other > kv) | ((other == kv) & later_rows[jp - 8 * v])
                ranks[v] = ranks[v] + jnp.where(ahead, 1.0, 0.0)
        selected = jnp.concatenate(ranks, axis=0) < float(min(N_SEL, nblk))
        sel_bias = jnp.where(selected, 0.0, MASK_VALUE)
        if nblk < HEAD_DIM:
            sel_bias = jnp.concatenate([sel_bias, jnp.zeros((HEAD_DIM - nblk, tq), F32)], axis=0)
        qt_sel.append(jnp.concatenate([q4[u], jnp.concatenate([sel_bias] * N_HEADS, axis=1)], axis=0).astype(BF16))

    def slc_scores(u, kt, c):
        start = pl.multiple_of((kt * n_c + c) * UPD, UPD)
        return _dot(ks_ref[pl.ds(start, UPD), :], qt_sel[u])

    def slc_update(u, carry, s, kt, c):
        return tuple(_flash_t(s[:, hs], *carry[i], vst_ref[kt * n_c + c]) for i, hs in enumerate(halves))

    for c in range(n_c):
        for u in subs:
            sbuf[u, c] = slc_scores(u, 0, c)

    def slc_tile(kt, carries):
        carries = list(carries)
        for c in range(n_c):
            for u in subs:
                s = sbuf[u, c]
                sbuf[u, c] = slc_scores(u, kt + 1, c)
                carries[u] = slc_update(u, carries[u], s, kt, c)
        return tuple(carries)

    carries = list(lax.fori_loop(0, step, slc_tile, tuple((_flash_init(hw), _flash_init(hw)) for u in subs)))
    k_upd = lax.broadcasted_iota(jnp.int32, (UPD, nq), 0)
    for c in range(n_c):
        first, last = c * UPD // tq, (c + 1) * UPD // tq - 1
        for u in subs:
            if first > u:
                continue
            s = sbuf[u, c]
            if last >= u:
                s = jnp.where(c * UPD + k_upd <= u * tq + q_local, s, MASK_VALUE)
            carries[u] = slc_update(u, carries[u], s, step, c)

    for u in subs:
        o_slc = jnp.concatenate([acc / l for _, l, acc in carries[u]], axis=1)
        g = mt_ref[:, u * tq:(u + 1) * tq]
        heads = []
        for h in range(N_HEADS):
            cols = slice(h * tq, (h + 1) * tq)
            heads.append(g[h:h + 1] * o_cmp[u][:, cols] + g[4 + h:5 + h] * o_slc[:, cols]
                         + g[8 + h:9 + h] * o_win[u][:, cols])
        o_ref[u * tq:(u + 1) * tq, :] = jnp.transpose(jnp.concatenate(heads, axis=0))


def _nsa_call(q, ks, kw, vst, vwt, kc, vct, mt, ovt):
    B, T, _ = q.shape
    nblk, ncmp = ovt.shape
    tq = CHUNK * NSA_SUB
    per_mt = TM_PROJ // tq
    full = lambda width: pl.BlockSpec((None, T, width), lambda b, i: (b, 0, 0))
    vfull = lambda keys: pl.BlockSpec((None, T // keys, V_ROWS, keys), lambda b, i: (b, 0, 0, 0))
    return pl.pallas_call(
        functools.partial(_nsa_kernel, nblk=nblk, ncmp=ncmp),
        grid=(B, T // tq),
        in_specs=[pl.BlockSpec((None, tq, 256), lambda b, i: (b, i, 0)),
                  full(128), full(128), vfull(UPD), vfull(CHUNK),
                  pl.BlockSpec((None, ncmp, 128), lambda b, i: (b, 0, 0)),
                  pl.BlockSpec((None, HEAD_DIM, ncmp), lambda b, i: (b, 0, 0)),
                  pl.BlockSpec((None, None, N_MISC_T, tq), lambda b, i: (b, i // per_mt, 0, i % per_mt)),
                  pl.BlockSpec((nblk, ncmp), lambda b, i: (0, 0))],
        out_specs=pl.BlockSpec((None, tq, 256), lambda b, i: (b, i, 0)),
        out_shape=jax.ShapeDtypeStruct((B, T, 256), F32),
        scratch_shapes=[pltpu.VMEM((NSA_SUB, nblk, CHUNK), F32),
                        pltpu.VMEM((NSA_SUB, KV_TILE // UPD, UPD, N_HEADS * CHUNK), F32),
                        pltpu.VMEM((NSA_SUB, ncmp, N_HEADS * CHUNK), F32)],
        compiler_params=pltpu.CompilerParams(dimension_semantics=("arbitrary", "arbitrary"),
                                             vmem_limit_bytes=VMEM_LIMIT),
        name="nsa",
    )(q, ks, kw, vst, vwt, kc, vct, mt, ovt)


def _fox_kernel(q_ref, k_ref, vt_ref, o_ref, sbuf):
    qi = pl.program_id(1)
    nq = TQ_FOX
    n_chunk = KV_TILE // UPD
    t_lane = qi * nq + lax.broadcasted_iota(jnp.int32, (1, nq), 1)
    k_local = lax.broadcasted_iota(jnp.int32, (UPD, nq), 0)
    q_t = jnp.transpose(q_ref[...].astype(F32))
    extra_row = lax.broadcasted_iota(jnp.int32, (HEAD_DIM, nq), 0)

    def ones_rows(h):
        return jnp.where((extra_row >= 3 * h) & (extra_row < 3 * h + 3), 1.0, 0.0)

    qts = [jnp.concatenate([q_t[HEAD_DIM * h:HEAD_DIM * (h + 1)], ones_rows(h)], axis=0).astype(BF16)
           for h in range(N_HEADS)]

    def scores(kt, c, h):
        start = pl.multiple_of((kt * n_chunk + c) * UPD, UPD)
        return _dot(k_ref[pl.ds(start, UPD), h * 128:(h + 1) * 128], qts[h])

    def value(kt, c, h):
        return vt_ref[kt * n_chunk + c, h * V_ROWS:(h + 1) * V_ROWS, :]

    for c in range(n_chunk):
        for h in range(N_HEADS):
            sbuf[h, c] = scores(0, c, h)

    def tile(kt, carries):
        carries = list(carries)
        for c in range(n_chunk):
            for h in range(N_HEADS):
                s = sbuf[h, c]
                sbuf[h, c] = scores(kt + 1, c, h)
                carries[h] = _flash_t(s, *carries[h], value(kt, c, h))
        return tuple(carries)

    carries = list(lax.fori_loop(0, qi, tile, tuple(_flash_init(nq) for _ in range(N_HEADS))))
    for c in range(n_chunk):
        causal = qi * nq + c * UPD + k_local <= t_lane
        for h in range(N_HEADS):
            carries[h] = _flash_t(jnp.where(causal, sbuf[h, c], MASK_VALUE), *carries[h], value(qi, c, h))
    o_ref[...] = jnp.transpose(jnp.concatenate([acc / l for _, l, acc in carries], axis=0))


def _fox_call(fq, fk, fvt):
    B, T, _ = fq.shape
    tq = TQ_FOX
    return pl.pallas_call(
        _fox_kernel,
        grid=(B, T // tq),
        in_specs=[pl.BlockSpec((None, tq, 256), lambda b, i: (b, i, 0)),
                  pl.BlockSpec((None, T, 512), lambda b, i: (b, 0, 0)),
                  pl.BlockSpec((None, T // UPD, N_HEADS * V_ROWS, UPD), lambda b, i: (b, 0, 0, 0))],
        out_specs=pl.BlockSpec((None, tq, 256), lambda b, i: (b, i, 0)),
        out_shape=jax.ShapeDtypeStruct((B, T, 256), F32),
        scratch_shapes=[pltpu.VMEM((N_HEADS, KV_TILE // UPD, UPD, tq), F32)],
        compiler_params=pltpu.CompilerParams(dimension_semantics=("arbitrary", "arbitrary"),
                                             vmem_limit_bytes=VMEM_LIMIT),
        name="fox",
    )(fq, fk, fvt)


def _merge_kernel(x_ref, g_ref, wg_ref, wm_ref, wb_ref, wo_ref, fg_ref, o0_ref, o1_ref, o2_ref, o3_ref,
                  out_ref, *, last, tm):
    rows = tm // 2
    halves = [slice(u * rows, (u + 1) * rows) for u in range(2)]
    xs = [x_ref[r, :] for r in halves]
    hs = [_rmsnorm(x, g_ref[...]).astype(BF16) for x in xs]
    accs = []
    for u, r in enumerate(halves):
        gates = _dot(hs[u], wg_ref[...])
        acc = None
        for i, o_ref in enumerate((o0_ref, o1_ref, o2_ref, o3_ref)):
            merge = _dot(hs[u], wm_ref[:, i * D_MODEL:(i + 1) * D_MODEL])
            gate = gates[:, i * BRANCH_WIDTH:(i + 1) * BRANCH_WIDTH]
            a = (o_ref[r, :] * (gate * _sigmoid(gate))).astype(BF16)
            term = _sigmoid(merge) * _dot(a, wb_ref[i])
            acc = term if acc is None else acc + term
        accs.append(acc)
    for u, r in enumerate(halves):
        y = xs[u] + _dot(accs[u].astype(BF16), wo_ref[...])
        if last:
            y = _rmsnorm(y, fg_ref[...])
        out_ref[r, :] = y


def _merge_call(x, g, wg, wm, wb, wo, fg, o_nsa, o_pool, o_conv, o_fox, last):
    B, T, _ = x.shape
    tm = TM_MERGE
    tok = lambda width: pl.BlockSpec((None, tm, width), lambda b, t: (b, t, 0))
    const = lambda shape: pl.BlockSpec(shape, lambda b, t: (0,) * len(shape))
    return pl.pallas_call(
        functools.partial(_merge_kernel, last=last, tm=tm),
        grid=(B, T // tm),
        in_specs=[tok(D_MODEL), const((1, D_MODEL)), const((D_MODEL, N_BRANCH * BRANCH_WIDTH)),
                  const((D_MODEL, N_BRANCH * D_MODEL)), const((N_BRANCH, BRANCH_WIDTH, D_MODEL)),
                  const((D_MODEL, D_MODEL)), const((1, D_MODEL)), tok(256), tok(256), tok(256), tok(256)],
        out_specs=tok(D_MODEL),
        out_shape=jax.ShapeDtypeStruct((B, T, D_MODEL), F32),
        compiler_params=pltpu.CompilerParams(dimension_semantics=("arbitrary", "arbitrary"),
                                             vmem_limit_bytes=VMEM_LIMIT),
        name="merge",
    )(x, g, wg, wm, wb, wo, fg, o_nsa, o_pool, o_conv, o_fox)


def _rope_tables(pos):
    n = pos.shape[0]
    inv = ROPE_THETA ** (-jnp.arange(ROPE_HALF, dtype=F32) / ROPE_HALF)
    ang = pos.astype(F32)[:, None] * inv[None, :]
    cos, sin = jnp.cos(ang), jnp.sin(ang)
    rest = HEAD_DIM - 2 * ROPE_HALF
    c = jnp.concatenate([cos, cos, jnp.ones((n, rest), F32)], axis=1)
    s = jnp.concatenate([-sin, sin, jnp.zeros((n, rest), F32)], axis=1)
    return jnp.concatenate([c, c], axis=1), jnp.concatenate([s, s], axis=1)


def _proj_weight(w_in):
    col = lambda name, width: w_in[:, _OFF[name]:_OFF[name] + width]
    z64 = jnp.zeros((D_MODEL, HEAD_DIM), F32)
    misc = jnp.concatenate([col("nsa_g", 12), col("fox_f", 4), jnp.zeros((D_MODEL, 112), F32)], axis=1)
    fox_k = []
    for h in range(N_HEADS):
        fox_k += [w_in[:, _OFF["fox_k"] + h * HEAD_DIM:_OFF["fox_k"] + (h + 1) * HEAD_DIM], z64]
    w = jnp.concatenate([col("nsa_q", 256), col("k_s", 64), z64, col("k_w", 64), z64,
                         col("v_s", 64), col("v_w", 64), col("k_c", 128), misc,
                         col("fox_q", 256)] + fox_k + [col("fox_v", 256), col("pool", 256), col("conv", 768)],
                        axis=1)
    assert w.shape[1] == _P_COLS
    return w.astype(BF16)


def _compress_weights(cmp_pos, cmp_w1, cmp_b1, cmp_w2, cmp_b2):
    half = CMP_LEN // 2

    def w1_half(lo):
        wk = cmp_w1[0, lo * HEAD_DIM:(lo + half) * HEAD_DIM].reshape(half, HEAD_DIM, CMP_HIDDEN)
        wv = cmp_w1[1, lo * HEAD_DIM:(lo + half) * HEAD_DIM].reshape(half, HEAD_DIM, CMP_HIDDEN)
        z = jnp.zeros_like(wk)
        top = jnp.concatenate([wk, z], axis=2)
        bot = jnp.concatenate([z, wv], axis=2)
        return jnp.concatenate([top, bot], axis=1).reshape(half * 2 * HEAD_DIM, 2 * CMP_HIDDEN).astype(BF16)

    def pos_half(lo):
        return jnp.concatenate([cmp_pos[0, lo:lo + half], cmp_pos[1, lo:lo + half]], axis=1).reshape(1, -1)

    lane = np.arange(HEAD_DIM)
    perm = np.where(lane < ROPE_HALF, lane + ROPE_HALF, np.where(lane < 2 * ROPE_HALF, lane - ROPE_HALF, lane))
    w2k, w2v = cmp_w2[0], cmp_w2[1]
    zk = jnp.zeros_like(w2k)
    w2 = jnp.concatenate([
        jnp.concatenate([w2k, w2k, w2k[:, perm], w2k[:, perm], zk, zk], axis=1),
        jnp.concatenate([zk, zk, zk, zk, w2v, w2v], axis=1)], axis=0).astype(BF16)
    b2k, b2v = cmp_b2[0], cmp_b2[1]
    b2 = jnp.concatenate([b2k, b2k, b2k[perm], b2k[perm], b2v, b2v])[None, :]
    b1 = jnp.concatenate([cmp_b1[0], cmp_b1[1]])[None, :]
    return pos_half(0), pos_half(half), w1_half(0), w1_half(half), b1, w2, b2


def _block_diag(pool_w):
    n, c, _ = pool_w.shape
    out = jnp.zeros((n * c, n * c), pool_w.dtype)
    for i in range(n):
        out = out.at[i * c:(i + 1) * c, i * c:(i + 1) * c].set(pool_w[i])
    return out


def kernel(x, norm_g, w_in, fox_f_bias, cmp_pos, cmp_w1, cmp_b1, cmp_w2, cmp_b2, pool_w, pool_scale, conv_w,
           w_branch, w_out, final_norm_g):
    B, T, _ = x.shape
    depth = norm_g.shape[0]
    assert T % TM_PROJ == 0 and T % KV_TILE == 0 and T // SEL_LEN <= HEAD_DIM and T >= WINDOW + CHUNK
    assert NSA_SUB * CHUNK == KV_TILE == TQ_FOX and TM_PROJ % KV_TILE == 0
    rows = T // CMP_STRIDE
    nblk = T // SEL_LEN

    cc_tok, ss_tok = _rope_tables(jnp.arange(T))
    cmp_end = jnp.arange(rows) * CMP_STRIDE + CMP_LEN - 1
    cc_cmp, ss_cmp = _rope_tables(cmp_end)
    ci = np.arange(rows)[:, None] * CMP_STRIDE
    sj = np.arange(nblk)[None, :] * SEL_LEN
    overlap_t = jnp.asarray(((ci < sj + SEL_LEN) & (ci + CMP_LEN > sj)).T, F32)
    tri = jnp.asarray(np.triu(np.ones((TM_PROJ, TM_PROJ), np.float32)), BF16)

    for l in range(depth):
        w1 = _proj_weight(w_in[l])
        fb = jnp.zeros((8, 1), F32).at[4:8, 0].set(fox_f_bias[l])
        (q, ks, kw, vst, vwt, kvc, mt, fq, fk, fvt, o_pool, o_conv) = _proj_call(
            x, norm_g[l][None, :], w1, cc_tok, ss_tok, fb, conv_w[l], _block_diag(pool_w[l]).astype(BF16),
            pool_scale[l][None, :], tri)
        x16 = kvc.reshape(B, rows, CMP_STRIDE * 2 * HEAD_DIM)
        kc, vct = _compress_call(x16, *_compress_weights(cmp_pos[l], cmp_w1[l], cmp_b1[l], cmp_w2[l], cmp_b2[l]),
                                 cc_cmp, ss_cmp)
        o_nsa = _nsa_call(q, ks, kw, vst, vwt, kc, vct, mt, overlap_t)
        o_fox = _fox_call(fq, fk, fvt)
        wg = w_in[l][:, _OFF["gate"]:_OFF["merge"]].astype(BF16)
        wm = w_in[l][:, _OFF["merge"]:_OFF["end"]].astype(BF16)
        x = _merge_call(x, norm_g[l][None, :], wg, wm, w_branch[l].astype(BF16), w_out[l].astype(BF16),
                        final_norm_g[None, :], o_nsa, o_pool, o_conv, o_fox, last=(l == depth - 1))
    return x
```

```python
import functools

import numpy as np
import jax
import jax.numpy as jnp
from jax import lax
from jax.experimental import pallas as pl
from jax.experimental.pallas import tpu as pltpu

F32 = jnp.float32
BF16 = jnp.bfloat16

D_MODEL = 1024
N_BRANCH = 4
BRANCH_WIDTH = 256
HEAD_DIM = 64
N_HEADS = 4
CMP_LEN = 32
CMP_STRIDE = 16
CMP_HIDDEN = 128
CMP_GROUP = 8
SEL_LEN = 64
N_SEL = 16
WINDOW = 512
FORCE_BONUS = 1.0e4
ROPE_THETA = 500000.0
ROPE_HALF = 8
NORM_EPS = 1e-6
MASK_VALUE = -1e30

_OFF = dict(nsa_q=0, k_c=256, v_c=320, k_s=384, v_s=448, k_w=512, v_w=576, nsa_g=640, pool=652,
            conv=908, fox_q=1676, fox_k=1932, fox_v=2188, fox_f=2444, gate=2448, merge=3472, end=7568)

_P_ROPE = 0
_P_VSW = 512
_P_FOXQ = 896
_P_FOXK = 1152
_P_FOXV = 1664
_P_POOL = 1920
_P_CONV = 2176
_P_COLS = 2944

CHUNK = 128
UPD = 256
V_ROWS = HEAD_DIM + 16
LOG2E = 1.4426950408889634
Q_SCALE = HEAD_DIM ** -0.5 * LOG2E
TM_PROJ = 512
KV_TILE = 512
NSA_SUB = 4
TQ_FOX = 512
TM_MERGE = 512
N_MISC_T = 16
POOL_HALO = 16
CONV_HALO = 8
VMEM_LIMIT = 56 * 1024 * 1024


def _dot(a, b, precision=None):
    return jnp.dot(a, b, preferred_element_type=F32, precision=precision)


def _sigmoid(x):
    return 1.0 / (1.0 + jnp.exp(-x))


def _rmsnorm(x, g):
    return x * lax.rsqrt(jnp.mean(x * x, axis=-1, keepdims=True) + NORM_EPS) * g


def _flash_t(s, m, l, acc, v_t):
    m_new = jnp.maximum(m, jnp.max(s, axis=0, keepdims=True))
    alpha = jnp.exp2(m - m_new)
    pv = _dot(v_t, jnp.exp2(s - m_new).astype(BF16))
    l = alpha * l + pv[HEAD_DIM:HEAD_DIM + 1]
    acc = alpha * acc + pv[0:HEAD_DIM]
    return m_new, l, acc


def _attend_chunks(streams, n):
    nxt = [st[0](0) for st in streams]
    carries = [st[2] for st in streams]
    for c in range(n):
        cur = nxt
        if c + 1 < n:
            nxt = [st[0](c + 1) for st in streams]
        carries = [_flash_t(cur[i], *carries[i], streams[i][1](c)) for i in range(len(streams))]
    return carries


def _flash_init(n):
    return (jnp.full((1, n), MASK_VALUE, F32), jnp.zeros((1, n), F32), jnp.zeros((HEAD_DIM, n), F32))


def _proj_kernel(x_ref, xn_ref, g_ref, w_ref, cc_ref, ss_ref, fb_ref, cw_ref, pw_ref, ps_ref, tri_ref,
                 q_ref, ks_ref, kw_ref, vst_ref, vwt_ref, kvc_ref, mt_ref, fq_ref, fk_ref, fvt_ref,
                 opool_ref, oconv_ref, pext, cext, ccarry, hbuf, *, tm):
    ti = pl.program_id(1)

    @pl.when((pl.program_id(0) == 0) & (ti == 0))
    def _():
        hbuf[...] = _rmsnorm(x_ref[...], g_ref[...]).astype(BF16)

    h = hbuf[...]

    def seg(lo, width):
        return _dot(h, w_ref[:, lo:lo + width])

    lane = lax.broadcasted_iota(jnp.int32, (tm, 128), 1)
    row = lax.broadcasted_iota(jnp.int32, (tm, 128), 0) + ti * tm
    first_half = (lane % HEAD_DIM) < ROPE_HALF
    cc = cc_ref[...]
    ss = ss_ref[...]

    def rope(xc):
        partner = jnp.where(first_half, pltpu.roll(xc, 128 - ROPE_HALF, 1), pltpu.roll(xc, ROPE_HALF, 1))
        return xc * cc + partner * ss

    @pl.when(ti == 0)
    def _():
        ccarry[...] = jnp.zeros_like(ccarry)
        pext[0:POOL_HALO, :] = jnp.zeros((POOL_HALO, BRANCH_WIDTH), F32)
        cext[0:CONV_HALO, :] = jnp.zeros((CONV_HALO, BRANCH_WIDTH), F32)

    g_small = seg(_P_VSW, 384)
    g_rope = seg(_P_ROPE, 512)

    z_t = jnp.transpose(g_small[:, 256:384])[0:N_MISC_T, :]
    gates = _sigmoid(z_t)
    row8 = lax.broadcasted_iota(jnp.int32, (8, tm), 0)
    zb = z_t[8:16] + fb_ref[...]
    logf = jnp.where(row8 >= 4, jnp.minimum(zb, 0.0) - jnp.log(1.0 + jnp.exp(-jnp.abs(zb))), 0.0)

    def split3(v):
        hi = v.astype(BF16).astype(F32)
        mid = (v - hi).astype(BF16).astype(F32)
        return hi, mid, (v - hi) - mid

    parts = jnp.concatenate(list(split3(logf)) + [jnp.zeros((8, tm), F32)], axis=0).astype(BF16)
    part_sums = _dot(parts, tri_ref[...])
    csum = (part_sums[0:8] + part_sums[8:16]) + part_sums[16:24] + ccarry[:, 0:1]
    ccarry[...] = jnp.broadcast_to(csum[:, tm - 1:tm], (8, 128))
    mt_ref[0:8, :] = gates[0:8]
    mt_ref[8:16, :] = jnp.where(row8 < 4, gates[8:16], csum)

    g_pool = seg(_P_POOL, 256)
    cv = seg(_P_CONV, 768)

    pext[POOL_HALO:POOL_HALO + tm, :] = g_pool

    def pld(k, c):
        return pext[pl.ds(POOL_HALO - k, tm), c * 128:(c + 1) * 128]

    left = lane < 64
    rowp1 = (row + 1).astype(F32)

    def cnt(w):
        return jnp.minimum(rowp1, float(w))

    e0 = pld(0, 0)
    s2 = e0 + pld(1, 0)
    s4 = s2 + pld(2, 0) + pld(3, 0)
    p0 = jnp.where(left, s2 / cnt(2), s4 / cnt(4)) - e0
    f0 = pld(0, 1)
    s8 = f0
    for k in range(1, 8):
        s8 = s8 + pld(k, 1)
    s16 = s8
    for k in range(8, 16):
        s16 = s16 + pld(k, 1)
    p1 = jnp.where(left, s8 / cnt(8), s16 / cnt(16)) - f0
    pw = pw_ref[...]
    mixed = _dot(p0.astype(BF16), pw[0:128, :]) + _dot(p1.astype(BF16), pw[128:256, :])
    opool_ref[...] = mixed * ps_ref[...]
    pext[0:POOL_HALO, :] = pext[tm:tm + POOL_HALO, :]

    g_fv = seg(_P_FOXV, 256)
    g_fk = seg(_P_FOXK, 512)
    g_fq = seg(_P_FOXQ, 256)

    q_ref[:, 0:128] = (rope(g_rope[:, 0:128]) * Q_SCALE).astype(BF16)
    q_ref[:, 128:256] = (rope(g_rope[:, 128:256]) * Q_SCALE).astype(BF16)
    block_onehot = jnp.where(lane - HEAD_DIM == row // SEL_LEN, 1.0, 0.0)
    ks_ref[...] = (rope(g_rope[:, 256:384]) + block_onehot).astype(BF16)
    kw_ref[...] = rope(g_rope[:, 384:512]).astype(BF16)
    vsw_t = jnp.transpose(g_small[:, 0:128]).astype(BF16)
    ones_upd = jnp.ones((V_ROWS - HEAD_DIM, UPD), BF16)
    for c in range(tm // UPD):
        vst_ref[c] = jnp.concatenate([vsw_t[0:HEAD_DIM, c * UPD:(c + 1) * UPD], ones_upd], axis=0)
    for c in range(tm // CHUNK):
        vwt_ref[c] = jnp.concatenate([vsw_t[HEAD_DIM:128, c * CHUNK:(c + 1) * CHUNK], ones_upd[:, 0:CHUNK]], axis=0)
    kvc_ref[...] = g_small[:, 128:256]

    fq_ref[...] = (g_fq * Q_SCALE).astype(BF16)
    terms = split3(csum * (-LOG2E))
    row16 = lax.broadcasted_iota(jnp.int32, (16, tm), 0)
    extra_t = jnp.zeros((16, tm), F32)
    for hd in range(N_HEADS):
        for j in range(3):
            src = jnp.broadcast_to(terms[j][4 + hd:5 + hd, :], (16, tm))
            extra_t = jnp.where(row16 == 3 * hd + j, src, extra_t)
    extra = jnp.transpose(jnp.concatenate(
        [jnp.zeros((HEAD_DIM, tm), F32), extra_t, jnp.zeros((128 - HEAD_DIM - 16, tm), F32)], axis=0))
    for hd in range(N_HEADS):
        fk_ref[:, hd * 128:(hd + 1) * 128] = (g_fk[:, hd * 128:(hd + 1) * 128] + extra).astype(BF16)
    fv_t = jnp.transpose(g_fv).astype(BF16)
    for c in range(tm // UPD):
        cols = slice(c * UPD, (c + 1) * UPD)
        fvt_ref[c] = jnp.concatenate(
            [piece for hd in range(N_HEADS) for piece in (fv_t[hd * HEAD_DIM:(hd + 1) * HEAD_DIM, cols], ones_upd)],
            axis=0)

    u = cv[:, 512:768] * cv[:, 0:256]
    cext[CONV_HALO:CONV_HALO + tm, :] = u
    y = (cext[pl.ds(CONV_HALO - 2, tm), :] * cw_ref[0:1, :]
         + cext[pl.ds(CONV_HALO - 1, tm), :] * cw_ref[1:2, :]
         + u * cw_ref[2:3, :])
    oconv_ref[...] = cv[:, 256:512] * y
    cext[0:CONV_HALO, :] = cext[tm:tm + CONV_HALO, :]

    hbuf[...] = _rmsnorm(xn_ref[...], g_ref[...]).astype(BF16)


def _proj_call(x, g, w1, cc, ss, fb, cw, pw, ps, tri):
    B, T, _ = x.shape
    tm = TM_PROJ
    nt = T // tm
    tok = lambda width: pl.BlockSpec((None, tm, width), lambda b, t: (b, t, 0))
    chunked = lambda rows, keys: pl.BlockSpec((None, tm // keys, rows, keys), lambda b, t: (b, t, 0, 0))
    const = lambda shape: pl.BlockSpec(shape, lambda b, t: (0,) * len(shape))
    out_shape = [
        jax.ShapeDtypeStruct((B, T, 256), BF16),
        jax.ShapeDtypeStruct((B, T, 128), BF16),
        jax.ShapeDtypeStruct((B, T, 128), BF16),
        jax.ShapeDtypeStruct((B, T // UPD, V_ROWS, UPD), BF16),
        jax.ShapeDtypeStruct((B, T // CHUNK, V_ROWS, CHUNK), BF16),
        jax.ShapeDtypeStruct((B, T, 128), F32),
        jax.ShapeDtypeStruct((B, nt, N_MISC_T, tm), F32),
        jax.ShapeDtypeStruct((B, T, 256), BF16),
        jax.ShapeDtypeStruct((B, T, 512), BF16),
        jax.ShapeDtypeStruct((B, T // UPD, N_HEADS * V_ROWS, UPD), BF16),
        jax.ShapeDtypeStruct((B, T, 256), F32),
        jax.ShapeDtypeStruct((B, T, 256), F32),
    ]
    out_specs = [tok(256), tok(128), tok(128), chunked(V_ROWS, UPD), chunked(V_ROWS, CHUNK), tok(128),
                 pl.BlockSpec((None, None, N_MISC_T, tm), lambda b, t: (b, t, 0, 0)),
                 tok(256), tok(512), chunked(N_HEADS * V_ROWS, UPD), tok(256), tok(256)]
    def next_tile(b, t):
        lin = jnp.minimum(b * nt + t + 1, B * nt - 1)
        return lin // nt, lin % nt, 0

    in_specs = [tok(D_MODEL), pl.BlockSpec((None, tm, D_MODEL), next_tile),
                const((1, D_MODEL)), const((D_MODEL, _P_COLS)),
                pl.BlockSpec((tm, 128), lambda b, t: (t, 0)), pl.BlockSpec((tm, 128), lambda b, t: (t, 0)),
                const((8, 1)), const((3, 256)), const((256, 256)), const((1, 256)), const((tm, tm))]
    return pl.pallas_call(
        functools.partial(_proj_kernel, tm=tm),
        grid=(B, nt),
        in_specs=in_specs,
        out_specs=out_specs,
        out_shape=out_shape,
        scratch_shapes=[pltpu.VMEM((tm + POOL_HALO, 256), F32), pltpu.VMEM((tm + CONV_HALO, 256), F32),
                        pltpu.VMEM((8, 128), F32), pltpu.VMEM((tm, D_MODEL), BF16)],
        compiler_params=pltpu.CompilerParams(dimension_semantics=("arbitrary", "arbitrary"),
                                             vmem_limit_bytes=VMEM_LIMIT),
        name="proj",
    )(x, x, g, w1, cc, ss, fb, cw, pw, ps, tri)


def _compress_kernel(x_ref, pt_ref, pb_ref, wt_ref, wb_ref, b1_ref, w2_ref, b2_ref, cc_ref, ss_ref,
                     kc_ref, vct_ref, *, rows):
    a = b = None
    for l in range(CMP_STRIDE):
        x_l = x_ref[pl.ds(l, rows, stride=CMP_STRIDE), :]
        a_l = _dot((x_l + pt_ref[l:l + 1, :]).astype(BF16), wt_ref[l])
        b_l = _dot((x_l + pb_ref[l:l + 1, :]).astype(BF16), wb_ref[l])
        a, b = (a_l, b_l) if a is None else (a + a_l, b + b_l)
    hid = a + pltpu.roll(b, rows - 1, 0) + b1_ref[...]
    act = hid * _sigmoid(hid)
    out = _dot(act.astype(BF16), w2_ref[...]) + b2_ref[...]
    lane = lax.broadcasted_iota(jnp.int32, (rows, 128), 1)
    roped = out[:, 0:128] * cc_ref[...] + out[:, 128:256] * ss_ref[...]
    block = lax.broadcasted_iota(jnp.int32, (rows, 128), 0)
    group_onehot = jnp.where(lane - HEAD_DIM == block // CMP_GROUP, 1.0, 0.0)
    kc_ref[...] = jnp.where(lane < HEAD_DIM, roped, group_onehot).astype(BF16)
    vct_ref[...] = jnp.transpose(out[:, 256:384])[0:HEAD_DIM, :].astype(BF16)


def _compress_call(kvc, pt, pb, wt, wb, b1, w2, b2, cc, ss):
    B, T, width = kvc.shape
    rows = T // CMP_STRIDE
    const = lambda shape: pl.BlockSpec(shape, lambda b: (0,) * len(shape))
    return pl.pallas_call(
        functools.partial(_compress_kernel, rows=rows),
        grid=(B,),
        in_specs=[pl.BlockSpec((None, T, width), lambda b: (b, 0, 0)),
                  const((CMP_STRIDE, width)), const((CMP_STRIDE, width)),
                  const((CMP_STRIDE, width, 256)), const((CMP_STRIDE, width, 256)),
                  const((1, 256)), const((256, 384)), const((1, 384)), const((rows, 128)), const((rows, 128))],
        out_specs=[pl.BlockSpec((None, rows, 128), lambda b: (b, 0, 0)),
                   pl.BlockSpec((None, HEAD_DIM, rows), lambda b: (b, 0, 0))],
        out_shape=[jax.ShapeDtypeStruct((B, rows, 128), BF16), jax.ShapeDtypeStruct((B, HEAD_DIM, rows), BF16)],
        compiler_params=pltpu.CompilerParams(dimension_semantics=("arbitrary",),
                                             vmem_limit_bytes=VMEM_LIMIT),
        name="compress",
    )(kvc, pt, pb, wt, wb, b1, w2, b2, cc, ss)


def _nsa_kernel(q_ref, ks_ref, kw_ref, vst_ref, vwt_ref, kc_ref, vct_ref, mt_ref, ovt_ref,
                o_ref, key_ref, sbuf, cbuf, *, nblk, ncmp):
    step = pl.program_id(1)
    tq = CHUNK
    nq = N_HEADS * tq
    hw = nq // 2
    halves = (slice(0, hw), slice(hw, nq))
    n_c = KV_TILE // UPD
    n_win = WINDOW // tq
    subs = range(NSA_SUB)
    qis = [step * NSA_SUB + u for u in subs]
    q_local = lax.broadcasted_iota(jnp.int32, (1, nq), 1) % tq
    k_local = lax.broadcasted_iota(jnp.int32, (tq, nq), 0)
    causal = k_local <= q_local

    q4, qt_plain = [], []
    group = lax.broadcasted_iota(jnp.int32, (HEAD_DIM, nq), 0)
    slab = 2 * CMP_GROUP
    n_slab = lax.broadcasted_iota(jnp.int32, (slab, nq), 0)
    for u in subs:
        q_t = jnp.transpose(q_ref[u * tq:(u + 1) * tq, :].astype(F32))
        q4.append(jnp.concatenate([q_t[HEAD_DIM * h:HEAD_DIM * (h + 1)] for h in range(N_HEADS)], axis=1))
        qt_plain.append(jnp.concatenate([q4[u], jnp.zeros((HEAD_DIM, nq), F32)], axis=0).astype(BF16))
        group_bias = jnp.where(group > qis[u], MASK_VALUE, 0.0)
        qt_cmp = jnp.concatenate([q4[u], group_bias], axis=0).astype(BF16)
        cbuf[u] = _dot(kc_ref[...], qt_cmp)
        start = pl.multiple_of(jnp.maximum(qis[u] - 1, 0) * CMP_GROUP, CMP_GROUP)
        visible = ((start + n_slab) * CMP_STRIDE + (CMP_LEN - 1)) <= qis[u] * tq + q_local
        cbuf[u, pl.ds(start, slab), :] = jnp.where(visible, cbuf[u, pl.ds(start, slab), :], MASK_VALUE)

    o_cmp, imp = [], []
    for u in subs:
        s = cbuf[u]
        mc = jnp.max(s, axis=0, keepdims=True)
        mc = jnp.where(mc <= MASK_VALUE, 0.0, mc)
        pc = jnp.exp2(s - mc)
        lc = jnp.sum(pc, axis=0, keepdims=True)
        pc = pc * (1.0 / jnp.where(lc > 0.0, lc, 1.0))
        o_cmp.append(_dot(vct_ref[...], pc.astype(BF16)))
        psum = (pc[:, 0:tq] + pc[:, tq:2 * tq]) + (pc[:, 2 * tq:3 * tq] + pc[:, 3 * tq:4 * tq])
        imp.append(_dot(ovt_ref[...], psum, precision=lax.Precision.HIGHEST))

    def win_stream(u):
        qi = qis[u]

        def score(c):
            if c == 0:
                own = kw_ref[pl.ds(pl.multiple_of(qi * tq, tq), tq), :]
                return jnp.where(causal, _dot(own, qt_plain[u]), MASK_VALUE)
            ci = qi - n_win + (c - 1)
            start = pl.multiple_of(jnp.maximum(ci, 0) * tq, tq)
            s = _dot(kw_ref[pl.ds(start, tq), :], qt_plain[u])
            ok = ((k_local > q_local) & (ci >= 0)) if c == 1 else jnp.broadcast_to(ci >= 0, (tq, nq))
            return jnp.where(ok, s, MASK_VALUE)

        def value(c):
            return vwt_ref[qi] if c == 0 else vwt_ref[jnp.maximum(qi - n_win + (c - 1), 0)]

        return score, value, _flash_init(nq)

    carries_w = _attend_chunks([win_stream(u) for u in subs], n_win + 1)
    o_win = [acc / l for _, l, acc in carries_w]

    jb = lax.broadcasted_iota(jnp.int32, (nblk, tq), 0)
    sub8 = lax.broadcasted_iota(jnp.int32, (8, tq), 0)
    later_rows = [sub8 > r for r in range(8)]
    qt_sel = []
    for u in subs:
        tt = qis[u] * tq + lax.broadcasted_iota(jnp.int32, (nblk, tq), 1)
        jt = tt // SEL_LEN
        forced = (jb == 0) | (jb == jt) | (jb == jt - 1)
        valid = jb * SEL_LEN <= tt
        key = jnp.where(valid, jnp.where(forced, imp[u] + FORCE_BONUS, imp[u]), MASK_VALUE)
        key_ref[u] = key
        key_rows = [key[8 * v:8 * v + 8] for v in range(nblk // 8)]
        ranks = [jnp.zeros((8, tq), F32) for _ in key_rows]
        for jp in range(nblk):
            other = jnp.broadcast_to(key_ref[u, pl.ds(jp, 1), :], (8, tq))
            for v, kv in enumerate(key_rows):
                if 8 * v > jp:
                    ahead = other >= kv
                elif 8 * v + 7 <= jp:
                    ahead = other > kv
                else:
                    ahead = (other > kv) | ((other == kv) & later_rows[jp - 8 * v])
                ranks[v] = ranks[v] + jnp.where(ahead, 1.0, 0.0)
        selected = jnp.concatenate(ranks, axis=0) < float(min(N_SEL, nblk))
        sel_bias = jnp.where(selected, 0.0, MASK_VALUE)
        if nblk < HEAD_DIM:
            sel_bias = jnp.concatenate([sel_bias, jnp.zeros((HEAD_DIM - nblk, tq), F32)], axis=0)
        qt_sel.append(jnp.concatenate([q4[u], jnp.concatenate([sel_bias] * N_HEADS, axis=1)], axis=0).astype(BF16))

    def slc_scores(u, kt, c):
        start = pl.multiple_of((kt * n_c + c) * UPD, UPD)
        return _dot(ks_ref[pl.ds(start, UPD), :], qt_sel[u])

    def slc_update(u, carry, s, kt, c):
        return tuple(_flash_t(s[:, hs], *carry[i], vst_ref[kt * n_c + c]) for i, hs in enumerate(halves))

    for c in range(n_c):
        for u in subs:
            sbuf[u, c] = slc_scores(u, 0, c)

    def slc_tile(kt, carries):
        carries = list(carries)
        for c in range(n_c):
            for u in subs:
                s = sbuf[u, c]
                sbuf[u, c] = slc_scores(u, kt + 1, c)
                carries[u] = slc_update(u, carries[u], s, kt, c)
        return tuple(carries)

    carries = list(lax.fori_loop(0, step, slc_tile, tuple((_flash_init(hw), _flash_init(hw)) for u in subs)))
    k_upd = lax.broadcasted_iota(jnp.int32, (UPD, nq), 0)
    for c in range(n_c):
        first, last = c * UPD // tq, (c + 1) * UPD // tq - 1
        for u in subs:
            if first > u:
                continue
            s = sbuf[u, c]
            if last >= u:
                s = jnp.where(c * UPD + k_upd <= u * tq + q_local, s, MASK_VALUE)
            carries[u] = slc_update(u, carries[u], s, step, c)

    for u in subs:
        o_slc = jnp.concatenate([acc / l for _, l, acc in carries[u]], axis=1)
        g = mt_ref[:, u * tq:(u + 1) * tq]
        heads = []
        for h in range(N_HEADS):
            cols = slice(h * tq, (h + 1) * tq)
            heads.append(g[h:h + 1] * o_cmp[u][:, cols] + g[4 + h:5 + h] * o_slc[:, cols]
                         + g[8 + h:9 + h] * o_win[u][:, cols])
        o_ref[u * tq:(u + 1) * tq, :] = jnp.transpose(jnp.concatenate(heads, axis=0))


def _nsa_call(q, ks, kw, vst, vwt, kc, vct, mt, ovt):
    B, T, _ = q.shape
    nblk, ncmp = ovt.shape
    tq = CHUNK * NSA_SUB
    per_mt = TM_PROJ // tq
    full = lambda width: pl.BlockSpec((None, T, width), lambda b, i: (b, 0, 0))
    vfull = lambda keys: pl.BlockSpec((None, T // keys, V_ROWS, keys), lambda b, i: (b, 0, 0, 0))
    return pl.pallas_call(
        functools.partial(_nsa_kernel, nblk=nblk, ncmp=ncmp),
        grid=(B, T // tq),
        in_specs=[pl.BlockSpec((None, tq, 256), lambda b, i: (b, i, 0)),
                  full(128), full(128), vfull(UPD), vfull(CHUNK),
                  pl.BlockSpec((None, ncmp, 128), lambda b, i: (b, 0, 0)),
                  pl.BlockSpec((None, HEAD_DIM, ncmp), lambda b, i: (b, 0, 0)),
                  pl.BlockSpec((None, None, N_MISC_T, tq), lambda b, i: (b, i // per_mt, 0, i % per_mt)),
                  pl.BlockSpec((nblk, ncmp), lambda b, i: (0, 0))],
        out_specs=pl.BlockSpec((None, tq, 256), lambda b, i: (b, i, 0)),
        out_shape=jax.ShapeDtypeStruct((B, T, 256), F32),
        scratch_shapes=[pltpu.VMEM((NSA_SUB, nblk, CHUNK), F32),
                        pltpu.VMEM((NSA_SUB, KV_TILE // UPD, UPD, N_HEADS * CHUNK), F32),
                        pltpu.VMEM((NSA_SUB, ncmp, N_HEADS * CHUNK), F32)],
        compiler_params=pltpu.CompilerParams(dimension_semantics=("arbitrary", "arbitrary"),
                                             vmem_limit_bytes=VMEM_LIMIT),
        name="nsa",
    )(q, ks, kw, vst, vwt, kc, vct, mt, ovt)


def _fox_kernel(q_ref, k_ref, vt_ref, o_ref, sbuf):
    qi = pl.program_id(1)
    nq = TQ_FOX
    n_chunk = KV_TILE // UPD
    t_lane = qi * nq + lax.broadcasted_iota(jnp.int32, (1, nq), 1)
    k_local = lax.broadcasted_iota(jnp.int32, (UPD, nq), 0)
    q_t = jnp.transpose(q_ref[...].astype(F32))
    extra_row = lax.broadcasted_iota(jnp.int32, (HEAD_DIM, nq), 0)

    def ones_rows(h):
        return jnp.where((extra_row >= 3 * h) & (extra_row < 3 * h + 3), 1.0, 0.0)

    qts = [jnp.concatenate([q_t[HEAD_DIM * h:HEAD_DIM * (h + 1)], ones_rows(h)], axis=0).astype(BF16)
           for h in range(N_HEADS)]

    def scores(kt, c, h):
        start = pl.multiple_of((kt * n_chunk + c) * UPD, UPD)
        return _dot(k_ref[pl.ds(start, UPD), h * 128:(h + 1) * 128], qts[h])

    def value(kt, c, h):
        return vt_ref[kt * n_chunk + c, h * V_ROWS:(h + 1) * V_ROWS, :]

    for c in range(n_chunk):
        for h in range(N_HEADS):
            sbuf[h, c] = scores(0, c, h)

    def tile(kt, carries):
        carries = list(carries)
        for c in range(n_chunk):
            for h in range(N_HEADS):
                s = sbuf[h, c]
                sbuf[h, c] = scores(kt + 1, c, h)
                carries[h] = _flash_t(s, *carries[h], value(kt, c, h))
        return tuple(carries)

    carries = list(lax.fori_loop(0, qi, tile, tuple(_flash_init(nq) for _ in range(N_HEADS))))
    for c in range(n_chunk):
        causal = qi * nq + c * UPD + k_local <= t_lane
        for h in range(N_HEADS):
            carries[h] = _flash_t(jnp.where(causal, sbuf[h, c], MASK_VALUE), *carries[h], value(qi, c, h))
    o_ref[...] = jnp.transpose(jnp.concatenate([acc / l for _, l, acc in carries], axis=0))


def _fox_call(fq, fk, fvt):
    B, T, _ = fq.shape
    tq = TQ_FOX
    return pl.pallas_call(
        _fox_kernel,
        grid=(B, T // tq),
        in_specs=[pl.BlockSpec((None, tq, 256), lambda b, i: (b, i, 0)),
                  pl.BlockSpec((None, T, 512), lambda b, i: (b, 0, 0)),
                  pl.BlockSpec((None, T // UPD, N_HEADS * V_ROWS, UPD), lambda b, i: (b, 0, 0, 0))],
        out_specs=pl.BlockSpec((None, tq, 256), lambda b, i: (b, i, 0)),
        out_shape=jax.ShapeDtypeStruct((B, T, 256), F32),
        scratch_shapes=[pltpu.VMEM((N_HEADS, KV_TILE // UPD, UPD, tq), F32)],
        compiler_params=pltpu.CompilerParams(dimension_semantics=("arbitrary", "arbitrary"),
                                             vmem_limit_bytes=VMEM_LIMIT),
        name="fox",
    )(fq, fk, fvt)


def _merge_kernel(x_ref, xn_ref, g_ref, wg_ref, wm_ref, wb_ref, wo_ref, fg_ref, o0_ref, o1_ref, o2_ref, o3_ref,
                  out_ref, hbuf, *, last, tm):
    @pl.when((pl.program_id(0) == 0) & (pl.program_id(1) == 0))
    def _():
        hbuf[...] = _rmsnorm(x_ref[...], g_ref[...]).astype(BF16)

    rows = tm // 2
    halves = [slice(u * rows, (u + 1) * rows) for u in range(2)]
    xs = [x_ref[r, :] for r in halves]
    hs = [hbuf[r, :] for r in halves]
    accs = []
    for u, r in enumerate(halves):
        gates = _dot(hs[u], wg_ref[...])
        acc = None
        for i, o_ref in enumerate((o0_ref, o1_ref, o2_ref, o3_ref)):
            merge = _dot(hs[u], wm_ref[:, i * D_MODEL:(i + 1) * D_MODEL])
            gate = gates[:, i * BRANCH_WIDTH:(i + 1) * BRANCH_WIDTH]
            a = (o_ref[r, :] * (gate * _sigmoid(gate))).astype(BF16)
            term = _sigmoid(merge) * _dot(a, wb_ref[i])
            acc = term if acc is None else acc + term
        accs.append(acc)
    for u, r in enumerate(halves):
        y = xs[u] + _dot(accs[u].astype(BF16), wo_ref[...])
        if last:
            y = _rmsnorm(y, fg_ref[...])
        out_ref[r, :] = y
    hbuf[...] = _rmsnorm(xn_ref[...], g_ref[...]).astype(BF16)


def _merge_call(x, g, wg, wm, wb, wo, fg, o_nsa, o_pool, o_conv, o_fox, last):
    B, T, _ = x.shape
    tm = TM_MERGE
    nt = T // tm
    tok = lambda width: pl.BlockSpec((None, tm, width), lambda b, t: (b, t, 0))
    const = lambda shape: pl.BlockSpec(shape, lambda b, t: (0,) * len(shape))

    def next_tile(b, t):
        lin = jnp.minimum(b * nt + t + 1, B * nt - 1)
        return lin // nt, lin % nt, 0

    return pl.pallas_call(
        functools.partial(_merge_kernel, last=last, tm=tm),
        grid=(B, nt),
        in_specs=[tok(D_MODEL), pl.BlockSpec((None, tm, D_MODEL), next_tile),
                  const((1, D_MODEL)), const((D_MODEL, N_BRANCH * BRANCH_WIDTH)),
                  const((D_MODEL, N_BRANCH * D_MODEL)), const((N_BRANCH, BRANCH_WIDTH, D_MODEL)),
                  const((D_MODEL, D_MODEL)), const((1, D_MODEL)), tok(256), tok(256), tok(256), tok(256)],
        out_specs=tok(D_MODEL),
        out_shape=jax.ShapeDtypeStruct((B, T, D_MODEL), F32),
        scratch_shapes=[pltpu.VMEM((tm, D_MODEL), BF16)],
        compiler_params=pltpu.CompilerParams(dimension_semantics=("arbitrary", "arbitrary"),
                                             vmem_limit_bytes=VMEM_LIMIT),
        name="merge",
    )(x, x, g, wg, wm, wb, wo, fg, o_nsa, o_pool, o_conv, o_fox)


def _rope_tables(pos):
    n = pos.shape[0]
    inv = ROPE_THETA ** (-jnp.arange(ROPE_HALF, dtype=F32) / ROPE_HALF)
    ang = pos.astype(F32)[:, None] * inv[None, :]
    cos, sin = jnp.cos(ang), jnp.sin(ang)
    rest = HEAD_DIM - 2 * ROPE_HALF
    c = jnp.concatenate([cos, cos, jnp.ones((n, rest), F32)], axis=1)
    s = jnp.concatenate([-sin, sin, jnp.zeros((n, rest), F32)], axis=1)
    return jnp.concatenate([c, c], axis=1), jnp.concatenate([s, s], axis=1)


def _proj_weight(w_in):
    col = lambda name, width: w_in[:, _OFF[name]:_OFF[name] + width]
    z64 = jnp.zeros((D_MODEL, HEAD_DIM), w_in.dtype)
    misc = jnp.concatenate([col("nsa_g", 12), col("fox_f", 4), jnp.zeros((D_MODEL, 112), w_in.dtype)], axis=1)
    fox_k = []
    for h in range(N_HEADS):
        fox_k += [w_in[:, _OFF["fox_k"] + h * HEAD_DIM:_OFF["fox_k"] + (h + 1) * HEAD_DIM], z64]
    w = jnp.concatenate([col("nsa_q", 256), col("k_s", 64), z64, col("k_w", 64), z64,
                         col("v_s", 64), col("v_w", 64), col("k_c", 128), misc,
                         col("fox_q", 256)] + fox_k + [col("fox_v", 256), col("pool", 256), col("conv", 768)],
                        axis=1)
    assert w.shape[1] == _P_COLS and w.dtype == BF16
    return w


def _compress_weights(cmp_pos, cmp_w1, cmp_b1, cmp_w2, cmp_b2):
    half = CMP_LEN // 2

    def w1_half(lo):
        wk = cmp_w1[0, lo * HEAD_DIM:(lo + half) * HEAD_DIM].reshape(half, HEAD_DIM, CMP_HIDDEN)
        wv = cmp_w1[1, lo * HEAD_DIM:(lo + half) * HEAD_DIM].reshape(half, HEAD_DIM, CMP_HIDDEN)
        z = jnp.zeros_like(wk)
        top = jnp.concatenate([wk, z], axis=2)
        bot = jnp.concatenate([z, wv], axis=2)
        return jnp.concatenate([top, bot], axis=1).astype(BF16)

    def pos_half(lo):
        return jnp.concatenate([cmp_pos[0, lo:lo + half], cmp_pos[1, lo:lo + half]], axis=1)

    lane = np.arange(HEAD_DIM)
    perm = np.where(lane < ROPE_HALF, lane + ROPE_HALF, np.where(lane < 2 * ROPE_HALF, lane - ROPE_HALF, lane))
    w2k, w2v = cmp_w2[0], cmp_w2[1]
    zk = jnp.zeros_like(w2k)
    w2 = jnp.concatenate([
        jnp.concatenate([w2k, w2k, w2k[:, perm], w2k[:, perm], zk, zk], axis=1),
        jnp.concatenate([zk, zk, zk, zk, w2v, w2v], axis=1)], axis=0).astype(BF16)
    b2k, b2v = cmp_b2[0], cmp_b2[1]
    b2 = jnp.concatenate([b2k, b2k, b2k[perm], b2k[perm], b2v, b2v])[None, :]
    b1 = jnp.concatenate([cmp_b1[0], cmp_b1[1]])[None, :]
    return pos_half(0), pos_half(half), w1_half(0), w1_half(half), b1, w2, b2


def _block_diag(pool_w):
    n, c, _ = pool_w.shape
    out = jnp.zeros((n * c, n * c), pool_w.dtype)
    for i in range(n):
        out = out.at[i * c:(i + 1) * c, i * c:(i + 1) * c].set(pool_w[i])
    return out


def kernel(x, norm_g, w_in, fox_f_bias, cmp_pos, cmp_w1, cmp_b1, cmp_w2, cmp_b2, pool_w, pool_scale, conv_w,
           w_branch, w_out, final_norm_g):
    B, T, _ = x.shape
    depth = norm_g.shape[0]
    assert T % TM_PROJ == 0 and T % KV_TILE == 0 and T // SEL_LEN <= HEAD_DIM and T >= WINDOW + CHUNK
    assert NSA_SUB * CHUNK == KV_TILE == TQ_FOX and TM_PROJ % KV_TILE == 0
    rows = T // CMP_STRIDE
    nblk = T // SEL_LEN

    cc_tok, ss_tok = _rope_tables(jnp.arange(T))
    cmp_end = jnp.arange(rows) * CMP_STRIDE + CMP_LEN - 1
    cc_cmp, ss_cmp = _rope_tables(cmp_end)
    ci = np.arange(rows)[:, None] * CMP_STRIDE
    sj = np.arange(nblk)[None, :] * SEL_LEN
    overlap_t = jnp.asarray(((ci < sj + SEL_LEN) & (ci + CMP_LEN > sj)).T, F32)
    tri = jnp.asarray(np.triu(np.ones((TM_PROJ, TM_PROJ), np.float32)), BF16)

    w_in_bf = w_in.astype(BF16)
    for l in range(depth):
        w1 = _proj_weight(w_in_bf[l])
        fb = jnp.zeros((8, 1), F32).at[4:8, 0].set(fox_f_bias[l])
        (q, ks, kw, vst, vwt, kvc, mt, fq, fk, fvt, o_pool, o_conv) = _proj_call(
            x, norm_g[l][None, :], w1, cc_tok, ss_tok, fb, conv_w[l], _block_diag(pool_w[l]).astype(BF16),
            pool_scale[l][None, :], tri)
        kc, vct = _compress_call(kvc, *_compress_weights(cmp_pos[l], cmp_w1[l], cmp_b1[l], cmp_w2[l], cmp_b2[l]),
                                 cc_cmp, ss_cmp)
        o_nsa = _nsa_call(q, ks, kw, vst, vwt, kc, vct, mt, overlap_t)
        o_fox = _fox_call(fq, fk, fvt)
        wg = w_in_bf[l][:, _OFF["gate"]:_OFF["merge"]]
        wm = w_in_bf[l][:, _OFF["merge"]:_OFF["end"]]
        x = _merge_call(x, norm_g[l][None, :], wg, wm, w_branch[l].astype(BF16), w_out[l].astype(BF16),
                        final_norm_g[None, :], o_nsa, o_pool, o_conv, o_fox, last=(l == depth - 1))
    return x
```

```python
import functools

import numpy as np
import jax
import jax.numpy as jnp
from jax import lax
from jax.experimental import pallas as pl
from jax.experimental.pallas import tpu as pltpu

F32 = jnp.float32
BF16 = jnp.bfloat16

D_MODEL = 1024
N_BRANCH = 4
BRANCH_WIDTH = 256
HEAD_DIM = 64
N_HEADS = 4
CMP_LEN = 32
CMP_STRIDE = 16
CMP_HIDDEN = 128
CMP_GROUP = 8
SEL_LEN = 64
N_SEL = 16
WINDOW = 512
FORCE_BONUS = 1.0e4
ROPE_THETA = 500000.0
ROPE_HALF = 8
NORM_EPS = 1e-6
MASK_VALUE = -1e30

_OFF = dict(nsa_q=0, k_c=256, v_c=320, k_s=384, v_s=448, k_w=512, v_w=576, nsa_g=640, pool=652,
            conv=908, fox_q=1676, fox_k=1932, fox_v=2188, fox_f=2444, gate=2448, merge=3472, end=7568)

_P_ROPE = 0
_P_VSW = 512
_P_FOXQ = 896
_P_FOXK = 1152
_P_FOXV = 1664
_P_POOL = 1920
_P_CONV = 2176
_P_COLS = 2944

CHUNK = 128
UPD = 256
V_ROWS = HEAD_DIM + 16
LOG2E = 1.4426950408889634
Q_SCALE = HEAD_DIM ** -0.5 * LOG2E
TM_PROJ = 512
KV_TILE = 512
NSA_SUB = 4
TQ_FOX = 512
TM_MERGE = 512
N_MISC_T = 16
POOL_HALO = 16
CONV_HALO = 8
VMEM_LIMIT = 56 * 1024 * 1024


def _dot(a, b, precision=None):
    return jnp.dot(a, b, preferred_element_type=F32, precision=precision)


def _sigmoid(x):
    return 1.0 / (1.0 + jnp.exp(-x))


def _rmsnorm(x, g):
    return x * lax.rsqrt(jnp.mean(x * x, axis=-1, keepdims=True) + NORM_EPS) * g


def _flash_t(s, m, l, acc, v_t):
    m_new = jnp.maximum(m, jnp.max(s, axis=0, keepdims=True))
    alpha = jnp.exp2(m - m_new)
    pv = _dot(v_t, jnp.exp2(s - m_new).astype(BF16))
    l = alpha * l + pv[HEAD_DIM:HEAD_DIM + 1]
    acc = alpha * acc + pv[0:HEAD_DIM]
    return m_new, l, acc


def _attend_chunks(streams, n):
    nxt = [st[0](0) for st in streams]
    carries = [st[2] for st in streams]
    for c in range(n):
        cur = nxt
        if c + 1 < n:
            nxt = [st[0](c + 1) for st in streams]
        carries = [_flash_t(cur[i], *carries[i], streams[i][1](c)) for i in range(len(streams))]
    return carries


def _flash_init(n):
    return (jnp.full((1, n), MASK_VALUE, F32), jnp.zeros((1, n), F32), jnp.zeros((HEAD_DIM, n), F32))


def _proj_kernel(x_ref, xn_ref, g_ref, w_ref, cc_ref, ss_ref, fb_ref, cw_ref, pw_ref, ps_ref, tri_ref,
                 q_ref, ks_ref, kw_ref, vst_ref, vwt_ref, kvc_ref, mt_ref, fq_ref, fk_ref, fvt_ref,
                 opool_ref, oconv_ref, pext, cext, ccarry, hbuf, *, tm):
    ti = pl.program_id(1)

    @pl.when((pl.program_id(0) == 0) & (ti == 0))
    def _():
        hbuf[...] = _rmsnorm(x_ref[...], g_ref[...]).astype(BF16)

    h = hbuf[...]

    def seg(lo, width):
        return _dot(h, w_ref[:, lo:lo + width])

    lane = lax.broadcasted_iota(jnp.int32, (tm, 128), 1)
    row = lax.broadcasted_iota(jnp.int32, (tm, 128), 0) + ti * tm
    first_half = (lane % HEAD_DIM) < ROPE_HALF
    cc = cc_ref[...]
    ss = ss_ref[...]

    def rope(xc):
        partner = jnp.where(first_half, pltpu.roll(xc, 128 - ROPE_HALF, 1), pltpu.roll(xc, ROPE_HALF, 1))
        return xc * cc + partner * ss

    @pl.when(ti == 0)
    def _():
        ccarry[...] = jnp.zeros_like(ccarry)
        pext[0:POOL_HALO, :] = jnp.zeros((POOL_HALO, BRANCH_WIDTH), F32)
        cext[0:CONV_HALO, :] = jnp.zeros((CONV_HALO, BRANCH_WIDTH), F32)

    g_small = seg(_P_VSW, 384)
    g_rope = seg(_P_ROPE, 512)

    z_t = jnp.transpose(g_small[:, 256:384])[0:N_MISC_T, :]
    gates = _sigmoid(z_t)
    row8 = lax.broadcasted_iota(jnp.int32, (8, tm), 0)
    zb = z_t[8:16] + fb_ref[...]
    logf = jnp.where(row8 >= 4, jnp.minimum(zb, 0.0) - jnp.log(1.0 + jnp.exp(-jnp.abs(zb))), 0.0)

    def split3(v):
        hi = v.astype(BF16).astype(F32)
        mid = (v - hi).astype(BF16).astype(F32)
        return hi, mid, (v - hi) - mid

    parts = jnp.concatenate(list(split3(logf)) + [jnp.zeros((8, tm), F32)], axis=0).astype(BF16)
    part_sums = _dot(parts, tri_ref[...])
    csum = (part_sums[0:8] + part_sums[8:16]) + part_sums[16:24] + ccarry[:, 0:1]
    ccarry[...] = jnp.broadcast_to(csum[:, tm - 1:tm], (8, 128))
    mt_ref[0:8, :] = gates[0:8]
    mt_ref[8:16, :] = jnp.where(row8 < 4, gates[8:16], csum)

    g_pool = seg(_P_POOL, 256)
    cv = seg(_P_CONV, 768)

    pext[POOL_HALO:POOL_HALO + tm, :] = g_pool

    def pld(k, c):
        return pext[pl.ds(POOL_HALO - k, tm), c * 128:(c + 1) * 128]

    left = lane < 64
    rowp1 = (row + 1).astype(F32)

    def cnt(w):
        return jnp.minimum(rowp1, float(w))

    e0 = pld(0, 0)
    s2 = e0 + pld(1, 0)
    s4 = s2 + pld(2, 0) + pld(3, 0)
    p0 = jnp.where(left, s2 / cnt(2), s4 / cnt(4)) - e0
    f0 = pld(0, 1)
    s8 = f0
    for k in range(1, 8):
        s8 = s8 + pld(k, 1)
    s16 = s8
    for k in range(8, 16):
        s16 = s16 + pld(k, 1)
    p1 = jnp.where(left, s8 / cnt(8), s16 / cnt(16)) - f0
    pw = pw_ref[...]
    mixed = _dot(p0.astype(BF16), pw[0:128, :]) + _dot(p1.astype(BF16), pw[128:256, :])
    opool_ref[...] = mixed * ps_ref[...]
    pext[0:POOL_HALO, :] = pext[tm:tm + POOL_HALO, :]

    g_fv = seg(_P_FOXV, 256)
    g_fk = seg(_P_FOXK, 512)
    g_fq = seg(_P_FOXQ, 256)

    q_ref[:, 0:128] = (rope(g_rope[:, 0:128]) * Q_SCALE).astype(BF16)
    q_ref[:, 128:256] = (rope(g_rope[:, 128:256]) * Q_SCALE).astype(BF16)
    block_onehot = jnp.where(lane - HEAD_DIM == row // SEL_LEN, 1.0, 0.0)
    ks_ref[...] = (rope(g_rope[:, 256:384]) + block_onehot).astype(BF16)
    kw_ref[...] = rope(g_rope[:, 384:512]).astype(BF16)
    vsw_t = jnp.transpose(g_small[:, 0:128]).astype(BF16)
    ones_upd = jnp.ones((V_ROWS - HEAD_DIM, UPD), BF16)
    for c in range(tm // UPD):
        vst_ref[c] = jnp.concatenate([vsw_t[0:HEAD_DIM, c * UPD:(c + 1) * UPD], ones_upd], axis=0)
    for c in range(tm // CHUNK):
        vwt_ref[c] = jnp.concatenate([vsw_t[HEAD_DIM:128, c * CHUNK:(c + 1) * CHUNK], ones_upd[:, 0:CHUNK]], axis=0)
    kvc_ref[...] = g_small[:, 128:256]

    fq_ref[...] = (g_fq * Q_SCALE).astype(BF16)
    terms = split3(csum * (-LOG2E))
    row16 = lax.broadcasted_iota(jnp.int32, (16, tm), 0)
    extra_t = jnp.zeros((16, tm), F32)
    for hd in range(N_HEADS):
        for j in range(3):
            src = jnp.broadcast_to(terms[j][4 + hd:5 + hd, :], (16, tm))
            extra_t = jnp.where(row16 == 3 * hd + j, src, extra_t)
    extra = jnp.transpose(jnp.concatenate(
        [jnp.zeros((HEAD_DIM, tm), F32), extra_t, jnp.zeros((128 - HEAD_DIM - 16, tm), F32)], axis=0))
    for hd in range(N_HEADS):
        fk_ref[:, hd * 128:(hd + 1) * 128] = (g_fk[:, hd * 128:(hd + 1) * 128] + extra).astype(BF16)
    fv_t = jnp.transpose(g_fv).astype(BF16)
    for c in range(tm // UPD):
        cols = slice(c * UPD, (c + 1) * UPD)
        fvt_ref[c] = jnp.concatenate(
            [piece for hd in range(N_HEADS) for piece in (fv_t[hd * HEAD_DIM:(hd + 1) * HEAD_DIM, cols], ones_upd)],
            axis=0)

    u = cv[:, 512:768] * cv[:, 0:256]
    cext[CONV_HALO:CONV_HALO + tm, :] = u
    y = (cext[pl.ds(CONV_HALO - 2, tm), :] * cw_ref[0:1, :]
         + cext[pl.ds(CONV_HALO - 1, tm), :] * cw_ref[1:2, :]
         + u * cw_ref[2:3, :])
    oconv_ref[...] = cv[:, 256:512] * y
    cext[0:CONV_HALO, :] = cext[tm:tm + CONV_HALO, :]

    hbuf[...] = _rmsnorm(xn_ref[...], g_ref[...]).astype(BF16)


def _proj_call(x, g, w1, cc, ss, fb, cw, pw, ps, tri):
    B, T, _ = x.shape
    tm = TM_PROJ
    nt = T // tm
    tok = lambda width: pl.BlockSpec((None, tm, width), lambda b, t: (b, t, 0))
    chunked = lambda rows, keys: pl.BlockSpec((None, tm // keys, rows, keys), lambda b, t: (b, t, 0, 0))
    const = lambda shape: pl.BlockSpec(shape, lambda b, t: (0,) * len(shape))
    out_shape = [
        jax.ShapeDtypeStruct((B, T, 256), BF16),
        jax.ShapeDtypeStruct((B, T, 128), BF16),
        jax.ShapeDtypeStruct((B, T, 128), BF16),
        jax.ShapeDtypeStruct((B, T // UPD, V_ROWS, UPD), BF16),
        jax.ShapeDtypeStruct((B, T // CHUNK, V_ROWS, CHUNK), BF16),
        jax.ShapeDtypeStruct((B, T, 128), F32),
        jax.ShapeDtypeStruct((B, nt, N_MISC_T, tm), F32),
        jax.ShapeDtypeStruct((B, T, 256), BF16),
        jax.ShapeDtypeStruct((B, T, 512), BF16),
        jax.ShapeDtypeStruct((B, T // UPD, N_HEADS * V_ROWS, UPD), BF16),
        jax.ShapeDtypeStruct((B, T, 256), F32),
        jax.ShapeDtypeStruct((B, T, 256), F32),
    ]
    out_specs = [tok(256), tok(128), tok(128), chunked(V_ROWS, UPD), chunked(V_ROWS, CHUNK), tok(128),
                 pl.BlockSpec((None, None, N_MISC_T, tm), lambda b, t: (b, t, 0, 0)),
                 tok(256), tok(512), chunked(N_HEADS * V_ROWS, UPD), tok(256), tok(256)]
    def next_tile(b, t):
        lin = jnp.minimum(b * nt + t + 1, B * nt - 1)
        return lin // nt, lin % nt, 0

    in_specs = [tok(D_MODEL), pl.BlockSpec((None, tm, D_MODEL), next_tile),
                const((1, D_MODEL)), const((D_MODEL, _P_COLS)),
                pl.BlockSpec((tm, 128), lambda b, t: (t, 0)), pl.BlockSpec((tm, 128), lambda b, t: (t, 0)),
                const((8, 1)), const((3, 256)), const((256, 256)), const((1, 256)), const((tm, tm))]
    return pl.pallas_call(
        functools.partial(_proj_kernel, tm=tm),
        grid=(B, nt),
        in_specs=in_specs,
        out_specs=out_specs,
        out_shape=out_shape,
        scratch_shapes=[pltpu.VMEM((tm + POOL_HALO, 256), F32), pltpu.VMEM((tm + CONV_HALO, 256), F32),
                        pltpu.VMEM((8, 128), F32), pltpu.VMEM((tm, D_MODEL), BF16)],
        compiler_params=pltpu.CompilerParams(dimension_semantics=("arbitrary", "arbitrary"),
                                             vmem_limit_bytes=VMEM_LIMIT),
        name="proj",
    )(x, x, g, w1, cc, ss, fb, cw, pw, ps, tri)


def _compress_kernel(x_ref, pt_ref, pb_ref, wt_ref, wb_ref, b1_ref, w2_ref, b2_ref, cc_ref, ss_ref,
                     kc_ref, vct_ref, *, rows):
    a = b = None
    for l in range(CMP_STRIDE):
        x_l = x_ref[pl.ds(l, rows, stride=CMP_STRIDE), :]
        a_l = _dot((x_l + pt_ref[l:l + 1, :]).astype(BF16), wt_ref[l])
        b_l = _dot((x_l + pb_ref[l:l + 1, :]).astype(BF16), wb_ref[l])
        a, b = (a_l, b_l) if a is None else (a + a_l, b + b_l)
    hid = a + pltpu.roll(b, rows - 1, 0) + b1_ref[...]
    act = hid * _sigmoid(hid)
    out = _dot(act.astype(BF16), w2_ref[...]) + b2_ref[...]
    lane = lax.broadcasted_iota(jnp.int32, (rows, 128), 1)
    roped = out[:, 0:128] * cc_ref[...] + out[:, 128:256] * ss_ref[...]
    block = lax.broadcasted_iota(jnp.int32, (rows, 128), 0)
    group_onehot = jnp.where(lane - HEAD_DIM == block // CMP_GROUP, 1.0, 0.0)
    kc_ref[...] = jnp.where(lane < HEAD_DIM, roped, group_onehot).astype(BF16)
    vct_ref[...] = jnp.transpose(out[:, 256:384])[0:HEAD_DIM, :].astype(BF16)


def _compress_call(kvc, pt, pb, wt, wb, b1, w2, b2, cc, ss):
    B, T, width = kvc.shape
    rows = T // CMP_STRIDE
    const = lambda shape: pl.BlockSpec(shape, lambda b: (0,) * len(shape))
    return pl.pallas_call(
        functools.partial(_compress_kernel, rows=rows),
        grid=(B,),
        in_specs=[pl.BlockSpec((None, T, width), lambda b: (b, 0, 0)),
                  const((CMP_STRIDE, width)), const((CMP_STRIDE, width)),
                  const((CMP_STRIDE, width, 256)), const((CMP_STRIDE, width, 256)),
                  const((1, 256)), const((256, 384)), const((1, 384)), const((rows, 128)), const((rows, 128))],
        out_specs=[pl.BlockSpec((None, rows, 128), lambda b: (b, 0, 0)),
                   pl.BlockSpec((None, HEAD_DIM, rows), lambda b: (b, 0, 0))],
        out_shape=[jax.ShapeDtypeStruct((B, rows, 128), BF16), jax.ShapeDtypeStruct((B, HEAD_DIM, rows), BF16)],
        compiler_params=pltpu.CompilerParams(dimension_semantics=("arbitrary",),
                                             vmem_limit_bytes=VMEM_LIMIT),
        name="compress",
    )(kvc, pt, pb, wt, wb, b1, w2, b2, cc, ss)


def _nsa_kernel(q_ref, ks_ref, kw_ref, vst_ref, vwt_ref, kc_ref, vct_ref, mt_ref, ovt_ref,
                o_ref, key_ref, sbuf, cbuf, *, nblk, ncmp):
    step = pl.program_id(1)
    tq = CHUNK
    nq = N_HEADS * tq
    hw = nq // 2
    halves = (slice(0, hw), slice(hw, nq))
    n_c = KV_TILE // UPD
    n_win = WINDOW // tq
    subs = range(NSA_SUB)
    qis = [step * NSA_SUB + u for u in subs]
    q_local = lax.broadcasted_iota(jnp.int32, (1, nq), 1) % tq
    k_local = lax.broadcasted_iota(jnp.int32, (tq, hw), 0)
    q_loc = q_local[:, 0:hw]
    causal = k_local <= q_loc

    q4, qt_plain = [], []
    group = lax.broadcasted_iota(jnp.int32, (HEAD_DIM, nq), 0)
    slab = 2 * CMP_GROUP
    n_slab = lax.broadcasted_iota(jnp.int32, (slab, nq), 0)
    for u in subs:
        q_t = jnp.transpose(q_ref[u * tq:(u + 1) * tq, :].astype(F32))
        q4.append(jnp.concatenate([q_t[HEAD_DIM * h:HEAD_DIM * (h + 1)] for h in range(N_HEADS)], axis=1))
        qt_plain.append(jnp.concatenate([q4[u], jnp.zeros((HEAD_DIM, nq), F32)], axis=0).astype(BF16))
        group_bias = jnp.where(group > qis[u], MASK_VALUE, 0.0)
        qt_cmp = jnp.concatenate([q4[u], group_bias], axis=0).astype(BF16)
        cbuf[u] = _dot(kc_ref[...], qt_cmp)
        start = pl.multiple_of(jnp.maximum(qis[u] - 1, 0) * CMP_GROUP, CMP_GROUP)
        visible = ((start + n_slab) * CMP_STRIDE + (CMP_LEN - 1)) <= qis[u] * tq + q_local
        cbuf[u, pl.ds(start, slab), :] = jnp.where(visible, cbuf[u, pl.ds(start, slab), :], MASK_VALUE)

    o_cmp, imp = [], []
    for u in subs:
        s = cbuf[u]
        mc = jnp.max(s, axis=0, keepdims=True)
        mc = jnp.where(mc <= MASK_VALUE, 0.0, mc)
        pc = jnp.exp2(s - mc)
        lc = jnp.sum(pc, axis=0, keepdims=True)
        pc = pc * (1.0 / jnp.where(lc > 0.0, lc, 1.0))
        o_cmp.append(_dot(vct_ref[...], pc.astype(BF16)))
        psum = (pc[:, 0:tq] + pc[:, tq:2 * tq]) + (pc[:, 2 * tq:3 * tq] + pc[:, 3 * tq:4 * tq])
        imp.append(_dot(ovt_ref[...], psum, precision=lax.Precision.HIGHEST))

    def win_stream(u, hs):
        qi = qis[u]

        def score(c):
            if c == 0:
                own = kw_ref[pl.ds(pl.multiple_of(qi * tq, tq), tq), :]
                return jnp.where(causal, _dot(own, qt_plain[u][:, hs]), MASK_VALUE)
            ci = qi - n_win + (c - 1)
            start = pl.multiple_of(jnp.maximum(ci, 0) * tq, tq)
            s = _dot(kw_ref[pl.ds(start, tq), :], qt_plain[u][:, hs])
            ok = ((k_local > q_loc) & (ci >= 0)) if c == 1 else jnp.broadcast_to(ci >= 0, (tq, hw))
            return jnp.where(ok, s, MASK_VALUE)

        def value(c):
            return vwt_ref[qi] if c == 0 else vwt_ref[jnp.maximum(qi - n_win + (c - 1), 0)]

        return score, value, _flash_init(hw)

    carries_w = _attend_chunks([win_stream(u, hs) for u in subs for hs in halves], n_win + 1)
    o_win = [jnp.concatenate([acc / l for _, l, acc in carries_w[2 * u:2 * u + 2]], axis=1) for u in subs]

    jb = lax.broadcasted_iota(jnp.int32, (nblk, tq), 0)
    sub8 = lax.broadcasted_iota(jnp.int32, (8, tq), 0)
    later_rows = [sub8 > r for r in range(8)]
    qt_sel = []
    for u in subs:
        tt = qis[u] * tq + lax.broadcasted_iota(jnp.int32, (nblk, tq), 1)
        jt = tt // SEL_LEN
        forced = (jb == 0) | (jb == jt) | (jb == jt - 1)
        valid = jb * SEL_LEN <= tt
        key = jnp.where(valid, jnp.where(forced, imp[u] + FORCE_BONUS, imp[u]), MASK_VALUE)
        key_ref[u] = key
        key_rows = [key[8 * v:8 * v + 8] for v in range(nblk // 8)]
        ranks = [jnp.zeros((8, tq), F32) for _ in key_rows]
        for jp in range(nblk):
            other = jnp.broadcast_to(key_ref[u, pl.ds(jp, 1), :], (8, tq))
            for v, kv in enumerate(key_rows):
                if 8 * v > jp:
                    ahead = other >= kv
                elif 8 * v + 7 <= jp:
                    ahead = other > kv
                else:
                    ahead = (other > kv) | ((other == kv) & later_rows[jp - 8 * v])
                ranks[v] = ranks[v] + jnp.where(ahead, 1.0, 0.0)
        selected = jnp.concatenate(ranks, axis=0) < float(min(N_SEL, nblk))
        sel_bias = jnp.where(selected, 0.0, MASK_VALUE)
        if nblk < HEAD_DIM:
            sel_bias = jnp.concatenate([sel_bias, jnp.zeros((HEAD_DIM - nblk, tq), F32)], axis=0)
        qt_sel.append(jnp.concatenate([q4[u], jnp.concatenate([sel_bias] * N_HEADS, axis=1)], axis=0).astype(BF16))

    def slc_scores(u, kt, c):
        start = pl.multiple_of((kt * n_c + c) * UPD, UPD)
        return _dot(ks_ref[pl.ds(start, UPD), :], qt_sel[u])

    def slc_update(u, carry, s, kt, c):
        return tuple(_flash_t(s[:, hs], *carry[i], vst_ref[kt * n_c + c]) for i, hs in enumerate(halves))

    for c in range(n_c):
        for u in subs:
            sbuf[u, c] = slc_scores(u, 0, c)

    def slc_tile(kt, carries):
        carries = list(carries)
        for c in range(n_c):
            for u in subs:
                s = sbuf[u, c]
                sbuf[u, c] = slc_scores(u, kt + 1, c)
                carries[u] = slc_update(u, carries[u], s, kt, c)
        return tuple(carries)

    carries = list(lax.fori_loop(0, step, slc_tile, tuple((_flash_init(hw), _flash_init(hw)) for u in subs)))
    k_upd = lax.broadcasted_iota(jnp.int32, (UPD, nq), 0)
    for c in range(n_c):
        first, last = c * UPD // tq, (c + 1) * UPD // tq - 1
        for u in subs:
            if first > u:
                continue
            s = sbuf[u, c]
            if last >= u:
                s = jnp.where(c * UPD + k_upd <= u * tq + q_local, s, MASK_VALUE)
            carries[u] = slc_update(u, carries[u], s, step, c)

    for u in subs:
        o_slc = jnp.concatenate([acc / l for _, l, acc in carries[u]], axis=1)
        g = mt_ref[:, u * tq:(u + 1) * tq]
        heads = []
        for h in range(N_HEADS):
            cols = slice(h * tq, (h + 1) * tq)
            heads.append(g[h:h + 1] * o_cmp[u][:, cols] + g[4 + h:5 + h] * o_slc[:, cols]
                         + g[8 + h:9 + h] * o_win[u][:, cols])
        o_ref[u * tq:(u + 1) * tq, :] = jnp.transpose(jnp.concatenate(heads, axis=0))


def _nsa_call(q, ks, kw, vst, vwt, kc, vct, mt, ovt):
    B, T, _ = q.shape
    nblk, ncmp = ovt.shape
    tq = CHUNK * NSA_SUB
    per_mt = TM_PROJ // tq
    full = lambda width: pl.BlockSpec((None, T, width), lambda b, i: (b, 0, 0))
    vfull = lambda keys: pl.BlockSpec((None, T // keys, V_ROWS, keys), lambda b, i: (b, 0, 0, 0))
    return pl.pallas_call(
        functools.partial(_nsa_kernel, nblk=nblk, ncmp=ncmp),
        grid=(B, T // tq),
        in_specs=[pl.BlockSpec((None, tq, 256), lambda b, i: (b, i, 0)),
                  full(128), full(128), vfull(UPD), vfull(CHUNK),
                  pl.BlockSpec((None, ncmp, 128), lambda b, i: (b, 0, 0)),
                  pl.BlockSpec((None, HEAD_DIM, ncmp), lambda b, i: (b, 0, 0)),
                  pl.BlockSpec((None, None, N_MISC_T, tq), lambda b, i: (b, i // per_mt, 0, i % per_mt)),
                  pl.BlockSpec((nblk, ncmp), lambda b, i: (0, 0))],
        out_specs=pl.BlockSpec((None, tq, 256), lambda b, i: (b, i, 0)),
        out_shape=jax.ShapeDtypeStruct((B, T, 256), F32),
        scratch_shapes=[pltpu.VMEM((NSA_SUB, nblk, CHUNK), F32),
                        pltpu.VMEM((NSA_SUB, KV_TILE // UPD, UPD, N_HEADS * CHUNK), F32),
                        pltpu.VMEM((NSA_SUB, ncmp, N_HEADS * CHUNK), F32)],
        compiler_params=pltpu.CompilerParams(dimension_semantics=("arbitrary", "arbitrary"),
                                             vmem_limit_bytes=VMEM_LIMIT),
        name="nsa",
    )(q, ks, kw, vst, vwt, kc, vct, mt, ovt)


def _fox_kernel(q_ref, k_ref, vt_ref, o_ref, sbuf):
    qi = pl.program_id(1)
    nq = TQ_FOX
    n_chunk = KV_TILE // UPD
    t_lane = qi * nq + lax.broadcasted_iota(jnp.int32, (1, nq), 1)
    k_local = lax.broadcasted_iota(jnp.int32, (UPD, nq), 0)
    q_t = jnp.transpose(q_ref[...].astype(F32))
    extra_row = lax.broadcasted_iota(jnp.int32, (HEAD_DIM, nq), 0)

    def ones_rows(h):
        return jnp.where((extra_row >= 3 * h) & (extra_row < 3 * h + 3), 1.0, 0.0)

    qts = [jnp.concatenate([q_t[HEAD_DIM * h:HEAD_DIM * (h + 1)], ones_rows(h)], axis=0).astype(BF16)
           for h in range(N_HEADS)]

    def scores(kt, c, h):
        start = pl.multiple_of((kt * n_chunk + c) * UPD, UPD)
        return _dot(k_ref[pl.ds(start, UPD), h * 128:(h + 1) * 128], qts[h])

    def value(kt, c, h):
        return vt_ref[kt * n_chunk + c, h * V_ROWS:(h + 1) * V_ROWS, :]

    for c in range(n_chunk):
        for h in range(N_HEADS):
            sbuf[h, c] = scores(0, c, h)

    def tile(kt, carries):
        carries = list(carries)
        for c in range(n_chunk):
            for h in range(N_HEADS):
                s = sbuf[h, c]
                sbuf[h, c] = scores(kt + 1, c, h)
                carries[h] = _flash_t(s, *carries[h], value(kt, c, h))
        return tuple(carries)

    carries = list(lax.fori_loop(0, qi, tile, tuple(_flash_init(nq) for _ in range(N_HEADS))))
    for c in range(n_chunk):
        causal = qi * nq + c * UPD + k_local <= t_lane
        for h in range(N_HEADS):
            carries[h] = _flash_t(jnp.where(causal, sbuf[h, c], MASK_VALUE), *carries[h], value(qi, c, h))
    o_ref[...] = jnp.transpose(jnp.concatenate([acc / l for _, l, acc in carries], axis=0))


def _fox_call(fq, fk, fvt):
    B, T, _ = fq.shape
    tq = TQ_FOX
    return pl.pallas_call(
        _fox_kernel,
        grid=(B, T // tq),
        in_specs=[pl.BlockSpec((None, tq, 256), lambda b, i: (b, i, 0)),
                  pl.BlockSpec((None, T, 512), lambda b, i: (b, 0, 0)),
                  pl.BlockSpec((None, T // UPD, N_HEADS * V_ROWS, UPD), lambda b, i: (b, 0, 0, 0))],
        out_specs=pl.BlockSpec((None, tq, 256), lambda b, i: (b, i, 0)),
        out_shape=jax.ShapeDtypeStruct((B, T, 256), F32),
        scratch_shapes=[pltpu.VMEM((N_HEADS, KV_TILE // UPD, UPD, tq), F32)],
        compiler_params=pltpu.CompilerParams(dimension_semantics=("arbitrary", "arbitrary"),
                                             vmem_limit_bytes=VMEM_LIMIT),
        name="fox",
    )(fq, fk, fvt)


def _merge_kernel(x_ref, g_ref, wg_ref, wm_ref, wb_ref, wo_ref, fg_ref, o0_ref, o1_ref, o2_ref, o3_ref,
                  out_ref, *, last, tm):
    rows = tm // 2
    halves = [slice(u * rows, (u + 1) * rows) for u in range(2)]
    xs = [x_ref[r, :] for r in halves]
    hs = [_rmsnorm(x, g_ref[...]).astype(BF16) for x in xs]
    accs = []
    for u, r in enumerate(halves):
        gates = _dot(hs[u], wg_ref[...])
        acc = None
        for i, o_ref in enumerate((o0_ref, o1_ref, o2_ref, o3_ref)):
            merge = _dot(hs[u], wm_ref[:, i * D_MODEL:(i + 1) * D_MODEL])
            gate = gates[:, i * BRANCH_WIDTH:(i + 1) * BRANCH_WIDTH]
            a = (o_ref[r, :] * (gate * _sigmoid(gate))).astype(BF16)
            term = _sigmoid(merge) * _dot(a, wb_ref[i])
            acc = term if acc is None else acc + term
        accs.append(acc)
    for u, r in enumerate(halves):
        y = xs[u] + _dot(accs[u].astype(BF16), wo_ref[...])
        if last:
            y = _rmsnorm(y, fg_ref[...])
        out_ref[r, :] = y


def _merge_call(x, g, wg, wm, wb, wo, fg, o_nsa, o_pool, o_conv, o_fox, last):
    B, T, _ = x.shape
    tm = TM_MERGE
    tok = lambda width: pl.BlockSpec((None, tm, width), lambda b, t: (b, t, 0))
    const = lambda shape: pl.BlockSpec(shape, lambda b, t: (0,) * len(shape))
    return pl.pallas_call(
        functools.partial(_merge_kernel, last=last, tm=tm),
        grid=(B, T // tm),
        in_specs=[tok(D_MODEL), const((1, D_MODEL)), const((D_MODEL, N_BRANCH * BRANCH_WIDTH)),
                  const((D_MODEL, N_BRANCH * D_MODEL)), const((N_BRANCH, BRANCH_WIDTH, D_MODEL)),
                  const((D_MODEL, D_MODEL)), const((1, D_MODEL)), tok(256), tok(256), tok(256), tok(256)],
        out_specs=tok(D_MODEL),
        out_shape=jax.ShapeDtypeStruct((B, T, D_MODEL), F32),
        compiler_params=pltpu.CompilerParams(dimension_semantics=("arbitrary", "arbitrary"),
                                             vmem_limit_bytes=VMEM_LIMIT),
        name="merge",
    )(x, g, wg, wm, wb, wo, fg, o_nsa, o_pool, o_conv, o_fox)


def _rope_tables(pos):
    n = pos.shape[0]
    inv = ROPE_THETA ** (-jnp.arange(ROPE_HALF, dtype=F32) / ROPE_HALF)
    ang = pos.astype(F32)[:, None] * inv[None, :]
    cos, sin = jnp.cos(ang), jnp.sin(ang)
    rest = HEAD_DIM - 2 * ROPE_HALF
    c = jnp.concatenate([cos, cos, jnp.ones((n, rest), F32)], axis=1)
    s = jnp.concatenate([-sin, sin, jnp.zeros((n, rest), F32)], axis=1)
    return jnp.concatenate([c, c], axis=1), jnp.concatenate([s, s], axis=1)


def _proj_weight(w_in):
    col = lambda name, width: w_in[:, _OFF[name]:_OFF[name] + width]
    z64 = jnp.zeros((D_MODEL, HEAD_DIM), w_in.dtype)
    misc = jnp.concatenate([col("nsa_g", 12), col("fox_f", 4), jnp.zeros((D_MODEL, 112), w_in.dtype)], axis=1)
    fox_k = []
    for h in range(N_HEADS):
        fox_k += [w_in[:, _OFF["fox_k"] + h * HEAD_DIM:_OFF["fox_k"] + (h + 1) * HEAD_DIM], z64]
    w = jnp.concatenate([col("nsa_q", 256), col("k_s", 64), z64, col("k_w", 64), z64,
                         col("v_s", 64), col("v_w", 64), col("k_c", 128), misc,
                         col("fox_q", 256)] + fox_k + [col("fox_v", 256), col("pool", 256), col("conv", 768)],
                        axis=1)
    assert w.shape[1] == _P_COLS and w.dtype == BF16
    return w


def _compress_weights(cmp_pos, cmp_w1, cmp_b1, cmp_w2, cmp_b2):
    half = CMP_LEN // 2

    def w1_half(lo):
        wk = cmp_w1[0, lo * HEAD_DIM:(lo + half) * HEAD_DIM].reshape(half, HEAD_DIM, CMP_HIDDEN)
        wv = cmp_w1[1, lo * HEAD_DIM:(lo + half) * HEAD_DIM].reshape(half, HEAD_DIM, CMP_HIDDEN)
        z = jnp.zeros_like(wk)
        top = jnp.concatenate([wk, z], axis=2)
        bot = jnp.concatenate([z, wv], axis=2)
        return jnp.concatenate([top, bot], axis=1).astype(BF16)

    def pos_half(lo):
        return jnp.concatenate([cmp_pos[0, lo:lo + half], cmp_pos[1, lo:lo + half]], axis=1)

    lane = np.arange(HEAD_DIM)
    perm = np.where(lane < ROPE_HALF, lane + ROPE_HALF, np.where(lane < 2 * ROPE_HALF, lane - ROPE_HALF, lane))
    w2k, w2v = cmp_w2[0], cmp_w2[1]
    zk = jnp.zeros_like(w2k)
    w2 = jnp.concatenate([
        jnp.concatenate([w2k, w2k, w2k[:, perm], w2k[:, perm], zk, zk], axis=1),
        jnp.concatenate([zk, zk, zk, zk, w2v, w2v], axis=1)], axis=0).astype(BF16)
    b2k, b2v = cmp_b2[0], cmp_b2[1]
    b2 = jnp.concatenate([b2k, b2k, b2k[perm], b2k[perm], b2v, b2v])[None, :]
    b1 = jnp.concatenate([cmp_b1[0], cmp_b1[1]])[None, :]
    return pos_half(0), pos_half(half), w1_half(0), w1_half(half), b1, w2, b2


def _block_diag(pool_w):
    n, c, _ = pool_w.shape
    out = jnp.zeros((n * c, n * c), pool_w.dtype)
    for i in range(n):
        out = out.at[i * c:(i + 1) * c, i * c:(i + 1) * c].set(pool_w[i])
    return out


def kernel(x, norm_g, w_in, fox_f_bias, cmp_pos, cmp_w1, cmp_b1, cmp_w2, cmp_b2, pool_w, pool_scale, conv_w,
           w_branch, w_out, final_norm_g):
    B, T, _ = x.shape
    depth = norm_g.shape[0]
    assert T % TM_PROJ == 0 and T % KV_TILE == 0 and T // SEL_LEN <= HEAD_DIM and T >= WINDOW + CHUNK
    assert NSA_SUB * CHUNK == KV_TILE == TQ_FOX and TM_PROJ % KV_TILE == 0
    rows = T // CMP_STRIDE
    nblk = T // SEL_LEN

    cc_tok, ss_tok = _rope_tables(jnp.arange(T))
    cmp_end = jnp.arange(rows) * CMP_STRIDE + CMP_LEN - 1
    cc_cmp, ss_cmp = _rope_tables(cmp_end)
    ci = np.arange(rows)[:, None] * CMP_STRIDE
    sj = np.arange(nblk)[None, :] * SEL_LEN
    overlap_t = jnp.asarray(((ci < sj + SEL_LEN) & (ci + CMP_LEN > sj)).T, F32)
    tri = jnp.asarray(np.triu(np.ones((TM_PROJ, TM_PROJ), np.float32)), BF16)

    w_in_bf = w_in.astype(BF16)
    for l in range(depth):
        w1 = _proj_weight(w_in_bf[l])
        fb = jnp.zeros((8, 1), F32).at[4:8, 0].set(fox_f_bias[l])
        (q, ks, kw, vst, vwt, kvc, mt, fq, fk, fvt, o_pool, o_conv) = _proj_call(
            x, norm_g[l][None, :], w1, cc_tok, ss_tok, fb, conv_w[l], _block_diag(pool_w[l]).astype(BF16),
            pool_scale[l][None, :], tri)
        kc, vct = _compress_call(kvc, *_compress_weights(cmp_pos[l], cmp_w1[l], cmp_b1[l], cmp_w2[l], cmp_b2[l]),
                                 cc_cmp, ss_cmp)
        o_nsa = _nsa_call(q, ks, kw, vst, vwt, kc, vct, mt, overlap_t)
        o_fox = _fox_call(fq, fk, fvt)
        wg = w_in_bf[l][:, _OFF["gate"]:_OFF["merge"]]
        wm = w_in_bf[l][:, _OFF["merge"]:_OFF["end"]]
        x = _merge_call(x, norm_g[l][None, :], wg, wm, w_branch[l].astype(BF16), w_out[l].astype(BF16),
                        final_norm_g[None, :], o_nsa, o_pool, o_conv, o_fox, last=(l == depth - 1))
    return x
```

```python
import functools

import numpy as np
import jax
import jax.numpy as jnp
from jax import lax
from jax.experimental import pallas as pl
from jax.experimental.pallas import tpu as pltpu

F32 = jnp.float32
BF16 = jnp.bfloat16

D_MODEL = 1024
N_BRANCH = 4
BRANCH_WIDTH = 256
HEAD_DIM = 64
N_HEADS = 4
CMP_LEN = 32
CMP_STRIDE = 16
CMP_HIDDEN = 128
CMP_GROUP = 8
SEL_LEN = 64
N_SEL = 16
WINDOW = 512
FORCE_BONUS = 1.0e4
ROPE_THETA = 500000.0
ROPE_HALF = 8
NORM_EPS = 1e-6
MASK_VALUE = -1e30

_OFF = dict(nsa_q=0, k_c=256, v_c=320, k_s=384, v_s=448, k_w=512, v_w=576, nsa_g=640, pool=652,
            conv=908, fox_q=1676, fox_k=1932, fox_v=2188, fox_f=2444, gate=2448, merge=3472, end=7568)

_P_ROPE = 0
_P_VSW = 512
_P_FOXQ = 896
_P_FOXK = 1152
_P_FOXV = 1664
_P_POOL = 1920
_P_CONV = 2176
_P_COLS = 2944

CHUNK = 128
UPD = 256
V_ROWS = HEAD_DIM + 16
LOG2E = 1.4426950408889634
Q_SCALE = HEAD_DIM ** -0.5 * LOG2E
TM_PROJ = 512
KV_TILE = 512
NSA_SUB = 4
TQ_FOX = 512
TM_MERGE = 512
N_MISC_T = 16
POOL_HALO = 16
CONV_HALO = 8
VMEM_LIMIT = 56 * 1024 * 1024


def _dot(a, b, precision=None):
    return jnp.dot(a, b, preferred_element_type=F32, precision=precision)


def _sigmoid(x):
    return 1.0 / (1.0 + jnp.exp(-x))


def _rmsnorm(x, g):
    return x * lax.rsqrt(jnp.mean(x * x, axis=-1, keepdims=True) + NORM_EPS) * g


def _flash_t(s, m, l, acc, v_t):
    m_new = jnp.maximum(m, jnp.max(s, axis=0, keepdims=True))
    alpha = jnp.exp2(m - m_new)
    pv = _dot(v_t, jnp.exp2(s - m_new).astype(BF16))
    l = alpha * l + pv[HEAD_DIM:HEAD_DIM + 1]
    acc = alpha * acc + pv[0:HEAD_DIM]
    return m_new, l, acc


def _attend_chunks(streams, n):
    nxt = [st[0](0) for st in streams]
    carries = [st[2] for st in streams]
    for c in range(n):
        cur = nxt
        if c + 1 < n:
            nxt = [st[0](c + 1) for st in streams]
        carries = [_flash_t(cur[i], *carries[i], streams[i][1](c)) for i in range(len(streams))]
    return carries


def _flash_init(n):
    return (jnp.full((1, n), MASK_VALUE, F32), jnp.zeros((1, n), F32), jnp.zeros((HEAD_DIM, n), F32))


def _proj_kernel(x_ref, xn_ref, g_ref, w_ref, cc_ref, ss_ref, fb_ref, cw_ref, pw_ref, ps_ref, tri_ref,
                 q_ref, ks_ref, kw_ref, vst_ref, vwt_ref, kvc_ref, mt_ref, fq_ref, fk_ref, fvt_ref,
                 opool_ref, oconv_ref, pext, cext, ccarry, hbuf, *, tm):
    ti = pl.program_id(1)

    @pl.when((pl.program_id(0) == 0) & (ti == 0))
    def _():
        hbuf[...] = _rmsnorm(x_ref[...], g_ref[...]).astype(BF16)

    h = hbuf[...]

    def seg(lo, width):
        return _dot(h, w_ref[:, lo:lo + width])

    lane = lax.broadcasted_iota(jnp.int32, (tm, 128), 1)
    row = lax.broadcasted_iota(jnp.int32, (tm, 128), 0) + ti * tm
    first_half = (lane % HEAD_DIM) < ROPE_HALF
    cc = cc_ref[...]
    ss = ss_ref[...]

    def rope(xc):
        partner = jnp.where(first_half, pltpu.roll(xc, 128 - ROPE_HALF, 1), pltpu.roll(xc, ROPE_HALF, 1))
        return xc * cc + partner * ss

    @pl.when(ti == 0)
    def _():
        ccarry[...] = jnp.zeros_like(ccarry)
        pext[0:POOL_HALO, :] = jnp.zeros((POOL_HALO, BRANCH_WIDTH), F32)
        cext[0:CONV_HALO, :] = jnp.zeros((CONV_HALO, BRANCH_WIDTH), F32)

    g_small = seg(_P_VSW, 384)
    g_rope = seg(_P_ROPE, 512)

    z_t = jnp.transpose(g_small[:, 256:384])[0:N_MISC_T, :]
    gates = _sigmoid(z_t)
    row8 = lax.broadcasted_iota(jnp.int32, (8, tm), 0)
    zb = z_t[8:16] + fb_ref[...]
    logf = jnp.where(row8 >= 4, jnp.minimum(zb, 0.0) - jnp.log(1.0 + jnp.exp(-jnp.abs(zb))), 0.0)

    def split3(v):
        hi = v.astype(BF16).astype(F32)
        mid = (v - hi).astype(BF16).astype(F32)
        return hi, mid, (v - hi) - mid

    parts = jnp.concatenate(list(split3(logf)) + [jnp.zeros((8, tm), F32)], axis=0).astype(BF16)
    part_sums = _dot(parts, tri_ref[...])
    csum = (part_sums[0:8] + part_sums[8:16]) + part_sums[16:24] + ccarry[:, 0:1]
    ccarry[...] = jnp.broadcast_to(csum[:, tm - 1:tm], (8, 128))
    mt_ref[0:8, :] = gates[0:8]
    mt_ref[8:16, :] = jnp.where(row8 < 4, gates[8:16], csum)

    g_pool = seg(_P_POOL, 256)
    cv = seg(_P_CONV, 768)

    pext[POOL_HALO:POOL_HALO + tm, :] = g_pool

    def pld(k, c):
        return pext[pl.ds(POOL_HALO - k, tm), c * 128:(c + 1) * 128]

    left = lane < 64
    rowp1 = (row + 1).astype(F32)

    def cnt(w):
        return jnp.minimum(rowp1, float(w))

    e0 = pld(0, 0)
    s2 = e0 + pld(1, 0)
    s4 = s2 + pld(2, 0) + pld(3, 0)
    p0 = jnp.where(left, s2 / cnt(2), s4 / cnt(4)) - e0
    f0 = pld(0, 1)
    s8 = f0
    for k in range(1, 8):
        s8 = s8 + pld(k, 1)
    s16 = s8
    for k in range(8, 16):
        s16 = s16 + pld(k, 1)
    p1 = jnp.where(left, s8 / cnt(8), s16 / cnt(16)) - f0
    pw = pw_ref[...]
    mixed = _dot(p0.astype(BF16), pw[0:128, :]) + _dot(p1.astype(BF16), pw[128:256, :])
    opool_ref[...] = mixed * ps_ref[...]
    pext[0:POOL_HALO, :] = pext[tm:tm + POOL_HALO, :]

    g_fv = seg(_P_FOXV, 256)
    g_fk = seg(_P_FOXK, 512)
    g_fq = seg(_P_FOXQ, 256)

    q_ref[:, 0:128] = (rope(g_rope[:, 0:128]) * Q_SCALE).astype(BF16)
    q_ref[:, 128:256] = (rope(g_rope[:, 128:256]) * Q_SCALE).astype(BF16)
    block_onehot = jnp.where(lane - HEAD_DIM == row // SEL_LEN, 1.0, 0.0)
    ks_ref[...] = (rope(g_rope[:, 256:384]) + block_onehot).astype(BF16)
    kw_ref[...] = rope(g_rope[:, 384:512]).astype(BF16)
    vsw_t = jnp.transpose(g_small[:, 0:128]).astype(BF16)
    ones_upd = jnp.ones((V_ROWS - HEAD_DIM, UPD), BF16)
    for c in range(tm // UPD):
        vst_ref[c] = jnp.concatenate([vsw_t[0:HEAD_DIM, c * UPD:(c + 1) * UPD], ones_upd], axis=0)
    for c in range(tm // CHUNK):
        vwt_ref[c] = jnp.concatenate([vsw_t[HEAD_DIM:128, c * CHUNK:(c + 1) * CHUNK], ones_upd[:, 0:CHUNK]], axis=0)
    kvc_ref[...] = g_small[:, 128:256]

    fq_ref[...] = (g_fq * Q_SCALE).astype(BF16)
    terms = split3(csum * (-LOG2E))
    row16 = lax.broadcasted_iota(jnp.int32, (16, tm), 0)
    extra_t = jnp.zeros((16, tm), F32)
    for hd in range(N_HEADS):
        for j in range(3):
            src = jnp.broadcast_to(terms[j][4 + hd:5 + hd, :], (16, tm))
            extra_t = jnp.where(row16 == 3 * hd + j, src, extra_t)
    extra = jnp.transpose(jnp.concatenate(
        [jnp.zeros((HEAD_DIM, tm), F32), extra_t, jnp.zeros((128 - HEAD_DIM - 16, tm), F32)], axis=0))
    for hd in range(N_HEADS):
        fk_ref[:, hd * 128:(hd + 1) * 128] = (g_fk[:, hd * 128:(hd + 1) * 128] + extra).astype(BF16)
    fv_t = jnp.transpose(g_fv).astype(BF16)
    for c in range(tm // UPD):
        cols = slice(c * UPD, (c + 1) * UPD)
        fvt_ref[c] = jnp.concatenate(
            [piece for hd in range(N_HEADS) for piece in (fv_t[hd * HEAD_DIM:(hd + 1) * HEAD_DIM, cols], ones_upd)],
            axis=0)

    u = cv[:, 512:768] * cv[:, 0:256]
    cext[CONV_HALO:CONV_HALO + tm, :] = u
    y = (cext[pl.ds(CONV_HALO - 2, tm), :] * cw_ref[0:1, :]
         + cext[pl.ds(CONV_HALO - 1, tm), :] * cw_ref[1:2, :]
         + u * cw_ref[2:3, :])
    oconv_ref[...] = cv[:, 256:512] * y
    cext[0:CONV_HALO, :] = cext[tm:tm + CONV_HALO, :]

    hbuf[...] = _rmsnorm(xn_ref[...], g_ref[...]).astype(BF16)


def _proj_call(x, g, w1, cc, ss, fb, cw, pw, ps, tri):
    B, T, _ = x.shape
    tm = TM_PROJ
    nt = T // tm
    tok = lambda width: pl.BlockSpec((None, tm, width), lambda b, t: (b, t, 0))
    chunked = lambda rows, keys: pl.BlockSpec((None, tm // keys, rows, keys), lambda b, t: (b, t, 0, 0))
    const = lambda shape: pl.BlockSpec(shape, lambda b, t: (0,) * len(shape))
    out_shape = [
        jax.ShapeDtypeStruct((B, T, 256), BF16),
        jax.ShapeDtypeStruct((B, T, 128), BF16),
        jax.ShapeDtypeStruct((B, T, 128), BF16),
        jax.ShapeDtypeStruct((B, T // UPD, V_ROWS, UPD), BF16),
        jax.ShapeDtypeStruct((B, T // CHUNK, V_ROWS, CHUNK), BF16),
        jax.ShapeDtypeStruct((B, T, 128), F32),
        jax.ShapeDtypeStruct((B, nt, N_MISC_T, tm), F32),
        jax.ShapeDtypeStruct((B, T, 256), BF16),
        jax.ShapeDtypeStruct((B, T, 512), BF16),
        jax.ShapeDtypeStruct((B, T // UPD, N_HEADS * V_ROWS, UPD), BF16),
        jax.ShapeDtypeStruct((B, T, 256), F32),
        jax.ShapeDtypeStruct((B, T, 256), F32),
    ]
    out_specs = [tok(256), tok(128), tok(128), chunked(V_ROWS, UPD), chunked(V_ROWS, CHUNK), tok(128),
                 pl.BlockSpec((None, None, N_MISC_T, tm), lambda b, t: (b, t, 0, 0)),
                 tok(256), tok(512), chunked(N_HEADS * V_ROWS, UPD), tok(256), tok(256)]
    def next_tile(b, t):
        lin = jnp.minimum(b * nt + t + 1, B * nt - 1)
        return lin // nt, lin % nt, 0

    in_specs = [tok(D_MODEL), pl.BlockSpec((None, tm, D_MODEL), next_tile),
                const((1, D_MODEL)), const((D_MODEL, _P_COLS)),
                pl.BlockSpec((tm, 128), lambda b, t: (t, 0)), pl.BlockSpec((tm, 128), lambda b, t: (t, 0)),
                const((8, 1)), const((3, 256)), const((256, 256)), const((1, 256)), const((tm, tm))]
    return pl.pallas_call(
        functools.partial(_proj_kernel, tm=tm),
        grid=(B, nt),
        in_specs=in_specs,
        out_specs=out_specs,
        out_shape=out_shape,
        scratch_shapes=[pltpu.VMEM((tm + POOL_HALO, 256), F32), pltpu.VMEM((tm + CONV_HALO, 256), F32),
                        pltpu.VMEM((8, 128), F32), pltpu.VMEM((tm, D_MODEL), BF16)],
        compiler_params=pltpu.CompilerParams(dimension_semantics=("arbitrary", "arbitrary"),
                                             vmem_limit_bytes=VMEM_LIMIT),
        name="proj",
    )(x, x, g, w1, cc, ss, fb, cw, pw, ps, tri)


def _compress_kernel(x_ref, pt_ref, pb_ref, wt_ref, wb_ref, b1_ref, w2_ref, b2_ref, cc_ref, ss_ref,
                     kc_ref, vct_ref, *, rows):
    a = b = None
    for l in range(CMP_STRIDE):
        x_l = x_ref[pl.ds(l, rows, stride=CMP_STRIDE), :]
        a_l = _dot((x_l + pt_ref[l:l + 1, :]).astype(BF16), wt_ref[l])
        b_l = _dot((x_l + pb_ref[l:l + 1, :]).astype(BF16), wb_ref[l])
        a, b = (a_l, b_l) if a is None else (a + a_l, b + b_l)
    hid = a + pltpu.roll(b, rows - 1, 0) + b1_ref[...]
    act = hid * _sigmoid(hid)
    out = _dot(act.astype(BF16), w2_ref[...]) + b2_ref[...]
    lane = lax.broadcasted_iota(jnp.int32, (rows, 128), 1)
    roped = out[:, 0:128] * cc_ref[...] + out[:, 128:256] * ss_ref[...]
    block = lax.broadcasted_iota(jnp.int32, (rows, 128), 0)
    group_onehot = jnp.where(lane - HEAD_DIM == block // CMP_GROUP, 1.0, 0.0)
    kc_ref[...] = jnp.where(lane < HEAD_DIM, roped, group_onehot).astype(BF16)
    vct_ref[...] = jnp.transpose(out[:, 256:384])[0:HEAD_DIM, :].astype(BF16)


def _compress_call(kvc, pt, pb, wt, wb, b1, w2, b2, cc, ss):
    B, T, width = kvc.shape
    rows = T // CMP_STRIDE
    const = lambda shape: pl.BlockSpec(shape, lambda b: (0,) * len(shape))
    return pl.pallas_call(
        functools.partial(_compress_kernel, rows=rows),
        grid=(B,),
        in_specs=[pl.BlockSpec((None, T, width), lambda b: (b, 0, 0)),
                  const((CMP_STRIDE, width)), const((CMP_STRIDE, width)),
                  const((CMP_STRIDE, width, 256)), const((CMP_STRIDE, width, 256)),
                  const((1, 256)), const((256, 384)), const((1, 384)), const((rows, 128)), const((rows, 128))],
        out_specs=[pl.BlockSpec((None, rows, 128), lambda b: (b, 0, 0)),
                   pl.BlockSpec((None, HEAD_DIM, rows), lambda b: (b, 0, 0))],
        out_shape=[jax.ShapeDtypeStruct((B, rows, 128), BF16), jax.ShapeDtypeStruct((B, HEAD_DIM, rows), BF16)],
        compiler_params=pltpu.CompilerParams(dimension_semantics=("arbitrary",),
                                             vmem_limit_bytes=VMEM_LIMIT),
        name="compress",
    )(kvc, pt, pb, wt, wb, b1, w2, b2, cc, ss)


def _nsa_kernel(q_ref, ks_ref, kw_ref, vst_ref, vwt_ref, kc_ref, vct_ref, mt_ref, ovt_ref,
                o_ref, key_ref, sbuf, cbuf, *, nblk, ncmp):
    step = pl.program_id(1)
    tq = CHUNK
    nq = N_HEADS * tq
    hw = nq // 2
    halves = (slice(0, hw), slice(hw, nq))
    n_c = KV_TILE // UPD
    n_win = WINDOW // tq
    subs = range(NSA_SUB)
    qis = [step * NSA_SUB + u for u in subs]
    q_local = lax.broadcasted_iota(jnp.int32, (1, nq), 1) % tq
    k_local = lax.broadcasted_iota(jnp.int32, (tq, hw), 0)
    q_loc = q_local[:, 0:hw]
    causal = k_local <= q_loc

    q4, qt_plain = [], []
    group = lax.broadcasted_iota(jnp.int32, (HEAD_DIM, nq), 0)
    slab = 2 * CMP_GROUP
    n_slab = lax.broadcasted_iota(jnp.int32, (slab, nq), 0)
    for u in subs:
        q_t = jnp.transpose(q_ref[u * tq:(u + 1) * tq, :].astype(F32))
        q4.append(jnp.concatenate([q_t[HEAD_DIM * h:HEAD_DIM * (h + 1)] for h in range(N_HEADS)], axis=1))
        qt_plain.append(jnp.concatenate([q4[u], jnp.zeros((HEAD_DIM, nq), F32)], axis=0).astype(BF16))
        group_bias = jnp.where(group > qis[u], MASK_VALUE, 0.0)
        qt_cmp = jnp.concatenate([q4[u], group_bias], axis=0).astype(BF16)
        cbuf[u] = _dot(kc_ref[...], qt_cmp)
        start = pl.multiple_of(jnp.maximum(qis[u] - 1, 0) * CMP_GROUP, CMP_GROUP)
        visible = ((start + n_slab) * CMP_STRIDE + (CMP_LEN - 1)) <= qis[u] * tq + q_local
        cbuf[u, pl.ds(start, slab), :] = jnp.where(visible, cbuf[u, pl.ds(start, slab), :], MASK_VALUE)

    o_cmp, imp = [], []
    for u in subs:
        s = cbuf[u]
        mc = jnp.max(s, axis=0, keepdims=True)
        mc = jnp.where(mc <= MASK_VALUE, 0.0, mc)
        pc = jnp.exp2(s - mc)
        lc = jnp.sum(pc, axis=0, keepdims=True)
        pc = pc * (1.0 / jnp.where(lc > 0.0, lc, 1.0))
        o_cmp.append(_dot(vct_ref[...], pc.astype(BF16)))
        psum = (pc[:, 0:tq] + pc[:, tq:2 * tq]) + (pc[:, 2 * tq:3 * tq] + pc[:, 3 * tq:4 * tq])
        imp.append(_dot(ovt_ref[...], psum, precision=lax.Precision.HIGHEST))

    def win_stream(u, hs):
        qi = qis[u]

        def score(c):
            if c == 0:
                own = kw_ref[pl.ds(pl.multiple_of(qi * tq, tq), tq), :]
                return jnp.where(causal, _dot(own, qt_plain[u][:, hs]), MASK_VALUE)
            ci = qi - n_win + (c - 1)
            start = pl.multiple_of(jnp.maximum(ci, 0) * tq, tq)
            s = _dot(kw_ref[pl.ds(start, tq), :], qt_plain[u][:, hs])
            ok = ((k_local > q_loc) & (ci >= 0)) if c == 1 else jnp.broadcast_to(ci >= 0, (tq, hw))
            return jnp.where(ok, s, MASK_VALUE)

        def value(c):
            return vwt_ref[qi] if c == 0 else vwt_ref[jnp.maximum(qi - n_win + (c - 1), 0)]

        return score, value, _flash_init(hw)

    carries_w = _attend_chunks([win_stream(u, hs) for u in subs for hs in halves], n_win + 1)
    o_win = [jnp.concatenate([acc / l for _, l, acc in carries_w[2 * u:2 * u + 2]], axis=1) for u in subs]

    jb = lax.broadcasted_iota(jnp.int32, (nblk, tq), 0)
    sub8 = lax.broadcasted_iota(jnp.int32, (8, tq), 0)
    later_rows = [sub8 > r for r in range(8)]
    qt_sel = []
    for u in subs:
        tt = qis[u] * tq + lax.broadcasted_iota(jnp.int32, (nblk, tq), 1)
        jt = tt // SEL_LEN
        forced = (jb == 0) | (jb == jt) | (jb == jt - 1)
        valid = jb * SEL_LEN <= tt
        key = jnp.where(valid, jnp.where(forced, imp[u] + FORCE_BONUS, imp[u]), MASK_VALUE)
        key_ref[u] = key
        key_rows = [key[8 * v:8 * v + 8] for v in range(nblk // 8)]
        ranks = [jnp.zeros((8, tq), F32) for _ in key_rows]
        for jp in range(nblk):
            other = jnp.broadcast_to(key_ref[u, pl.ds(jp, 1), :], (8, tq))
            for v, kv in enumerate(key_rows):
                if 8 * v > jp:
                    ahead = other >= kv
                elif 8 * v + 7 <= jp:
                    ahead = other > kv
                else:
                    ahead = (other > kv) | ((other == kv) & later_rows[jp - 8 * v])
                ranks[v] = ranks[v] + jnp.where(ahead, 1.0, 0.0)
        selected = jnp.concatenate(ranks, axis=0) < float(min(N_SEL, nblk))
        sel_bias = jnp.where(selected, 0.0, MASK_VALUE)
        if nblk < HEAD_DIM:
            sel_bias = jnp.concatenate([sel_bias, jnp.zeros((HEAD_DIM - nblk, tq), F32)], axis=0)
        qt_sel.append(jnp.concatenate([q4[u], jnp.concatenate([sel_bias] * N_HEADS, axis=1)], axis=0).astype(BF16))

    def slc_scores(u, kt, c):
        start = pl.multiple_of((kt * n_c + c) * UPD, UPD)
        return _dot(ks_ref[pl.ds(start, UPD), :], qt_sel[u])

    def slc_update(u, carry, s, kt, c):
        return tuple(_flash_t(s[:, hs], *carry[i], vst_ref[kt * n_c + c]) for i, hs in enumerate(halves))

    for c in range(n_c):
        for u in subs:
            sbuf[u, c] = slc_scores(u, 0, c)

    def slc_tile(kt, carries):
        carries = list(carries)
        for c in range(n_c):
            for u in subs:
                s = sbuf[u, c]
                sbuf[u, c] = slc_scores(u, kt + 1, c)
                carries[u] = slc_update(u, carries[u], s, kt, c)
        return tuple(carries)

    carries = tuple((_flash_init(hw), _flash_init(hw)) for u in subs)
    carries = lax.fori_loop(0, step // 2, lambda p, cs: slc_tile(2 * p + 1, slc_tile(2 * p, cs)), carries)
    carries = list(lax.fori_loop(2 * (step // 2), step, slc_tile, carries))
    k_upd = lax.broadcasted_iota(jnp.int32, (UPD, nq), 0)
    for c in range(n_c):
        first, last = c * UPD // tq, (c + 1) * UPD // tq - 1
        for u in subs:
            if first > u:
                continue
            s = sbuf[u, c]
            if last >= u:
                s = jnp.where(c * UPD + k_upd <= u * tq + q_local, s, MASK_VALUE)
            carries[u] = slc_update(u, carries[u], s, step, c)

    for u in subs:
        o_slc = jnp.concatenate([acc / l for _, l, acc in carries[u]], axis=1)
        g = mt_ref[:, u * tq:(u + 1) * tq]
        heads = []
        for h in range(N_HEADS):
            cols = slice(h * tq, (h + 1) * tq)
            heads.append(g[h:h + 1] * o_cmp[u][:, cols] + g[4 + h:5 + h] * o_slc[:, cols]
                         + g[8 + h:9 + h] * o_win[u][:, cols])
        o_ref[u * tq:(u + 1) * tq, :] = jnp.transpose(jnp.concatenate(heads, axis=0))


def _nsa_call(q, ks, kw, vst, vwt, kc, vct, mt, ovt):
    B, T, _ = q.shape
    nblk, ncmp = ovt.shape
    tq = CHUNK * NSA_SUB
    per_mt = TM_PROJ // tq
    full = lambda width: pl.BlockSpec((None, T, width), lambda b, i: (b, 0, 0))
    vfull = lambda keys: pl.BlockSpec((None, T // keys, V_ROWS, keys), lambda b, i: (b, 0, 0, 0))
    return pl.pallas_call(
        functools.partial(_nsa_kernel, nblk=nblk, ncmp=ncmp),
        grid=(B, T // tq),
        in_specs=[pl.BlockSpec((None, tq, 256), lambda b, i: (b, i, 0)),
                  full(128), full(128), vfull(UPD), vfull(CHUNK),
                  pl.BlockSpec((None, ncmp, 128), lambda b, i: (b, 0, 0)),
                  pl.BlockSpec((None, HEAD_DIM, ncmp), lambda b, i: (b, 0, 0)),
                  pl.BlockSpec((None, None, N_MISC_T, tq), lambda b, i: (b, i // per_mt, 0, i % per_mt)),
                  pl.BlockSpec((nblk, ncmp), lambda b, i: (0, 0))],
        out_specs=pl.BlockSpec((None, tq, 256), lambda b, i: (b, i, 0)),
        out_shape=jax.ShapeDtypeStruct((B, T, 256), F32),
        scratch_shapes=[pltpu.VMEM((NSA_SUB, nblk, CHUNK), F32),
                        pltpu.VMEM((NSA_SUB, KV_TILE // UPD, UPD, N_HEADS * CHUNK), F32),
                        pltpu.VMEM((NSA_SUB, ncmp, N_HEADS * CHUNK), F32)],
        compiler_params=pltpu.CompilerParams(dimension_semantics=("arbitrary", "arbitrary"),
                                             vmem_limit_bytes=VMEM_LIMIT),
        name="nsa",
    )(q, ks, kw, vst, vwt, kc, vct, mt, ovt)


def _fox_kernel(q_ref, k_ref, vt_ref, o_ref, sbuf):
    qi = pl.program_id(1)
    nq = TQ_FOX
    n_chunk = KV_TILE // UPD
    t_lane = qi * nq + lax.broadcasted_iota(jnp.int32, (1, nq), 1)
    k_local = lax.broadcasted_iota(jnp.int32, (UPD, nq), 0)
    q_t = jnp.transpose(q_ref[...].astype(F32))
    extra_row = lax.broadcasted_iota(jnp.int32, (HEAD_DIM, nq), 0)

    def ones_rows(h):
        return jnp.where((extra_row >= 3 * h) & (extra_row < 3 * h + 3), 1.0, 0.0)

    qts = [jnp.concatenate([q_t[HEAD_DIM * h:HEAD_DIM * (h + 1)], ones_rows(h)], axis=0).astype(BF16)
           for h in range(N_HEADS)]

    def scores(kt, c, h):
        start = pl.multiple_of((kt * n_chunk + c) * UPD, UPD)
        return _dot(k_ref[pl.ds(start, UPD), h * 128:(h + 1) * 128], qts[h])

    def value(kt, c, h):
        return vt_ref[kt * n_chunk + c, h * V_ROWS:(h + 1) * V_ROWS, :]

    for c in range(n_chunk):
        for h in range(N_HEADS):
            sbuf[h, c] = scores(0, c, h)

    def tile(kt, carries):
        carries = list(carries)
        for c in range(n_chunk):
            for h in range(N_HEADS):
                s = sbuf[h, c]
                sbuf[h, c] = scores(kt + 1, c, h)
                carries[h] = _flash_t(s, *carries[h], value(kt, c, h))
        return tuple(carries)

    carries = tuple(_flash_init(nq) for _ in range(N_HEADS))
    carries = lax.fori_loop(0, qi // 2, lambda p, cs: tile(2 * p + 1, tile(2 * p, cs)), carries)
    carries = list(lax.fori_loop(2 * (qi // 2), qi, tile, carries))
    for c in range(n_chunk):
        causal = qi * nq + c * UPD + k_local <= t_lane
        for h in range(N_HEADS):
            carries[h] = _flash_t(jnp.where(causal, sbuf[h, c], MASK_VALUE), *carries[h], value(qi, c, h))
    o_ref[...] = jnp.transpose(jnp.concatenate([acc / l for _, l, acc in carries], axis=0))


def _fox_call(fq, fk, fvt):
    B, T, _ = fq.shape
    tq = TQ_FOX
    return pl.pallas_call(
        _fox_kernel,
        grid=(B, T // tq),
        in_specs=[pl.BlockSpec((None, tq, 256), lambda b, i: (b, i, 0)),
                  pl.BlockSpec((None, T, 512), lambda b, i: (b, 0, 0)),
                  pl.BlockSpec((None, T // UPD, N_HEADS * V_ROWS, UPD), lambda b, i: (b, 0, 0, 0))],
        out_specs=pl.BlockSpec((None, tq, 256), lambda b, i: (b, i, 0)),
        out_shape=jax.ShapeDtypeStruct((B, T, 256), F32),
        scratch_shapes=[pltpu.VMEM((N_HEADS, KV_TILE // UPD, UPD, tq), F32)],
        compiler_params=pltpu.CompilerParams(dimension_semantics=("arbitrary", "arbitrary"),
                                             vmem_limit_bytes=VMEM_LIMIT),
        name="fox",
    )(fq, fk, fvt)


def _merge_kernel(x_ref, g_ref, wg_ref, wm_ref, wb_ref, wo_ref, fg_ref, o0_ref, o1_ref, o2_ref, o3_ref,
                  out_ref, *, last, tm):
    rows = tm // 2
    halves = [slice(u * rows, (u + 1) * rows) for u in range(2)]
    xs = [x_ref[r, :] for r in halves]
    hs = [_rmsnorm(x, g_ref[...]).astype(BF16) for x in xs]
    accs = []
    for u, r in enumerate(halves):
        gates = _dot(hs[u], wg_ref[...])
        acc = None
        for i, o_ref in enumerate((o0_ref, o1_ref, o2_ref, o3_ref)):
            merge = _dot(hs[u], wm_ref[:, i * D_MODEL:(i + 1) * D_MODEL])
            gate = gates[:, i * BRANCH_WIDTH:(i + 1) * BRANCH_WIDTH]
            a = (o_ref[r, :] * (gate * _sigmoid(gate))).astype(BF16)
            term = _sigmoid(merge) * _dot(a, wb_ref[i])
            acc = term if acc is None else acc + term
        accs.append(acc)
    for u, r in enumerate(halves):
        y = xs[u] + _dot(accs[u].astype(BF16), wo_ref[...])
        if last:
            y = _rmsnorm(y, fg_ref[...])
        out_ref[r, :] = y


def _merge_call(x, g, wg, wm, wb, wo, fg, o_nsa, o_pool, o_conv, o_fox, last):
    B, T, _ = x.shape
    tm = TM_MERGE
    tok = lambda width: pl.BlockSpec((None, tm, width), lambda b, t: (b, t, 0))
    const = lambda shape: pl.BlockSpec(shape, lambda b, t: (0,) * len(shape))
    return pl.pallas_call(
        functools.partial(_merge_kernel, last=last, tm=tm),
        grid=(B, T // tm),
        in_specs=[tok(D_MODEL), const((1, D_MODEL)), const((D_MODEL, N_BRANCH * BRANCH_WIDTH)),
                  const((D_MODEL, N_BRANCH * D_MODEL)), const((N_BRANCH, BRANCH_WIDTH, D_MODEL)),
                  const((D_MODEL, D_MODEL)), const((1, D_MODEL)), tok(256), tok(256), tok(256), tok(256)],
        out_specs=tok(D_MODEL),
        out_shape=jax.ShapeDtypeStruct((B, T, D_MODEL), F32),
        compiler_params=pltpu.CompilerParams(dimension_semantics=("arbitrary", "arbitrary"),
                                             vmem_limit_bytes=VMEM_LIMIT),
        name="merge",
    )(x, g, wg, wm, wb, wo, fg, o_nsa, o_pool, o_conv, o_fox)


def _rope_tables(pos):
    n = pos.shape[0]
    inv = ROPE_THETA ** (-jnp.arange(ROPE_HALF, dtype=F32) / ROPE_HALF)
    ang = pos.astype(F32)[:, None] * inv[None, :]
    cos, sin = jnp.cos(ang), jnp.sin(ang)
    rest = HEAD_DIM - 2 * ROPE_HALF
    c = jnp.concatenate([cos, cos, jnp.ones((n, rest), F32)], axis=1)
    s = jnp.concatenate([-sin, sin, jnp.zeros((n, rest), F32)], axis=1)
    return jnp.concatenate([c, c], axis=1), jnp.concatenate([s, s], axis=1)


def _proj_weight(w_in):
    col = lambda name, width: w_in[:, _OFF[name]:_OFF[name] + width]
    z64 = jnp.zeros((D_MODEL, HEAD_DIM), w_in.dtype)
    misc = jnp.concatenate([col("nsa_g", 12), col("fox_f", 4), jnp.zeros((D_MODEL, 112), w_in.dtype)], axis=1)
    fox_k = []
    for h in range(N_HEADS):
        fox_k += [w_in[:, _OFF["fox_k"] + h * HEAD_DIM:_OFF["fox_k"] + (h + 1) * HEAD_DIM], z64]
    w = jnp.concatenate([col("nsa_q", 256), col("k_s", 64), z64, col("k_w", 64), z64,
                         col("v_s", 64), col("v_w", 64), col("k_c", 128), misc,
                         col("fox_q", 256)] + fox_k + [col("fox_v", 256), col("pool", 256), col("conv", 768)],
                        axis=1)
    assert w.shape[1] == _P_COLS and w.dtype == BF16
    return w


def _compress_weights(cmp_pos, cmp_w1, cmp_b1, cmp_w2, cmp_b2):
    half = CMP_LEN // 2

    def w1_half(lo):
        wk = cmp_w1[0, lo * HEAD_DIM:(lo + half) * HEAD_DIM].reshape(half, HEAD_DIM, CMP_HIDDEN)
        wv = cmp_w1[1, lo * HEAD_DIM:(lo + half) * HEAD_DIM].reshape(half, HEAD_DIM, CMP_HIDDEN)
        z = jnp.zeros_like(wk)
        top = jnp.concatenate([wk, z], axis=2)
        bot = jnp.concatenate([z, wv], axis=2)
        return jnp.concatenate([top, bot], axis=1).astype(BF16)

    def pos_half(lo):
        return jnp.concatenate([cmp_pos[0, lo:lo + half], cmp_pos[1, lo:lo + half]], axis=1)

    lane = np.arange(HEAD_DIM)
    perm = np.where(lane < ROPE_HALF, lane + ROPE_HALF, np.where(lane < 2 * ROPE_HALF, lane - ROPE_HALF, lane))
    w2k, w2v = cmp_w2[0], cmp_w2[1]
    zk = jnp.zeros_like(w2k)
    w2 = jnp.concatenate([
        jnp.concatenate([w2k, w2k, w2k[:, perm], w2k[:, perm], zk, zk], axis=1),
        jnp.concatenate([zk, zk, zk, zk, w2v, w2v], axis=1)], axis=0).astype(BF16)
    b2k, b2v = cmp_b2[0], cmp_b2[1]
    b2 = jnp.concatenate([b2k, b2k, b2k[perm], b2k[perm], b2v, b2v])[None, :]
    b1 = jnp.concatenate([cmp_b1[0], cmp_b1[1]])[None, :]
    return pos_half(0), pos_half(half), w1_half(0), w1_half(half), b1, w2, b2


def _block_diag(pool_w):
    n, c, _ = pool_w.shape
    out = jnp.zeros((n * c, n * c), pool_w.dtype)
    for i in range(n):
        out = out.at[i * c:(i + 1) * c, i * c:(i + 1) * c].set(pool_w[i])
    return out


def kernel(x, norm_g, w_in, fox_f_bias, cmp_pos, cmp_w1, cmp_b1, cmp_w2, cmp_b2, pool_w, pool_scale, conv_w,
           w_branch, w_out, final_norm_g):
    B, T, _ = x.shape
    depth = norm_g.shape[0]
    assert T % TM_PROJ == 0 and T % KV_TILE == 0 and T // SEL_LEN <= HEAD_DIM and T >= WINDOW + CHUNK
    assert NSA_SUB * CHUNK == KV_TILE == TQ_FOX and TM_PROJ % KV_TILE == 0
    rows = T // CMP_STRIDE
    nblk = T // SEL_LEN

    cc_tok, ss_tok = _rope_tables(jnp.arange(T))
    cmp_end = jnp.arange(rows) * CMP_STRIDE + CMP_LEN - 1
    cc_cmp, ss_cmp = _rope_tables(cmp_end)
    ci = np.arange(rows)[:, None] * CMP_STRIDE
    sj = np.arange(nblk)[None, :] * SEL_LEN
    overlap_t = jnp.asarray(((ci < sj + SEL_LEN) & (ci + CMP_LEN > sj)).T, F32)
    tri = jnp.asarray(np.triu(np.ones((TM_PROJ, TM_PROJ), np.float32)), BF16)

    w_in_bf = w_in.astype(BF16)
    for l in range(depth):
        w1 = _proj_weight(w_in_bf[l])
        fb = jnp.zeros((8, 1), F32).at[4:8, 0].set(fox_f_bias[l])
        (q, ks, kw, vst, vwt, kvc, mt, fq, fk, fvt, o_pool, o_conv) = _proj_call(
            x, norm_g[l][None, :], w1, cc_tok, ss_tok, fb, conv_w[l], _block_diag(pool_w[l]).astype(BF16),
            pool_scale[l][None, :], tri)
        kc, vct = _compress_call(kvc, *_compress_weights(cmp_pos[l], cmp_w1[l], cmp_b1[l], cmp_w2[l], cmp_b2[l]),
                                 cc_cmp, ss_cmp)
        o_nsa = _nsa_call(q, ks, kw, vst, vwt, kc, vct, mt, overlap_t)
        o_fox = _fox_call(fq, fk, fvt)
        wg = w_in_bf[l][:, _OFF["gate"]:_OFF["merge"]]
        wm = w_in_bf[l][:, _OFF["merge"]:_OFF["end"]]
        x = _merge_call(x, norm_g[l][None, :], wg, wm, w_branch[l].astype(BF16), w_out[l].astype(BF16),
                        final_norm_g[None, :], o_nsa, o_pool, o_conv, o_fox, last=(l == depth - 1))
    return x
```

```python
import functools

import numpy as np
import jax
import jax.numpy as jnp
from jax import lax
from jax.experimental import pallas as pl
from jax.experimental.pallas import tpu as pltpu

F32 = jnp.float32
BF16 = jnp.bfloat16

D_MODEL = 1024
N_BRANCH = 4
BRANCH_WIDTH = 256
HEAD_DIM = 64
N_HEADS = 4
CMP_LEN = 32
CMP_STRIDE = 16
CMP_HIDDEN = 128
CMP_GROUP = 8
SEL_LEN = 64
N_SEL = 16
WINDOW = 512
FORCE_BONUS = 1.0e4
ROPE_THETA = 500000.0
ROPE_HALF = 8
NORM_EPS = 1e-6
MASK_VALUE = -1e30

_OFF = dict(nsa_q=0, k_c=256, v_c=320, k_s=384, v_s=448, k_w=512, v_w=576, nsa_g=640, pool=652,
            conv=908, fox_q=1676, fox_k=1932, fox_v=2188, fox_f=2444, gate=2448, merge=3472, end=7568)

_P_ROPE = 0
_P_VSW = 512
_P_FOXQ = 896
_P_FOXK = 1152
_P_FOXV = 1664
_P_POOL = 1920
_P_CONV = 2176
_P_COLS = 2944

CHUNK = 128
UPD = 256
V_ROWS = HEAD_DIM + 16
LOG2E = 1.4426950408889634
Q_SCALE = HEAD_DIM ** -0.5 * LOG2E
TM_PROJ = 512
KV_TILE = 512
NSA_SUB = 4
TQ_FOX = 512
TM_MERGE = 512
N_MISC_T = 16
POOL_HALO = 16
CONV_HALO = 8
VMEM_LIMIT = 56 * 1024 * 1024


def _dot(a, b, precision=None):
    return jnp.dot(a, b, preferred_element_type=F32, precision=precision)


def _sigmoid(x):
    return 1.0 / (1.0 + jnp.exp(-x))


def _rmsnorm(x, g):
    return x * lax.rsqrt(jnp.mean(x * x, axis=-1, keepdims=True) + NORM_EPS) * g


def _flash_t(s, m, l, acc, v_t):
    m_new = jnp.maximum(m, jnp.max(s, axis=0, keepdims=True))
    alpha = jnp.exp2(m - m_new)
    pv = _dot(v_t, jnp.exp2(s - m_new).astype(BF16))
    l = alpha * l + pv[HEAD_DIM:HEAD_DIM + 1]
    acc = alpha * acc + pv[0:HEAD_DIM]
    return m_new, l, acc


def _attend_chunks(streams, n):
    nxt = [st[0](0) for st in streams]
    carries = [st[2] for st in streams]
    for c in range(n):
        cur = nxt
        if c + 1 < n:
            nxt = [st[0](c + 1) for st in streams]
        carries = [_flash_t(cur[i], *carries[i], streams[i][1](c)) for i in range(len(streams))]
    return carries


def _run_tiles(tile, n, carries):
    def run(first, count, carries):
        for j in range(count):
            carries = tile(first + j, carries)
        return carries

    quads = n // 4
    carries = lax.fori_loop(0, quads, lambda i, cs: run(4 * i, 4, cs), carries)
    pairs = (n - 4 * quads) // 2
    carries = lax.fori_loop(0, pairs, lambda i, cs: run(4 * quads, 2, cs), carries)
    return lax.fori_loop(4 * quads + 2 * pairs, n, tile, carries)


def _flash_init(n):
    return (jnp.full((1, n), MASK_VALUE, F32), jnp.zeros((1, n), F32), jnp.zeros((HEAD_DIM, n), F32))


def _proj_kernel(x_ref, xn_ref, g_ref, w_ref, cc_ref, ss_ref, fb_ref, cw_ref, pw_ref, ps_ref, tri_ref,
                 q_ref, ks_ref, kw_ref, vst_ref, vwt_ref, kvc_ref, mt_ref, fq_ref, fk_ref, fvt_ref,
                 opool_ref, oconv_ref, pext, cext, ccarry, hbuf, *, tm):
    ti = pl.program_id(1)

    @pl.when((pl.program_id(0) == 0) & (ti == 0))
    def _():
        hbuf[...] = _rmsnorm(x_ref[...], g_ref[...]).astype(BF16)

    h = hbuf[...]

    def seg(lo, width):
        return _dot(h, w_ref[:, lo:lo + width])

    lane = lax.broadcasted_iota(jnp.int32, (tm, 128), 1)
    row = lax.broadcasted_iota(jnp.int32, (tm, 128), 0) + ti * tm
    first_half = (lane % HEAD_DIM) < ROPE_HALF
    cc = cc_ref[...]
    ss = ss_ref[...]

    def rope(xc):
        partner = jnp.where(first_half, pltpu.roll(xc, 128 - ROPE_HALF, 1), pltpu.roll(xc, ROPE_HALF, 1))
        return xc * cc + partner * ss

    @pl.when(ti == 0)
    def _():
        ccarry[...] = jnp.zeros_like(ccarry)
        pext[0:POOL_HALO, :] = jnp.zeros((POOL_HALO, BRANCH_WIDTH), F32)
        cext[0:CONV_HALO, :] = jnp.zeros((CONV_HALO, BRANCH_WIDTH), F32)

    g_small = seg(_P_VSW, 384)
    g_rope = seg(_P_ROPE, 512)

    z_t = jnp.transpose(g_small[:, 256:384])[0:N_MISC_T, :]
    gates = _sigmoid(z_t)
    row8 = lax.broadcasted_iota(jnp.int32, (8, tm), 0)
    zb = z_t[8:16] + fb_ref[...]
    logf = jnp.where(row8 >= 4, jnp.minimum(zb, 0.0) - jnp.log(1.0 + jnp.exp(-jnp.abs(zb))), 0.0)

    def split3(v):
        hi = v.astype(BF16).astype(F32)
        mid = (v - hi).astype(BF16).astype(F32)
        return hi, mid, (v - hi) - mid

    parts = jnp.concatenate(list(split3(logf)) + [jnp.zeros((8, tm), F32)], axis=0).astype(BF16)
    part_sums = _dot(parts, tri_ref[...])
    csum = (part_sums[0:8] + part_sums[8:16]) + part_sums[16:24] + ccarry[:, 0:1]
    ccarry[...] = jnp.broadcast_to(csum[:, tm - 1:tm], (8, 128))
    mt_ref[0:8, :] = gates[0:8]
    mt_ref[8:16, :] = jnp.where(row8 < 4, gates[8:16], csum)

    g_pool = seg(_P_POOL, 256)
    cv = seg(_P_CONV, 768)

    pext[POOL_HALO:POOL_HALO + tm, :] = g_pool

    def pld(k, c):
        return pext[pl.ds(POOL_HALO - k, tm), c * 128:(c + 1) * 128]

    left = lane < 64
    rowp1 = (row + 1).astype(F32)

    def cnt(w):
        return jnp.minimum(rowp1, float(w))

    e0 = pld(0, 0)
    s2 = e0 + pld(1, 0)
    s4 = s2 + pld(2, 0) + pld(3, 0)
    p0 = jnp.where(left, s2 / cnt(2), s4 / cnt(4)) - e0
    f0 = pld(0, 1)
    s8 = f0
    for k in range(1, 8):
        s8 = s8 + pld(k, 1)
    s16 = s8
    for k in range(8, 16):
        s16 = s16 + pld(k, 1)
    p1 = jnp.where(left, s8 / cnt(8), s16 / cnt(16)) - f0
    pw = pw_ref[...]
    mixed = _dot(p0.astype(BF16), pw[0:128, :]) + _dot(p1.astype(BF16), pw[128:256, :])
    opool_ref[...] = mixed * ps_ref[...]
    pext[0:POOL_HALO, :] = pext[tm:tm + POOL_HALO, :]

    g_fv = seg(_P_FOXV, 256)
    g_fk = seg(_P_FOXK, 512)
    g_fq = seg(_P_FOXQ, 256)

    q_ref[:, 0:128] = (rope(g_rope[:, 0:128]) * Q_SCALE).astype(BF16)
    q_ref[:, 128:256] = (rope(g_rope[:, 128:256]) * Q_SCALE).astype(BF16)
    block_onehot = jnp.where(lane - HEAD_DIM == row // SEL_LEN, 1.0, 0.0)
    ks_ref[...] = (rope(g_rope[:, 256:384]) + block_onehot).astype(BF16)
    kw_ref[...] = rope(g_rope[:, 384:512]).astype(BF16)
    vsw_t = jnp.transpose(g_small[:, 0:128]).astype(BF16)
    ones_upd = jnp.ones((V_ROWS - HEAD_DIM, UPD), BF16)
    for c in range(tm // UPD):
        vst_ref[c] = jnp.concatenate([vsw_t[0:HEAD_DIM, c * UPD:(c + 1) * UPD], ones_upd], axis=0)
    for c in range(tm // CHUNK):
        vwt_ref[c] = jnp.concatenate([vsw_t[HEAD_DIM:128, c * CHUNK:(c + 1) * CHUNK], ones_upd[:, 0:CHUNK]], axis=0)
    kvc_ref[...] = g_small[:, 128:256]

    fq_ref[...] = (g_fq * Q_SCALE).astype(BF16)
    terms = split3(csum * (-LOG2E))
    row16 = lax.broadcasted_iota(jnp.int32, (16, tm), 0)
    extra_t = jnp.zeros((16, tm), F32)
    for hd in range(N_HEADS):
        for j in range(3):
            src = jnp.broadcast_to(terms[j][4 + hd:5 + hd, :], (16, tm))
            extra_t = jnp.where(row16 == 3 * hd + j, src, extra_t)
    extra = jnp.transpose(jnp.concatenate(
        [jnp.zeros((HEAD_DIM, tm), F32), extra_t, jnp.zeros((128 - HEAD_DIM - 16, tm), F32)], axis=0))
    for hd in range(N_HEADS):
        fk_ref[:, hd * 128:(hd + 1) * 128] = (g_fk[:, hd * 128:(hd + 1) * 128] + extra).astype(BF16)
    fv_t = jnp.transpose(g_fv).astype(BF16)
    for c in range(tm // UPD):
        cols = slice(c * UPD, (c + 1) * UPD)
        fvt_ref[c] = jnp.concatenate(
            [piece for hd in range(N_HEADS) for piece in (fv_t[hd * HEAD_DIM:(hd + 1) * HEAD_DIM, cols], ones_upd)],
            axis=0)

    u = cv[:, 512:768] * cv[:, 0:256]
    cext[CONV_HALO:CONV_HALO + tm, :] = u
    y = (cext[pl.ds(CONV_HALO - 2, tm), :] * cw_ref[0:1, :]
         + cext[pl.ds(CONV_HALO - 1, tm), :] * cw_ref[1:2, :]
         + u * cw_ref[2:3, :])
    oconv_ref[...] = cv[:, 256:512] * y
    cext[0:CONV_HALO, :] = cext[tm:tm + CONV_HALO, :]

    hbuf[...] = _rmsnorm(xn_ref[...], g_ref[...]).astype(BF16)


def _proj_call(x, g, w1, cc, ss, fb, cw, pw, ps, tri):
    B, T, _ = x.shape
    tm = TM_PROJ
    nt = T // tm
    tok = lambda width: pl.BlockSpec((None, tm, width), lambda b, t: (b, t, 0))
    chunked = lambda rows, keys: pl.BlockSpec((None, tm // keys, rows, keys), lambda b, t: (b, t, 0, 0))
    const = lambda shape: pl.BlockSpec(shape, lambda b, t: (0,) * len(shape))
    out_shape = [
        jax.ShapeDtypeStruct((B, T, 256), BF16),
        jax.ShapeDtypeStruct((B, T, 128), BF16),
        jax.ShapeDtypeStruct((B, T, 128), BF16),
        jax.ShapeDtypeStruct((B, T // UPD, V_ROWS, UPD), BF16),
        jax.ShapeDtypeStruct((B, T // CHUNK, V_ROWS, CHUNK), BF16),
        jax.ShapeDtypeStruct((B, T, 128), F32),
        jax.ShapeDtypeStruct((B, nt, N_MISC_T, tm), F32),
        jax.ShapeDtypeStruct((B, T, 256), BF16),
        jax.ShapeDtypeStruct((B, T, 512), BF16),
        jax.ShapeDtypeStruct((B, T // UPD, N_HEADS * V_ROWS, UPD), BF16),
        jax.ShapeDtypeStruct((B, T, 256), F32),
        jax.ShapeDtypeStruct((B, T, 256), F32),
    ]
    out_specs = [tok(256), tok(128), tok(128), chunked(V_ROWS, UPD), chunked(V_ROWS, CHUNK), tok(128),
                 pl.BlockSpec((None, None, N_MISC_T, tm), lambda b, t: (b, t, 0, 0)),
                 tok(256), tok(512), chunked(N_HEADS * V_ROWS, UPD), tok(256), tok(256)]
    def next_tile(b, t):
        lin = jnp.minimum(b * nt + t + 1, B * nt - 1)
        return lin // nt, lin % nt, 0

    in_specs = [tok(D_MODEL), pl.BlockSpec((None, tm, D_MODEL), next_tile),
                const((1, D_MODEL)), const((D_MODEL, _P_COLS)),
                pl.BlockSpec((tm, 128), lambda b, t: (t, 0)), pl.BlockSpec((tm, 128), lambda b, t: (t, 0)),
                const((8, 1)), const((3, 256)), const((256, 256)), const((1, 256)), const((tm, tm))]
    return pl.pallas_call(
        functools.partial(_proj_kernel, tm=tm),
        grid=(B, nt),
        in_specs=in_specs,
        out_specs=out_specs,
        out_shape=out_shape,
        scratch_shapes=[pltpu.VMEM((tm + POOL_HALO, 256), F32), pltpu.VMEM((tm + CONV_HALO, 256), F32),
                        pltpu.VMEM((8, 128), F32), pltpu.VMEM((tm, D_MODEL), BF16)],
        compiler_params=pltpu.CompilerParams(dimension_semantics=("arbitrary", "arbitrary"),
                                             vmem_limit_bytes=VMEM_LIMIT),
        name="proj",
    )(x, x, g, w1, cc, ss, fb, cw, pw, ps, tri)


def _compress_kernel(x_ref, pt_ref, pb_ref, wt_ref, wb_ref, b1_ref, w2_ref, b2_ref, cc_ref, ss_ref,
                     kc_ref, vct_ref, *, rows):
    a = b = None
    for l in range(CMP_STRIDE):
        x_l = x_ref[pl.ds(l, rows, stride=CMP_STRIDE), :]
        a_l = _dot((x_l + pt_ref[l:l + 1, :]).astype(BF16), wt_ref[l])
        b_l = _dot((x_l + pb_ref[l:l + 1, :]).astype(BF16), wb_ref[l])
        a, b = (a_l, b_l) if a is None else (a + a_l, b + b_l)
    hid = a + pltpu.roll(b, rows - 1, 0) + b1_ref[...]
    act = hid * _sigmoid(hid)
    out = _dot(act.astype(BF16), w2_ref[...]) + b2_ref[...]
    lane = lax.broadcasted_iota(jnp.int32, (rows, 128), 1)
    roped = out[:, 0:128] * cc_ref[...] + out[:, 128:256] * ss_ref[...]
    block = lax.broadcasted_iota(jnp.int32, (rows, 128), 0)
    group_onehot = jnp.where(lane - HEAD_DIM == block // CMP_GROUP, 1.0, 0.0)
    kc_ref[...] = jnp.where(lane < HEAD_DIM, roped, group_onehot).astype(BF16)
    vct_ref[...] = jnp.transpose(out[:, 256:384])[0:HEAD_DIM, :].astype(BF16)


def _compress_call(kvc, pt, pb, wt, wb, b1, w2, b2, cc, ss):
    B, T, width = kvc.shape
    rows = T // CMP_STRIDE
    const = lambda shape: pl.BlockSpec(shape, lambda b: (0,) * len(shape))
    return pl.pallas_call(
        functools.partial(_compress_kernel, rows=rows),
        grid=(B,),
        in_specs=[pl.BlockSpec((None, T, width), lambda b: (b, 0, 0)),
                  const((CMP_STRIDE, width)), const((CMP_STRIDE, width)),
                  const((CMP_STRIDE, width, 256)), const((CMP_STRIDE, width, 256)),
                  const((1, 256)), const((256, 384)), const((1, 384)), const((rows, 128)), const((rows, 128))],
        out_specs=[pl.BlockSpec((None, rows, 128), lambda b: (b, 0, 0)),
                   pl.BlockSpec((None, HEAD_DIM, rows), lambda b: (b, 0, 0))],
        out_shape=[jax.ShapeDtypeStruct((B, rows, 128), BF16), jax.ShapeDtypeStruct((B, HEAD_DIM, rows), BF16)],
        compiler_params=pltpu.CompilerParams(dimension_semantics=("arbitrary",),
                                             vmem_limit_bytes=VMEM_LIMIT),
        name="compress",
    )(kvc, pt, pb, wt, wb, b1, w2, b2, cc, ss)


def _nsa_kernel(q_ref, ks_ref, kw_ref, vst_ref, vwt_ref, kc_ref, vct_ref, mt_ref, ovt_ref,
                o_ref, key_ref, sbuf, cbuf, *, nblk, ncmp):
    step = pl.program_id(1)
    tq = CHUNK
    nq = N_HEADS * tq
    hw = nq // 2
    halves = (slice(0, hw), slice(hw, nq))
    n_c = KV_TILE // UPD
    n_win = WINDOW // tq
    subs = range(NSA_SUB)
    qis = [step * NSA_SUB + u for u in subs]
    q_local = lax.broadcasted_iota(jnp.int32, (1, nq), 1) % tq
    k_local = lax.broadcasted_iota(jnp.int32, (tq, hw), 0)
    q_loc = q_local[:, 0:hw]
    causal = k_local <= q_loc

    q4, qt_plain = [], []
    group = lax.broadcasted_iota(jnp.int32, (HEAD_DIM, nq), 0)
    slab = 2 * CMP_GROUP
    n_slab = lax.broadcasted_iota(jnp.int32, (slab, nq), 0)
    for u in subs:
        q_t = jnp.transpose(q_ref[u * tq:(u + 1) * tq, :].astype(F32))
        q4.append(jnp.concatenate([q_t[HEAD_DIM * h:HEAD_DIM * (h + 1)] for h in range(N_HEADS)], axis=1))
        qt_plain.append(jnp.concatenate([q4[u], jnp.zeros((HEAD_DIM, nq), F32)], axis=0).astype(BF16))
        group_bias = jnp.where(group > qis[u], MASK_VALUE, 0.0)
        qt_cmp = jnp.concatenate([q4[u], group_bias], axis=0).astype(BF16)
        cbuf[u] = _dot(kc_ref[...], qt_cmp)
        start = pl.multiple_of(jnp.maximum(qis[u] - 1, 0) * CMP_GROUP, CMP_GROUP)
        visible = ((start + n_slab) * CMP_STRIDE + (CMP_LEN - 1)) <= qis[u] * tq + q_local
        cbuf[u, pl.ds(start, slab), :] = jnp.where(visible, cbuf[u, pl.ds(start, slab), :], MASK_VALUE)

    o_cmp, imp = [], []
    for u in subs:
        s = cbuf[u]
        mc = jnp.max(s, axis=0, keepdims=True)
        mc = jnp.where(mc <= MASK_VALUE, 0.0, mc)
        pc = jnp.exp2(s - mc)
        lc = jnp.sum(pc, axis=0, keepdims=True)
        pc = pc * (1.0 / jnp.where(lc > 0.0, lc, 1.0))
        o_cmp.append(_dot(vct_ref[...], pc.astype(BF16)))
        psum = (pc[:, 0:tq] + pc[:, tq:2 * tq]) + (pc[:, 2 * tq:3 * tq] + pc[:, 3 * tq:4 * tq])
        imp.append(_dot(ovt_ref[...], psum, precision=lax.Precision.HIGHEST))

    def win_stream(u, hs):
        qi = qis[u]

        def score(c):
            if c == 0:
                own = kw_ref[pl.ds(pl.multiple_of(qi * tq, tq), tq), :]
                return jnp.where(causal, _dot(own, qt_plain[u][:, hs]), MASK_VALUE)
            ci = qi - n_win + (c - 1)
            start = pl.multiple_of(jnp.maximum(ci, 0) * tq, tq)
            s = _dot(kw_ref[pl.ds(start, tq), :], qt_plain[u][:, hs])
            ok = ((k_local > q_loc) & (ci >= 0)) if c == 1 else jnp.broadcast_to(ci >= 0, (tq, hw))
            return jnp.where(ok, s, MASK_VALUE)

        def value(c):
            return vwt_ref[qi] if c == 0 else vwt_ref[jnp.maximum(qi - n_win + (c - 1), 0)]

        return score, value, _flash_init(hw)

    carries_w = _attend_chunks([win_stream(u, hs) for u in subs for hs in halves], n_win + 1)
    o_win = [jnp.concatenate([acc / l for _, l, acc in carries_w[2 * u:2 * u + 2]], axis=1) for u in subs]

    jb = lax.broadcasted_iota(jnp.int32, (nblk, tq), 0)
    sub8 = lax.broadcasted_iota(jnp.int32, (8, tq), 0)
    later_rows = [sub8 > r for r in range(8)]
    qt_sel = []
    for u in subs:
        tt = qis[u] * tq + lax.broadcasted_iota(jnp.int32, (nblk, tq), 1)
        jt = tt // SEL_LEN
        forced = (jb == 0) | (jb == jt) | (jb == jt - 1)
        valid = jb * SEL_LEN <= tt
        key = jnp.where(valid, jnp.where(forced, imp[u] + FORCE_BONUS, imp[u]), MASK_VALUE)
        key_ref[u] = key
        key_rows = [key[8 * v:8 * v + 8] for v in range(nblk // 8)]
        ranks = [jnp.zeros((8, tq), F32) for _ in key_rows]
        for jp in range(nblk):
            other = jnp.broadcast_to(key_ref[u, pl.ds(jp, 1), :], (8, tq))
            for v, kv in enumerate(key_rows):
                if 8 * v > jp:
                    ahead = other >= kv
                elif 8 * v + 7 <= jp:
                    ahead = other > kv
                else:
                    ahead = (other > kv) | ((other == kv) & later_rows[jp - 8 * v])
                ranks[v] = ranks[v] + jnp.where(ahead, 1.0, 0.0)
        selected = jnp.concatenate(ranks, axis=0) < float(min(N_SEL, nblk))
        sel_bias = jnp.where(selected, 0.0, MASK_VALUE)
        if nblk < HEAD_DIM:
            sel_bias = jnp.concatenate([sel_bias, jnp.zeros((HEAD_DIM - nblk, tq), F32)], axis=0)
        qt_sel.append(jnp.concatenate([q4[u], jnp.concatenate([sel_bias] * N_HEADS, axis=1)], axis=0).astype(BF16))

    def slc_scores(u, kt, c):
        start = pl.multiple_of((kt * n_c + c) * UPD, UPD)
        return _dot(ks_ref[pl.ds(start, UPD), :], qt_sel[u])

    def slc_update(u, carry, s, kt, c):
        return tuple(_flash_t(s[:, hs], *carry[i], vst_ref[kt * n_c + c]) for i, hs in enumerate(halves))

    for c in range(n_c):
        for u in subs:
            sbuf[u, c] = slc_scores(u, 0, c)

    def slc_tile(kt, carries):
        carries = list(carries)
        for c in range(n_c):
            for u in subs:
                s = sbuf[u, c]
                sbuf[u, c] = slc_scores(u, kt + 1, c)
                carries[u] = slc_update(u, carries[u], s, kt, c)
        return tuple(carries)

    carries = list(_run_tiles(slc_tile, step, tuple((_flash_init(hw), _flash_init(hw)) for u in subs)))
    k_upd = lax.broadcasted_iota(jnp.int32, (UPD, nq), 0)
    for c in range(n_c):
        first, last = c * UPD // tq, (c + 1) * UPD // tq - 1
        for u in subs:
            if first > u:
                continue
            s = sbuf[u, c]
            if last >= u:
                s = jnp.where(c * UPD + k_upd <= u * tq + q_local, s, MASK_VALUE)
            carries[u] = slc_update(u, carries[u], s, step, c)

    for u in subs:
        o_slc = jnp.concatenate([acc / l for _, l, acc in carries[u]], axis=1)
        g = mt_ref[:, u * tq:(u + 1) * tq]
        heads = []
        for h in range(N_HEADS):
            cols = slice(h * tq, (h + 1) * tq)
            heads.append(g[h:h + 1] * o_cmp[u][:, cols] + g[4 + h:5 + h] * o_slc[:, cols]
                         + g[8 + h:9 + h] * o_win[u][:, cols])
        o_ref[u * tq:(u + 1) * tq, :] = jnp.transpose(jnp.concatenate(heads, axis=0))


def _nsa_call(q, ks, kw, vst, vwt, kc, vct, mt, ovt):
    B, T, _ = q.shape
    nblk, ncmp = ovt.shape
    tq = CHUNK * NSA_SUB
    per_mt = TM_PROJ // tq
    full = lambda width: pl.BlockSpec((None, T, width), lambda b, i: (b, 0, 0))
    vfull = lambda keys: pl.BlockSpec((None, T // keys, V_ROWS, keys), lambda b, i: (b, 0, 0, 0))
    return pl.pallas_call(
        functools.partial(_nsa_kernel, nblk=nblk, ncmp=ncmp),
        grid=(B, T // tq),
        in_specs=[pl.BlockSpec((None, tq, 256), lambda b, i: (b, i, 0)),
                  full(128), full(128), vfull(UPD), vfull(CHUNK),
                  pl.BlockSpec((None, ncmp, 128), lambda b, i: (b, 0, 0)),
                  pl.BlockSpec((None, HEAD_DIM, ncmp), lambda b, i: (b, 0, 0)),
                  pl.BlockSpec((None, None, N_MISC_T, tq), lambda b, i: (b, i // per_mt, 0, i % per_mt)),
                  pl.BlockSpec((nblk, ncmp), lambda b, i: (0, 0))],
        out_specs=pl.BlockSpec((None, tq, 256), lambda b, i: (b, i, 0)),
        out_shape=jax.ShapeDtypeStruct((B, T, 256), F32),
        scratch_shapes=[pltpu.VMEM((NSA_SUB, nblk, CHUNK), F32),
                        pltpu.VMEM((NSA_SUB, KV_TILE // UPD, UPD, N_HEADS * CHUNK), F32),
                        pltpu.VMEM((NSA_SUB, ncmp, N_HEADS * CHUNK), F32)],
        compiler_params=pltpu.CompilerParams(dimension_semantics=("arbitrary", "arbitrary"),
                                             vmem_limit_bytes=VMEM_LIMIT),
        name="nsa",
    )(q, ks, kw, vst, vwt, kc, vct, mt, ovt)


def _fox_kernel(q_ref, k_ref, vt_ref, o_ref, sbuf):
    qi = pl.program_id(1)
    nq = TQ_FOX
    n_chunk = KV_TILE // UPD
    t_lane = qi * nq + lax.broadcasted_iota(jnp.int32, (1, nq), 1)
    k_local = lax.broadcasted_iota(jnp.int32, (UPD, nq), 0)
    q_t = jnp.transpose(q_ref[...].astype(F32))
    extra_row = lax.broadcasted_iota(jnp.int32, (HEAD_DIM, nq), 0)

    def ones_rows(h):
        return jnp.where((extra_row >= 3 * h) & (extra_row < 3 * h + 3), 1.0, 0.0)

    qts = [jnp.concatenate([q_t[HEAD_DIM * h:HEAD_DIM * (h + 1)], ones_rows(h)], axis=0).astype(BF16)
           for h in range(N_HEADS)]

    def scores(kt, c, h):
        start = pl.multiple_of((kt * n_chunk + c) * UPD, UPD)
        return _dot(k_ref[pl.ds(start, UPD), h * 128:(h + 1) * 128], qts[h])

    def value(kt, c, h):
        return vt_ref[kt * n_chunk + c, h * V_ROWS:(h + 1) * V_ROWS, :]

    for c in range(n_chunk):
        for h in range(N_HEADS):
            sbuf[h, c] = scores(0, c, h)

    def tile(kt, carries):
        carries = list(carries)
        for c in range(n_chunk):
            for h in range(N_HEADS):
                s = sbuf[h, c]
                sbuf[h, c] = scores(kt + 1, c, h)
                carries[h] = _flash_t(s, *carries[h], value(kt, c, h))
        return tuple(carries)

    carries = list(_run_tiles(tile, qi, tuple(_flash_init(nq) for _ in range(N_HEADS))))
    for c in range(n_chunk):
        causal = qi * nq + c * UPD + k_local <= t_lane
        for h in range(N_HEADS):
            carries[h] = _flash_t(jnp.where(causal, sbuf[h, c], MASK_VALUE), *carries[h], value(qi, c, h))
    o_ref[...] = jnp.transpose(jnp.concatenate([acc / l for _, l, acc in carries], axis=0))


def _fox_call(fq, fk, fvt):
    B, T, _ = fq.shape
    tq = TQ_FOX
    return pl.pallas_call(
        _fox_kernel,
        grid=(B, T // tq),
        in_specs=[pl.BlockSpec((None, tq, 256), lambda b, i: (b, i, 0)),
                  pl.BlockSpec((None, T, 512), lambda b, i: (b, 0, 0)),
                  pl.BlockSpec((None, T // UPD, N_HEADS * V_ROWS, UPD), lambda b, i: (b, 0, 0, 0))],
        out_specs=pl.BlockSpec((None, tq, 256), lambda b, i: (b, i, 0)),
        out_shape=jax.ShapeDtypeStruct((B, T, 256), F32),
        scratch_shapes=[pltpu.VMEM((N_HEADS, KV_TILE // UPD, UPD, tq), F32)],
        compiler_params=pltpu.CompilerParams(dimension_semantics=("arbitrary", "arbitrary"),
                                             vmem_limit_bytes=VMEM_LIMIT),
        name="fox",
    )(fq, fk, fvt)


def _merge_kernel(x_ref, g_ref, wg_ref, wm_ref, wb_ref, wo_ref, fg_ref, o0_ref, o1_ref, o2_ref, o3_ref,
                  out_ref, *, last, tm):
    rows = tm // 2
    halves = [slice(u * rows, (u + 1) * rows) for u in range(2)]
    xs = [x_ref[r, :] for r in halves]
    hs = [_rmsnorm(x, g_ref[...]).astype(BF16) for x in xs]
    accs = []
    for u, r in enumerate(halves):
        gates = _dot(hs[u], wg_ref[...])
        acc = None
        for i, o_ref in enumerate((o0_ref, o1_ref, o2_ref, o3_ref)):
            merge = _dot(hs[u], wm_ref[:, i * D_MODEL:(i + 1) * D_MODEL])
            gate = gates[:, i * BRANCH_WIDTH:(i + 1) * BRANCH_WIDTH]
            a = (o_ref[r, :] * (gate * _sigmoid(gate))).astype(BF16)
            term = _sigmoid(merge) * _dot(a, wb_ref[i])
            acc = term if acc is None else acc + term
        accs.append(acc)
    for u, r in enumerate(halves):
        y = xs[u] + _dot(accs[u].astype(BF16), wo_ref[...])
        if last:
            y = _rmsnorm(y, fg_ref[...])
        out_ref[r, :] = y


def _merge_call(x, g, wg, wm, wb, wo, fg, o_nsa, o_pool, o_conv, o_fox, last):
    B, T, _ = x.shape
    tm = TM_MERGE
    tok = lambda width: pl.BlockSpec((None, tm, width), lambda b, t: (b, t, 0))
    const = lambda shape: pl.BlockSpec(shape, lambda b, t: (0,) * len(shape))
    return pl.pallas_call(
        functools.partial(_merge_kernel, last=last, tm=tm),
        grid=(B, T // tm),
        in_specs=[tok(D_MODEL), const((1, D_MODEL)), const((D_MODEL, N_BRANCH * BRANCH_WIDTH)),
                  const((D_MODEL, N_BRANCH * D_MODEL)), const((N_BRANCH, BRANCH_WIDTH, D_MODEL)),
                  const((D_MODEL, D_MODEL)), const((1, D_MODEL)), tok(256), tok(256), tok(256), tok(256)],
        out_specs=tok(D_MODEL),
        out_shape=jax.ShapeDtypeStruct((B, T, D_MODEL), F32),
        compiler_params=pltpu.CompilerParams(dimension_semantics=("arbitrary", "arbitrary"),
                                             vmem_limit_bytes=VMEM_LIMIT),
        name="merge",
    )(x, g, wg, wm, wb, wo, fg, o_nsa, o_pool, o_conv, o_fox)


def _rope_tables(pos):
    n = pos.shape[0]
    inv = ROPE_THETA ** (-jnp.arange(ROPE_HALF, dtype=F32) / ROPE_HALF)
    ang = pos.astype(F32)[:, None] * inv[None, :]
    cos, sin = jnp.cos(ang), jnp.sin(ang)
    rest = HEAD_DIM - 2 * ROPE_HALF
    c = jnp.concatenate([cos, cos, jnp.ones((n, rest), F32)], axis=1)
    s = jnp.concatenate([-sin, sin, jnp.zeros((n, rest), F32)], axis=1)
    return jnp.concatenate([c, c], axis=1), jnp.concatenate([s, s], axis=1)


def _proj_weight(w_in):
    col = lambda name, width: w_in[:, _OFF[name]:_OFF[name] + width]
    z64 = jnp.zeros((D_MODEL, HEAD_DIM), w_in.dtype)
    misc = jnp.concatenate([col("nsa_g", 12), col("fox_f", 4), jnp.zeros((D_MODEL, 112), w_in.dtype)], axis=1)
    fox_k = []
    for h in range(N_HEADS):
        fox_k += [w_in[:, _OFF["fox_k"] + h * HEAD_DIM:_OFF["fox_k"] + (h + 1) * HEAD_DIM], z64]
    w = jnp.concatenate([col("nsa_q", 256), col("k_s", 64), z64, col("k_w", 64), z64,
                         col("v_s", 64), col("v_w", 64), col("k_c", 128), misc,
                         col("fox_q", 256)] + fox_k + [col("fox_v", 256), col("pool", 256), col("conv", 768)],
                        axis=1)
    assert w.shape[1] == _P_COLS and w.dtype == BF16
    return w


def _compress_weights(cmp_pos, cmp_w1, cmp_b1, cmp_w2, cmp_b2):
    half = CMP_LEN // 2

    def w1_half(lo):
        wk = cmp_w1[0, lo * HEAD_DIM:(lo + half) * HEAD_DIM].reshape(half, HEAD_DIM, CMP_HIDDEN)
        wv = cmp_w1[1, lo * HEAD_DIM:(lo + half) * HEAD_DIM].reshape(half, HEAD_DIM, CMP_HIDDEN)
        z = jnp.zeros_like(wk)
        top = jnp.concatenate([wk, z], axis=2)
        bot = jnp.concatenate([z, wv], axis=2)
        return jnp.concatenate([top, bot], axis=1).astype(BF16)

    def pos_half(lo):
        return jnp.concatenate([cmp_pos[0, lo:lo + half], cmp_pos[1, lo:lo + half]], axis=1)

    lane = np.arange(HEAD_DIM)
    perm = np.where(lane < ROPE_HALF, lane + ROPE_HALF, np.where(lane < 2 * ROPE_HALF, lane - ROPE_HALF, lane))
    w2k, w2v = cmp_w2[0], cmp_w2[1]
    zk = jnp.zeros_like(w2k)
    w2 = jnp.concatenate([
        jnp.concatenate([w2k, w2k, w2k[:, perm], w2k[:, perm], zk, zk], axis=1),
        jnp.concatenate([zk, zk, zk, zk, w2v, w2v], axis=1)], axis=0).astype(BF16)
    b2k, b2v = cmp_b2[0], cmp_b2[1]
    b2 = jnp.concatenate([b2k, b2k, b2k[perm], b2k[perm], b2v, b2v])[None, :]
    b1 = jnp.concatenate([cmp_b1[0], cmp_b1[1]])[None, :]
    return pos_half(0), pos_half(half), w1_half(0), w1_half(half), b1, w2, b2


def _block_diag(pool_w):
    n, c, _ = pool_w.shape
    out = jnp.zeros((n * c, n * c), pool_w.dtype)
    for i in range(n):
        out = out.at[i * c:(i + 1) * c, i * c:(i + 1) * c].set(pool_w[i])
    return out


def kernel(x, norm_g, w_in, fox_f_bias, cmp_pos, cmp_w1, cmp_b1, cmp_w2, cmp_b2, pool_w, pool_scale, conv_w,
           w_branch, w_out, final_norm_g):
    B, T, _ = x.shape
    depth = norm_g.shape[0]
    assert T % TM_PROJ == 0 and T % KV_TILE == 0 and T // SEL_LEN <= HEAD_DIM and T >= WINDOW + CHUNK
    assert NSA_SUB * CHUNK == KV_TILE == TQ_FOX and TM_PROJ % KV_TILE == 0
    rows = T // CMP_STRIDE
    nblk = T // SEL_LEN

    cc_tok, ss_tok = _rope_tables(jnp.arange(T))
    cmp_end = jnp.arange(rows) * CMP_STRIDE + CMP_LEN - 1
    cc_cmp, ss_cmp = _rope_tables(cmp_end)
    ci = np.arange(rows)[:, None] * CMP_STRIDE
    sj = np.arange(nblk)[None, :] * SEL_LEN
    overlap_t = jnp.asarray(((ci < sj + SEL_LEN) & (ci + CMP_LEN > sj)).T, F32)
    tri = jnp.asarray(np.triu(np.ones((TM_PROJ, TM_PROJ), np.float32)), BF16)

    w_in_bf = w_in.astype(BF16)
    for l in range(depth):
        w1 = _proj_weight(w_in_bf[l])
        fb = jnp.zeros((8, 1), F32).at[4:8, 0].set(fox_f_bias[l])
        (q, ks, kw, vst, vwt, kvc, mt, fq, fk, fvt, o_pool, o_conv) = _proj_call(
            x, norm_g[l][None, :], w1, cc_tok, ss_tok, fb, conv_w[l], _block_diag(pool_w[l]).astype(BF16),
            pool_scale[l][None, :], tri)
        kc, vct = _compress_call(kvc, *_compress_weights(cmp_pos[l], cmp_w1[l], cmp_b1[l], cmp_w2[l], cmp_b2[l]),
                                 cc_cmp, ss_cmp)
        o_nsa = _nsa_call(q, ks, kw, vst, vwt, kc, vct, mt, overlap_t)
        o_fox = _fox_call(fq, fk, fvt)
        wg = w_in_bf[l][:, _OFF["gate"]:_OFF["merge"]]
        wm = w_in_bf[l][:, _OFF["merge"]:_OFF["end"]]
        x = _merge_call(x, norm_g[l][None, :], wg, wm, w_branch[l].astype(BF16), w_out[l].astype(BF16),
                        final_norm_g[None, :], o_nsa, o_pool, o_conv, o_fox, last=(l == depth - 1))
    return x
```

```python
import functools

import numpy as np
import jax
import jax.numpy as jnp
from jax import lax
from jax.experimental import pallas as pl
from jax.experimental.pallas import tpu as pltpu

F32 = jnp.float32
BF16 = jnp.bfloat16

D_MODEL = 1024
N_BRANCH = 4
BRANCH_WIDTH = 256
HEAD_DIM = 64
N_HEADS = 4
CMP_LEN = 32
CMP_STRIDE = 16
CMP_HIDDEN = 128
CMP_GROUP = 8
SEL_LEN = 64
N_SEL = 16
WINDOW = 512
FORCE_BONUS = 1.0e4
ROPE_THETA = 500000.0
ROPE_HALF = 8
NORM_EPS = 1e-6
MASK_VALUE = -1e30

_OFF = dict(nsa_q=0, k_c=256, v_c=320, k_s=384, v_s=448, k_w=512, v_w=576, nsa_g=640, pool=652,
            conv=908, fox_q=1676, fox_k=1932, fox_v=2188, fox_f=2444, gate=2448, merge=3472, end=7568)

_P_ROPE = 0
_P_VSW = 512
_P_FOXQ = 896
_P_FOXK = 1152
_P_FOXV = 1664
_P_POOL = 1920
_P_CONV = 2176
_P_COLS = 2944

CHUNK = 128
UPD = 256
V_ROWS = HEAD_DIM + 16
LOG2E = 1.4426950408889634
Q_SCALE = HEAD_DIM ** -0.5 * LOG2E
TM_PROJ = 512
KV_TILE = 512
NSA_SUB = 4
TQ_FOX = 512
TM_MERGE = 512
N_MISC_T = 16
POOL_HALO = 16
CONV_HALO = 8
VMEM_LIMIT = 56 * 1024 * 1024


def _dot(a, b, precision=None):
    return jnp.dot(a, b, preferred_element_type=F32, precision=precision)


def _sigmoid(x):
    return 1.0 / (1.0 + jnp.exp(-x))


def _rmsnorm(x, g):
    return x * lax.rsqrt(jnp.mean(x * x, axis=-1, keepdims=True) + NORM_EPS) * g


def _flash_t(s, m, l, acc, v_t, s_max=None):
    m_new = jnp.maximum(m, jnp.max(s, axis=0, keepdims=True) if s_max is None else s_max)
    alpha = jnp.exp2(m - m_new)
    pv = _dot(v_t, jnp.exp2(s - m_new).astype(BF16))
    l = alpha * l + pv[HEAD_DIM:HEAD_DIM + 1]
    acc = alpha * acc + pv[0:HEAD_DIM]
    return m_new, l, acc


def _attend_chunks(streams, n):
    nxt = [st[0](0) for st in streams]
    carries = [st[2] for st in streams]
    for c in range(n):
        cur = nxt
        if c + 1 < n:
            nxt = [st[0](c + 1) for st in streams]
        carries = [_flash_t(cur[i], *carries[i], streams[i][1](c)) for i in range(len(streams))]
    return carries


def _run_tiles(tile, n, carries):
    def run(first, count, carries):
        for j in range(count):
            carries = tile(first + j, carries)
        return carries

    quads = n // 4
    carries = lax.fori_loop(0, quads, lambda i, cs: run(4 * i, 4, cs), carries)
    pairs = (n - 4 * quads) // 2
    carries = lax.fori_loop(0, pairs, lambda i, cs: run(4 * quads, 2, cs), carries)
    return lax.fori_loop(4 * quads + 2 * pairs, n, tile, carries)


def _flash_init(n):
    return (jnp.full((1, n), MASK_VALUE, F32), jnp.zeros((1, n), F32), jnp.zeros((HEAD_DIM, n), F32))


def _proj_kernel(x_ref, xn_ref, g_ref, w_ref, cc_ref, ss_ref, fb_ref, cw_ref, pw_ref, ps_ref, tri_ref,
                 q_ref, ks_ref, kw_ref, vst_ref, vwt_ref, kvc_ref, mt_ref, fq_ref, fk_ref, fvt_ref,
                 opool_ref, oconv_ref, pext, cext, ccarry, hbuf, *, tm):
    ti = pl.program_id(1)

    @pl.when((pl.program_id(0) == 0) & (ti == 0))
    def _():
        hbuf[...] = _rmsnorm(x_ref[...], g_ref[...]).astype(BF16)

    h = hbuf[...]

    def seg(lo, width):
        return _dot(h, w_ref[:, lo:lo + width])

    lane = lax.broadcasted_iota(jnp.int32, (tm, 128), 1)
    row = lax.broadcasted_iota(jnp.int32, (tm, 128), 0) + ti * tm
    first_half = (lane % HEAD_DIM) < ROPE_HALF
    cc = cc_ref[...]
    ss = ss_ref[...]

    def rope(xc):
        partner = jnp.where(first_half, pltpu.roll(xc, 128 - ROPE_HALF, 1), pltpu.roll(xc, ROPE_HALF, 1))
        return xc * cc + partner * ss

    @pl.when(ti == 0)
    def _():
        ccarry[...] = jnp.zeros_like(ccarry)
        pext[0:POOL_HALO, :] = jnp.zeros((POOL_HALO, BRANCH_WIDTH), F32)
        cext[0:CONV_HALO, :] = jnp.zeros((CONV_HALO, BRANCH_WIDTH), F32)

    g_small = seg(_P_VSW, 384)
    g_rope = seg(_P_ROPE, 512)

    z_t = jnp.transpose(g_small[:, 256:384])[0:N_MISC_T, :]
    gates = _sigmoid(z_t)
    row8 = lax.broadcasted_iota(jnp.int32, (8, tm), 0)
    zb = z_t[8:16] + fb_ref[...]
    logf = jnp.where(row8 >= 4, jnp.minimum(zb, 0.0) - jnp.log(1.0 + jnp.exp(-jnp.abs(zb))), 0.0)

    def split3(v):
        hi = v.astype(BF16).astype(F32)
        mid = (v - hi).astype(BF16).astype(F32)
        return hi, mid, (v - hi) - mid

    parts = jnp.concatenate(list(split3(logf)) + [jnp.zeros((8, tm), F32)], axis=0).astype(BF16)
    part_sums = _dot(parts, tri_ref[...])
    csum = (part_sums[0:8] + part_sums[8:16]) + part_sums[16:24] + ccarry[:, 0:1]
    ccarry[...] = jnp.broadcast_to(csum[:, tm - 1:tm], (8, 128))
    mt_ref[0:8, :] = gates[0:8]
    mt_ref[8:16, :] = jnp.where(row8 < 4, gates[8:16], csum)

    g_pool = seg(_P_POOL, 256)
    cv = seg(_P_CONV, 768)

    pext[POOL_HALO:POOL_HALO + tm, :] = g_pool

    def pld(k, c):
        return pext[pl.ds(POOL_HALO - k, tm), c * 128:(c + 1) * 128]

    left = lane < 64
    rowp1 = (row + 1).astype(F32)

    def cnt(w):
        return jnp.minimum(rowp1, float(w))

    e0 = pld(0, 0)
    s2 = e0 + pld(1, 0)
    s4 = s2 + pld(2, 0) + pld(3, 0)
    p0 = jnp.where(left, s2 / cnt(2), s4 / cnt(4)) - e0
    f0 = pld(0, 1)
    s8 = f0
    for k in range(1, 8):
        s8 = s8 + pld(k, 1)
    s16 = s8
    for k in range(8, 16):
        s16 = s16 + pld(k, 1)
    p1 = jnp.where(left, s8 / cnt(8), s16 / cnt(16)) - f0
    pw = pw_ref[...]
    mixed = _dot(p0.astype(BF16), pw[0:128, :]) + _dot(p1.astype(BF16), pw[128:256, :])
    opool_ref[...] = mixed * ps_ref[...]
    pext[0:POOL_HALO, :] = pext[tm:tm + POOL_HALO, :]

    g_fv = seg(_P_FOXV, 256)
    g_fk = seg(_P_FOXK, 512)
    g_fq = seg(_P_FOXQ, 256)

    q_ref[:, 0:128] = (rope(g_rope[:, 0:128]) * Q_SCALE).astype(BF16)
    q_ref[:, 128:256] = (rope(g_rope[:, 128:256]) * Q_SCALE).astype(BF16)
    block_onehot = jnp.where(lane - HEAD_DIM == row // SEL_LEN, 1.0, 0.0)
    ks_ref[...] = (rope(g_rope[:, 256:384]) + block_onehot).astype(BF16)
    kw_ref[...] = rope(g_rope[:, 384:512]).astype(BF16)
    vsw_t = jnp.transpose(g_small[:, 0:128]).astype(BF16)
    ones_upd = jnp.ones((V_ROWS - HEAD_DIM, UPD), BF16)
    for c in range(tm // UPD):
        vst_ref[c] = jnp.concatenate([vsw_t[0:HEAD_DIM, c * UPD:(c + 1) * UPD], ones_upd], axis=0)
    for c in range(tm // CHUNK):
        vwt_ref[c] = jnp.concatenate([vsw_t[HEAD_DIM:128, c * CHUNK:(c + 1) * CHUNK], ones_upd[:, 0:CHUNK]], axis=0)
    kvc_ref[...] = g_small[:, 128:256]

    fq_ref[...] = (g_fq * Q_SCALE).astype(BF16)
    terms = split3(csum * (-LOG2E))
    row16 = lax.broadcasted_iota(jnp.int32, (16, tm), 0)
    extra_t = jnp.zeros((16, tm), F32)
    for hd in range(N_HEADS):
        for j in range(3):
            src = jnp.broadcast_to(terms[j][4 + hd:5 + hd, :], (16, tm))
            extra_t = jnp.where(row16 == 3 * hd + j, src, extra_t)
    extra = jnp.transpose(jnp.concatenate(
        [jnp.zeros((HEAD_DIM, tm), F32), extra_t, jnp.zeros((128 - HEAD_DIM - 16, tm), F32)], axis=0))
    for hd in range(N_HEADS):
        fk_ref[:, hd * 128:(hd + 1) * 128] = (g_fk[:, hd * 128:(hd + 1) * 128] + extra).astype(BF16)
    fv_t = jnp.transpose(g_fv).astype(BF16)
    for c in range(tm // UPD):
        cols = slice(c * UPD, (c + 1) * UPD)
        fvt_ref[c] = jnp.concatenate(
            [piece for hd in range(N_HEADS) for piece in (fv_t[hd * HEAD_DIM:(hd + 1) * HEAD_DIM, cols], ones_upd)],
            axis=0)

    u = cv[:, 512:768] * cv[:, 0:256]
    cext[CONV_HALO:CONV_HALO + tm, :] = u
    y = (cext[pl.ds(CONV_HALO - 2, tm), :] * cw_ref[0:1, :]
         + cext[pl.ds(CONV_HALO - 1, tm), :] * cw_ref[1:2, :]
         + u * cw_ref[2:3, :])
    oconv_ref[...] = cv[:, 256:512] * y
    cext[0:CONV_HALO, :] = cext[tm:tm + CONV_HALO, :]

    hbuf[...] = _rmsnorm(xn_ref[...], g_ref[...]).astype(BF16)


def _proj_call(x, g, w1, cc, ss, fb, cw, pw, ps, tri):
    B, T, _ = x.shape
    tm = TM_PROJ
    nt = T // tm
    tok = lambda width: pl.BlockSpec((None, tm, width), lambda b, t: (b, t, 0))
    chunked = lambda rows, keys: pl.BlockSpec((None, tm // keys, rows, keys), lambda b, t: (b, t, 0, 0))
    const = lambda shape: pl.BlockSpec(shape, lambda b, t: (0,) * len(shape))
    out_shape = [
        jax.ShapeDtypeStruct((B, T, 256), BF16),
        jax.ShapeDtypeStruct((B, T, 128), BF16),
        jax.ShapeDtypeStruct((B, T, 128), BF16),
        jax.ShapeDtypeStruct((B, T // UPD, V_ROWS, UPD), BF16),
        jax.ShapeDtypeStruct((B, T // CHUNK, V_ROWS, CHUNK), BF16),
        jax.ShapeDtypeStruct((B, T, 128), F32),
        jax.ShapeDtypeStruct((B, nt, N_MISC_T, tm), F32),
        jax.ShapeDtypeStruct((B, T, 256), BF16),
        jax.ShapeDtypeStruct((B, T, 512), BF16),
        jax.ShapeDtypeStruct((B, T // UPD, N_HEADS * V_ROWS, UPD), BF16),
        jax.ShapeDtypeStruct((B, T, 256), F32),
        jax.ShapeDtypeStruct((B, T, 256), F32),
    ]
    out_specs = [tok(256), tok(128), tok(128), chunked(V_ROWS, UPD), chunked(V_ROWS, CHUNK), tok(128),
                 pl.BlockSpec((None, None, N_MISC_T, tm), lambda b, t: (b, t, 0, 0)),
                 tok(256), tok(512), chunked(N_HEADS * V_ROWS, UPD), tok(256), tok(256)]
    def next_tile(b, t):
        lin = jnp.minimum(b * nt + t + 1, B * nt - 1)
        return lin // nt, lin % nt, 0

    in_specs = [tok(D_MODEL), pl.BlockSpec((None, tm, D_MODEL), next_tile),
                const((1, D_MODEL)), const((D_MODEL, _P_COLS)),
                pl.BlockSpec((tm, 128), lambda b, t: (t, 0)), pl.BlockSpec((tm, 128), lambda b, t: (t, 0)),
                const((8, 1)), const((3, 256)), const((256, 256)), const((1, 256)), const((tm, tm))]
    return pl.pallas_call(
        functools.partial(_proj_kernel, tm=tm),
        grid=(B, nt),
        in_specs=in_specs,
        out_specs=out_specs,
        out_shape=out_shape,
        scratch_shapes=[pltpu.VMEM((tm + POOL_HALO, 256), F32), pltpu.VMEM((tm + CONV_HALO, 256), F32),
                        pltpu.VMEM((8, 128), F32), pltpu.VMEM((tm, D_MODEL), BF16)],
        compiler_params=pltpu.CompilerParams(dimension_semantics=("arbitrary", "arbitrary"),
                                             vmem_limit_bytes=VMEM_LIMIT),
        name="proj",
    )(x, x, g, w1, cc, ss, fb, cw, pw, ps, tri)


def _compress_kernel(x_ref, pt_ref, pb_ref, wt_ref, wb_ref, b1_ref, w2_ref, b2_ref, cc_ref, ss_ref,
                     kc_ref, vct_ref, *, rows):
    a = b = None
    for l in range(CMP_STRIDE):
        x_l = x_ref[pl.ds(l, rows, stride=CMP_STRIDE), :]
        a_l = _dot((x_l + pt_ref[l:l + 1, :]).astype(BF16), wt_ref[l])
        b_l = _dot((x_l + pb_ref[l:l + 1, :]).astype(BF16), wb_ref[l])
        a, b = (a_l, b_l) if a is None else (a + a_l, b + b_l)
    hid = a + pltpu.roll(b, rows - 1, 0) + b1_ref[...]
    act = hid * _sigmoid(hid)
    out = _dot(act.astype(BF16), w2_ref[...]) + b2_ref[...]
    lane = lax.broadcasted_iota(jnp.int32, (rows, 128), 1)
    roped = out[:, 0:128] * cc_ref[...] + out[:, 128:256] * ss_ref[...]
    block = lax.broadcasted_iota(jnp.int32, (rows, 128), 0)
    group_onehot = jnp.where(lane - HEAD_DIM == block // CMP_GROUP, 1.0, 0.0)
    kc_ref[...] = jnp.where(lane < HEAD_DIM, roped, group_onehot).astype(BF16)
    vct_ref[...] = jnp.transpose(out[:, 256:384])[0:HEAD_DIM, :].astype(BF16)


def _compress_call(kvc, pt, pb, wt, wb, b1, w2, b2, cc, ss):
    B, T, width = kvc.shape
    rows = T // CMP_STRIDE
    const = lambda shape: pl.BlockSpec(shape, lambda b: (0,) * len(shape))
    return pl.pallas_call(
        functools.partial(_compress_kernel, rows=rows),
        grid=(B,),
        in_specs=[pl.BlockSpec((None, T, width), lambda b: (b, 0, 0)),
                  const((CMP_STRIDE, width)), const((CMP_STRIDE, width)),
                  const((CMP_STRIDE, width, 256)), const((CMP_STRIDE, width, 256)),
                  const((1, 256)), const((256, 384)), const((1, 384)), const((rows, 128)), const((rows, 128))],
        out_specs=[pl.BlockSpec((None, rows, 128), lambda b: (b, 0, 0)),
                   pl.BlockSpec((None, HEAD_DIM, rows), lambda b: (b, 0, 0))],
        out_shape=[jax.ShapeDtypeStruct((B, rows, 128), BF16), jax.ShapeDtypeStruct((B, HEAD_DIM, rows), BF16)],
        compiler_params=pltpu.CompilerParams(dimension_semantics=("arbitrary",),
                                             vmem_limit_bytes=VMEM_LIMIT),
        name="compress",
    )(kvc, pt, pb, wt, wb, b1, w2, b2, cc, ss)


def _nsa_kernel(q_ref, ks_ref, kw_ref, vst_ref, vwt_ref, kc_ref, vct_ref, mt_ref, ovt_ref,
                o_ref, key_ref, sbuf, cbuf, mbuf, *, nblk, ncmp):
    step = pl.program_id(1)
    tq = CHUNK
    nq = N_HEADS * tq
    hw = nq // 2
    halves = (slice(0, hw), slice(hw, nq))
    n_c = KV_TILE // UPD
    n_win = WINDOW // tq
    subs = range(NSA_SUB)
    qis = [step * NSA_SUB + u for u in subs]
    q_local = lax.broadcasted_iota(jnp.int32, (1, nq), 1) % tq
    k_local = lax.broadcasted_iota(jnp.int32, (tq, hw), 0)
    q_loc = q_local[:, 0:hw]
    causal = k_local <= q_loc

    q4, qt_plain = [], []
    group = lax.broadcasted_iota(jnp.int32, (HEAD_DIM, nq), 0)
    slab = 2 * CMP_GROUP
    n_slab = lax.broadcasted_iota(jnp.int32, (slab, nq), 0)
    for u in subs:
        q_t = jnp.transpose(q_ref[u * tq:(u + 1) * tq, :].astype(F32))
        q4.append(jnp.concatenate([q_t[HEAD_DIM * h:HEAD_DIM * (h + 1)] for h in range(N_HEADS)], axis=1))
        qt_plain.append(jnp.concatenate([q4[u], jnp.zeros((HEAD_DIM, nq), F32)], axis=0).astype(BF16))
        group_bias = jnp.where(group > qis[u], MASK_VALUE, 0.0)
        qt_cmp = jnp.concatenate([q4[u], group_bias], axis=0).astype(BF16)
        cbuf[u] = _dot(kc_ref[...], qt_cmp)
        start = pl.multiple_of(jnp.maximum(qis[u] - 1, 0) * CMP_GROUP, CMP_GROUP)
        visible = ((start + n_slab) * CMP_STRIDE + (CMP_LEN - 1)) <= qis[u] * tq + q_local
        cbuf[u, pl.ds(start, slab), :] = jnp.where(visible, cbuf[u, pl.ds(start, slab), :], MASK_VALUE)

    o_cmp, imp = [], []
    for u in subs:
        s = cbuf[u]
        mc = jnp.max(s, axis=0, keepdims=True)
        mc = jnp.where(mc <= MASK_VALUE, 0.0, mc)
        pc = jnp.exp2(s - mc)
        lc = jnp.sum(pc, axis=0, keepdims=True)
        pc = pc * (1.0 / jnp.where(lc > 0.0, lc, 1.0))
        o_cmp.append(_dot(vct_ref[...], pc.astype(BF16)))
        psum = (pc[:, 0:tq] + pc[:, tq:2 * tq]) + (pc[:, 2 * tq:3 * tq] + pc[:, 3 * tq:4 * tq])
        imp.append(_dot(ovt_ref[...], psum, precision=lax.Precision.HIGHEST))

    def win_stream(u, hs):
        qi = qis[u]

        def score(c):
            if c == 0:
                own = kw_ref[pl.ds(pl.multiple_of(qi * tq, tq), tq), :]
                return jnp.where(causal, _dot(own, qt_plain[u][:, hs]), MASK_VALUE)
            ci = qi - n_win + (c - 1)
            start = pl.multiple_of(jnp.maximum(ci, 0) * tq, tq)
            s = _dot(kw_ref[pl.ds(start, tq), :], qt_plain[u][:, hs])
            ok = ((k_local > q_loc) & (ci >= 0)) if c == 1 else jnp.broadcast_to(ci >= 0, (tq, hw))
            return jnp.where(ok, s, MASK_VALUE)

        def value(c):
            return vwt_ref[qi] if c == 0 else vwt_ref[jnp.maximum(qi - n_win + (c - 1), 0)]

        return score, value, _flash_init(hw)

    carries_w = _attend_chunks([win_stream(u, hs) for u in subs for hs in halves], n_win + 1)
    o_win = [jnp.concatenate([acc / l for _, l, acc in carries_w[2 * u:2 * u + 2]], axis=1) for u in subs]

    jb = lax.broadcasted_iota(jnp.int32, (nblk, tq), 0)
    sub8 = lax.broadcasted_iota(jnp.int32, (8, tq), 0)
    later_rows = [sub8 > r for r in range(8)]
    qt_sel = []
    for u in subs:
        tt = qis[u] * tq + lax.broadcasted_iota(jnp.int32, (nblk, tq), 1)
        jt = tt // SEL_LEN
        forced = (jb == 0) | (jb == jt) | (jb == jt - 1)
        valid = jb * SEL_LEN <= tt
        key = jnp.where(valid, jnp.where(forced, imp[u] + FORCE_BONUS, imp[u]), MASK_VALUE)
        key_ref[u] = key
        key_rows = [key[8 * v:8 * v + 8] for v in range(nblk // 8)]
        ranks = [jnp.zeros((8, tq), F32) for _ in key_rows]
        for jp in range(nblk):
            other = jnp.broadcast_to(key_ref[u, pl.ds(jp, 1), :], (8, tq))
            for v, kv in enumerate(key_rows):
                if 8 * v > jp:
                    ahead = other >= kv
                elif 8 * v + 7 <= jp:
                    ahead = other > kv
                else:
                    ahead = (other > kv) | ((other == kv) & later_rows[jp - 8 * v])
                ranks[v] = ranks[v] + jnp.where(ahead, 1.0, 0.0)
        selected = jnp.concatenate(ranks, axis=0) < float(min(N_SEL, nblk))
        sel_bias = jnp.where(selected, 0.0, MASK_VALUE)
        if nblk < HEAD_DIM:
            sel_bias = jnp.concatenate([sel_bias, jnp.zeros((HEAD_DIM - nblk, tq), F32)], axis=0)
        qt_sel.append(jnp.concatenate([q4[u], jnp.concatenate([sel_bias] * N_HEADS, axis=1)], axis=0).astype(BF16))

    def slc_scores(u, kt, c):
        start = pl.multiple_of((kt * n_c + c) * UPD, UPD)
        return _dot(ks_ref[pl.ds(start, UPD), :], qt_sel[u])

    def slc_update(u, carry, s, kt, c, s_max=None):
        return tuple(_flash_t(s[:, hs], *carry[i], vst_ref[kt * n_c + c], None if s_max is None else s_max[:, hs])
                     for i, hs in enumerate(halves))

    def stage(u, kt, c):
        s = slc_scores(u, kt, c)
        sbuf[u, c] = s
        mbuf[pl.ds(u * n_c + c, 1), :] = jnp.max(s, axis=0, keepdims=True)

    for c in range(n_c):
        for u in subs:
            stage(u, 0, c)

    def slc_tile(kt, carries):
        carries = list(carries)
        for c in range(n_c):
            for u in subs:
                s, s_max = sbuf[u, c], mbuf[pl.ds(u * n_c + c, 1), :]
                stage(u, kt + 1, c)
                carries[u] = slc_update(u, carries[u], s, kt, c, s_max)
        return tuple(carries)

    carries = list(_run_tiles(slc_tile, step, tuple((_flash_init(hw), _flash_init(hw)) for u in subs)))
    k_upd = lax.broadcasted_iota(jnp.int32, (UPD, nq), 0)
    for c in range(n_c):
        first, last = c * UPD // tq, (c + 1) * UPD // tq - 1
        for u in subs:
            if first > u:
                continue
            s = sbuf[u, c]
            if last >= u:
                s = jnp.where(c * UPD + k_upd <= u * tq + q_local, s, MASK_VALUE)
            carries[u] = slc_update(u, carries[u], s, step, c)

    for u in subs:
        o_slc = jnp.concatenate([acc / l for _, l, acc in carries[u]], axis=1)
        g = mt_ref[:, u * tq:(u + 1) * tq]
        heads = []
        for h in range(N_HEADS):
            cols = slice(h * tq, (h + 1) * tq)
            heads.append(g[h:h + 1] * o_cmp[u][:, cols] + g[4 + h:5 + h] * o_slc[:, cols]
                         + g[8 + h:9 + h] * o_win[u][:, cols])
        o_ref[u * tq:(u + 1) * tq, :] = jnp.transpose(jnp.concatenate(heads, axis=0))


def _nsa_call(q, ks, kw, vst, vwt, kc, vct, mt, ovt):
    B, T, _ = q.shape
    nblk, ncmp = ovt.shape
    tq = CHUNK * NSA_SUB
    per_mt = TM_PROJ // tq
    full = lambda width: pl.BlockSpec((None, T, width), lambda b, i: (b, 0, 0))
    vfull = lambda keys: pl.BlockSpec((None, T // keys, V_ROWS, keys), lambda b, i: (b, 0, 0, 0))
    return pl.pallas_call(
        functools.partial(_nsa_kernel, nblk=nblk, ncmp=ncmp),
        grid=(B, T // tq),
        in_specs=[pl.BlockSpec((None, tq, 256), lambda b, i: (b, i, 0)),
                  full(128), full(128), vfull(UPD), vfull(CHUNK),
                  pl.BlockSpec((None, ncmp, 128), lambda b, i: (b, 0, 0)),
                  pl.BlockSpec((None, HEAD_DIM, ncmp), lambda b, i: (b, 0, 0)),
                  pl.BlockSpec((None, None, N_MISC_T, tq), lambda b, i: (b, i // per_mt, 0, i % per_mt)),
                  pl.BlockSpec((nblk, ncmp), lambda b, i: (0, 0))],
        out_specs=pl.BlockSpec((None, tq, 256), lambda b, i: (b, i, 0)),
        out_shape=jax.ShapeDtypeStruct((B, T, 256), F32),
        scratch_shapes=[pltpu.VMEM((NSA_SUB, nblk, CHUNK), F32),
                        pltpu.VMEM((NSA_SUB, KV_TILE // UPD, UPD, N_HEADS * CHUNK), F32),
                        pltpu.VMEM((NSA_SUB, ncmp, N_HEADS * CHUNK), F32),
                        pltpu.VMEM((NSA_SUB * KV_TILE // UPD, N_HEADS * CHUNK), F32)],
        compiler_params=pltpu.CompilerParams(dimension_semantics=("arbitrary", "arbitrary"),
                                             vmem_limit_bytes=VMEM_LIMIT),
        name="nsa",
    )(q, ks, kw, vst, vwt, kc, vct, mt, ovt)


def _fox_kernel(q_ref, k_ref, vt_ref, o_ref, sbuf, mbuf):
    qi = pl.program_id(1)
    nq = TQ_FOX
    n_chunk = KV_TILE // UPD
    t_lane = qi * nq + lax.broadcasted_iota(jnp.int32, (1, nq), 1)
    k_local = lax.broadcasted_iota(jnp.int32, (UPD, nq), 0)
    q_t = jnp.transpose(q_ref[...].astype(F32))
    extra_row = lax.broadcasted_iota(jnp.int32, (HEAD_DIM, nq), 0)

    def ones_rows(h):
        return jnp.where((extra_row >= 3 * h) & (extra_row < 3 * h + 3), 1.0, 0.0)

    qts = [jnp.concatenate([q_t[HEAD_DIM * h:HEAD_DIM * (h + 1)], ones_rows(h)], axis=0).astype(BF16)
           for h in range(N_HEADS)]

    def scores(kt, c, h):
        start = pl.multiple_of((kt * n_chunk + c) * UPD, UPD)
        return _dot(k_ref[pl.ds(start, UPD), h * 128:(h + 1) * 128], qts[h])

    def value(kt, c, h):
        return vt_ref[kt * n_chunk + c, h * V_ROWS:(h + 1) * V_ROWS, :]

    def stage(kt, c, h):
        s = scores(kt, c, h)
        sbuf[h, c] = s
        mbuf[pl.ds(h * n_chunk + c, 1), :] = jnp.max(s, axis=0, keepdims=True)

    for c in range(n_chunk):
        for h in range(N_HEADS):
            stage(0, c, h)

    def tile(kt, carries):
        carries = list(carries)
        for c in range(n_chunk):
            for h in range(N_HEADS):
                s, s_max = sbuf[h, c], mbuf[pl.ds(h * n_chunk + c, 1), :]
                stage(kt + 1, c, h)
                carries[h] = _flash_t(s, *carries[h], value(kt, c, h), s_max)
        return tuple(carries)

    carries = list(_run_tiles(tile, qi, tuple(_flash_init(nq) for _ in range(N_HEADS))))
    for c in range(n_chunk):
        causal = qi * nq + c * UPD + k_local <= t_lane
        for h in range(N_HEADS):
            carries[h] = _flash_t(jnp.where(causal, sbuf[h, c], MASK_VALUE), *carries[h], value(qi, c, h))
    o_ref[...] = jnp.transpose(jnp.concatenate([acc / l for _, l, acc in carries], axis=0))


def _fox_call(fq, fk, fvt):
    B, T, _ = fq.shape
    tq = TQ_FOX
    return pl.pallas_call(
        _fox_kernel,
        grid=(B, T // tq),
        in_specs=[pl.BlockSpec((None, tq, 256), lambda b, i: (b, i, 0)),
                  pl.BlockSpec((None, T, 512), lambda b, i: (b, 0, 0)),
                  pl.BlockSpec((None, T // UPD, N_HEADS * V_ROWS, UPD), lambda b, i: (b, 0, 0, 0))],
        out_specs=pl.BlockSpec((None, tq, 256), lambda b, i: (b, i, 0)),
        out_shape=jax.ShapeDtypeStruct((B, T, 256), F32),
        scratch_shapes=[pltpu.VMEM((N_HEADS, KV_TILE // UPD, UPD, tq), F32),
                        pltpu.VMEM((N_HEADS * KV_TILE // UPD, tq), F32)],
        compiler_params=pltpu.CompilerParams(dimension_semantics=("arbitrary", "arbitrary"),
                                             vmem_limit_bytes=VMEM_LIMIT),
        name="fox",
    )(fq, fk, fvt)


def _merge_kernel(x_ref, g_ref, wg_ref, wm_ref, wb_ref, wo_ref, fg_ref, o0_ref, o1_ref, o2_ref, o3_ref,
                  out_ref, *, last, tm):
    rows = tm // 2
    halves = [slice(u * rows, (u + 1) * rows) for u in range(2)]
    xs = [x_ref[r, :] for r in halves]
    hs = [_rmsnorm(x, g_ref[...]).astype(BF16) for x in xs]
    accs = []
    for u, r in enumerate(halves):
        gates = _dot(hs[u], wg_ref[...])
        acc = None
        for i, o_ref in enumerate((o0_ref, o1_ref, o2_ref, o3_ref)):
            merge = _dot(hs[u], wm_ref[:, i * D_MODEL:(i + 1) * D_MODEL])
            gate = gates[:, i * BRANCH_WIDTH:(i + 1) * BRANCH_WIDTH]
            a = (o_ref[r, :] * (gate * _sigmoid(gate))).astype(BF16)
            term = _sigmoid(merge) * _dot(a, wb_ref[i])
            acc = term if acc is None else acc + term
        accs.append(acc)
    for u, r in enumerate(halves):
        y = xs[u] + _dot(accs[u].astype(BF16), wo_ref[...])
        if last:
            y = _rmsnorm(y, fg_ref[...])
        out_ref[r, :] = y


def _merge_call(x, g, wg, wm, wb, wo, fg, o_nsa, o_pool, o_conv, o_fox, last):
    B, T, _ = x.shape
    tm = TM_MERGE
    tok = lambda width: pl.BlockSpec((None, tm, width), lambda b, t: (b, t, 0))
    const = lambda shape: pl.BlockSpec(shape, lambda b, t: (0,) * len(shape))
    return pl.pallas_call(
        functools.partial(_merge_kernel, last=last, tm=tm),
        grid=(B, T // tm),
        in_specs=[tok(D_MODEL), const((1, D_MODEL)), const((D_MODEL, N_BRANCH * BRANCH_WIDTH)),
                  const((D_MODEL, N_BRANCH * D_MODEL)), const((N_BRANCH, BRANCH_WIDTH, D_MODEL)),
                  const((D_MODEL, D_MODEL)), const((1, D_MODEL)), tok(256), tok(256), tok(256), tok(256)],
        out_specs=tok(D_MODEL),
        out_shape=jax.ShapeDtypeStruct((B, T, D_MODEL), F32),
        compiler_params=pltpu.CompilerParams(dimension_semantics=("arbitrary", "arbitrary"),
                                             vmem_limit_bytes=VMEM_LIMIT),
        name="merge",
    )(x, g, wg, wm, wb, wo, fg, o_nsa, o_pool, o_conv, o_fox)


def _rope_tables(pos):
    n = pos.shape[0]
    inv = ROPE_THETA ** (-jnp.arange(ROPE_HALF, dtype=F32) / ROPE_HALF)
    ang = pos.astype(F32)[:, None] * inv[None, :]
    cos, sin = jnp.cos(ang), jnp.sin(ang)
    rest = HEAD_DIM - 2 * ROPE_HALF
    c = jnp.concatenate([cos, cos, jnp.ones((n, rest), F32)], axis=1)
    s = jnp.concatenate([-sin, sin, jnp.zeros((n, rest), F32)], axis=1)
    return jnp.concatenate([c, c], axis=1), jnp.concatenate([s, s], axis=1)


def _proj_weight(w_in):
    col = lambda name, width: w_in[:, _OFF[name]:_OFF[name] + width]
    z64 = jnp.zeros((D_MODEL, HEAD_DIM), w_in.dtype)
    misc = jnp.concatenate([col("nsa_g", 12), col("fox_f", 4), jnp.zeros((D_MODEL, 112), w_in.dtype)], axis=1)
    fox_k = []
    for h in range(N_HEADS):
        fox_k += [w_in[:, _OFF["fox_k"] + h * HEAD_DIM:_OFF["fox_k"] + (h + 1) * HEAD_DIM], z64]
    w = jnp.concatenate([col("nsa_q", 256), col("k_s", 64), z64, col("k_w", 64), z64,
                         col("v_s", 64), col("v_w", 64), col("k_c", 128), misc,
                         col("fox_q", 256)] + fox_k + [col("fox_v", 256), col("pool", 256), col("conv", 768)],
                        axis=1)
    assert w.shape[1] == _P_COLS and w.dtype == BF16
    return w


def _compress_weights(cmp_pos, cmp_w1, cmp_b1, cmp_w2, cmp_b2):
    half = CMP_LEN // 2

    def w1_half(lo):
        wk = cmp_w1[0, lo * HEAD_DIM:(lo + half) * HEAD_DIM].reshape(half, HEAD_DIM, CMP_HIDDEN)
        wv = cmp_w1[1, lo * HEAD_DIM:(lo + half) * HEAD_DIM].reshape(half, HEAD_DIM, CMP_HIDDEN)
        z = jnp.zeros_like(wk)
        top = jnp.concatenate([wk, z], axis=2)
        bot = jnp.concatenate([z, wv], axis=2)
        return jnp.concatenate([top, bot], axis=1).astype(BF16)

    def pos_half(lo):
        return jnp.concatenate([cmp_pos[0, lo:lo + half], cmp_pos[1, lo:lo + half]], axis=1)

    lane = np.arange(HEAD_DIM)
    perm = np.where(lane < ROPE_HALF, lane + ROPE_HALF, np.where(lane < 2 * ROPE_HALF, lane - ROPE_HALF, lane))
    w2k, w2v = cmp_w2[0], cmp_w2[1]
    zk = jnp.zeros_like(w2k)
    w2 = jnp.concatenate([
        jnp.concatenate([w2k, w2k, w2k[:, perm], w2k[:, perm], zk, zk], axis=1),
        jnp.concatenate([zk, zk, zk, zk, w2v, w2v], axis=1)], axis=0).astype(BF16)
    b2k, b2v = cmp_b2[0], cmp_b2[1]
    b2 = jnp.concatenate([b2k, b2k, b2k[perm], b2k[perm], b2v, b2v])[None, :]
    b1 = jnp.concatenate([cmp_b1[0], cmp_b1[1]])[None, :]
    return pos_half(0), pos_half(half), w1_half(0), w1_half(half), b1, w2, b2


def _block_diag(pool_w):
    n, c, _ = pool_w.shape
    out = jnp.zeros((n * c, n * c), pool_w.dtype)
    for i in range(n):
        out = out.at[i * c:(i + 1) * c, i * c:(i + 1) * c].set(pool_w[i])
    return out


def kernel(x, norm_g, w_in, fox_f_bias, cmp_pos, cmp_w1, cmp_b1, cmp_w2, cmp_b2, pool_w, pool_scale, conv_w,
           w_branch, w_out, final_norm_g):
    B, T, _ = x.shape
    depth = norm_g.shape[0]
    assert T % TM_PROJ == 0 and T % KV_TILE == 0 and T // SEL_LEN <= HEAD_DIM and T >= WINDOW + CHUNK
    assert NSA_SUB * CHUNK == KV_TILE == TQ_FOX and TM_PROJ % KV_TILE == 0
    rows = T // CMP_STRIDE
    nblk = T // SEL_LEN

    cc_tok, ss_tok = _rope_tables(jnp.arange(T))
    cmp_end = jnp.arange(rows) * CMP_STRIDE + CMP_LEN - 1
    cc_cmp, ss_cmp = _rope_tables(cmp_end)
    ci = np.arange(rows)[:, None] * CMP_STRIDE
    sj = np.arange(nblk)[None, :] * SEL_LEN
    overlap_t = jnp.asarray(((ci < sj + SEL_LEN) & (ci + CMP_LEN > sj)).T, F32)
    tri = jnp.asarray(np.triu(np.ones((TM_PROJ, TM_PROJ), np.float32)), BF16)

    w_in_bf = w_in.astype(BF16)
    for l in range(depth):
        w1 = _proj_weight(w_in_bf[l])
        fb = jnp.zeros((8, 1), F32).at[4:8, 0].set(fox_f_bias[l])
        (q, ks, kw, vst, vwt, kvc, mt, fq, fk, fvt, o_pool, o_conv) = _proj_call(
            x, norm_g[l][None, :], w1, cc_tok, ss_tok, fb, conv_w[l], _block_diag(pool_w[l]).astype(BF16),
            pool_scale[l][None, :], tri)
        kc, vct = _compress_call(kvc, *_compress_weights(cmp_pos[l], cmp_w1[l], cmp_b1[l], cmp_w2[l], cmp_b2[l]),
                                 cc_cmp, ss_cmp)
        o_nsa = _nsa_call(q, ks, kw, vst, vwt, kc, vct, mt, overlap_t)
        o_fox = _fox_call(fq, fk, fvt)
        wg = w_in_bf[l][:, _OFF["gate"]:_OFF["merge"]]
        wm = w_in_bf[l][:, _OFF["merge"]:_OFF["end"]]
        x = _merge_call(x, norm_g[l][None, :], wg, wm, w_branch[l].astype(BF16), w_out[l].astype(BF16),
                        final_norm_g[None, :], o_nsa, o_pool, o_conv, o_fox, last=(l == depth - 1))
    return x
```

```python
import functools

import numpy as np
import jax
import jax.numpy as jnp
from jax import lax
from jax.experimental import pallas as pl
from jax.experimental.pallas import tpu as pltpu

F32 = jnp.float32
BF16 = jnp.bfloat16

D_MODEL = 1024
N_BRANCH = 4
BRANCH_WIDTH = 256
HEAD_DIM = 64
N_HEADS = 4
CMP_LEN = 32
CMP_STRIDE = 16
CMP_HIDDEN = 128
CMP_GROUP = 8
SEL_LEN = 64
N_SEL = 16
RANK_LEVELS = (16, 24, 32, 40, 48, 56, 64)
WINDOW = 512
FORCE_BONUS = 1.0e4
ROPE_THETA = 500000.0
ROPE_HALF = 8
NORM_EPS = 1e-6
MASK_VALUE = -1e30

_OFF = dict(nsa_q=0, k_c=256, v_c=320, k_s=384, v_s=448, k_w=512, v_w=576, nsa_g=640, pool=652,
            conv=908, fox_q=1676, fox_k=1932, fox_v=2188, fox_f=2444, gate=2448, merge=3472, end=7568)

_P_ROPE = 0
_P_VSW = 512
_P_FOXQ = 896
_P_FOXK = 1152
_P_FOXV = 1664
_P_POOL = 1920
_P_CONV = 2176
_P_COLS = 2944

CHUNK = 128
UPD = 256
V_ROWS = HEAD_DIM + 16
LOG2E = 1.4426950408889634
Q_SCALE = HEAD_DIM ** -0.5 * LOG2E
TM_PROJ = 512
KV_TILE = 512
NSA_SUB = 4
TQ_FOX = 512
TM_MERGE = 512
N_GATES = 3 * N_HEADS
N_MISC_T = N_GATES + N_HEADS
POOL_HALO = 16
CONV_HALO = 8
VMEM_LIMIT = 56 * 1024 * 1024


def _dot(a, b, precision=None):
    return jnp.dot(a, b, preferred_element_type=F32, precision=precision)


def _sigmoid(x):
    return 1.0 / (1.0 + jnp.exp(-x))


def _rmsnorm(x, g):
    return x * lax.rsqrt(jnp.mean(x * x, axis=-1, keepdims=True) + NORM_EPS) * g


def _flash_t(s, m, l, acc, v_t, s_max=None):
    m_new = jnp.maximum(m, jnp.max(s, axis=0, keepdims=True) if s_max is None else s_max)
    alpha = jnp.exp2(m - m_new)
    pv = _dot(v_t, jnp.exp2(s - m_new).astype(BF16))
    l = alpha * l + pv[HEAD_DIM:HEAD_DIM + 1]
    acc = alpha * acc + pv[0:HEAD_DIM]
    return m_new, l, acc


def _attend_chunks(streams, n):
    nxt = [st[0](0) for st in streams]
    carries = [st[2] for st in streams]
    for c in range(n):
        cur = nxt
        if c + 1 < n:
            nxt = [st[0](c + 1) for st in streams]
        carries = [_flash_t(cur[i], *carries[i], streams[i][1](c)) for i in range(len(streams))]
    return carries


def _run_tiles(tile, n, carries):
    def run(first, count, carries):
        for j in range(count):
            carries = tile(first + j, carries)
        return carries

    quads = n // 4
    carries = lax.fori_loop(0, quads, lambda i, cs: run(4 * i, 4, cs), carries)
    pairs = (n - 4 * quads) // 2
    carries = lax.fori_loop(0, pairs, lambda i, cs: run(4 * quads, 2, cs), carries)
    return lax.fori_loop(4 * quads + 2 * pairs, n, tile, carries)


def _flash_init(n):
    return (jnp.full((1, n), MASK_VALUE, F32), jnp.zeros((1, n), F32), jnp.zeros((HEAD_DIM, n), F32))


def _proj_kernel(x_ref, xn_ref, g_ref, w_ref, cc_ref, ss_ref, fb_ref, cw_ref, pw_ref, ps_ref, tri_ref,
                 q_ref, ks_ref, kw_ref, vst_ref, vwt_ref, kvc_ref, mt_ref, fq_ref, fk_ref, fvt_ref,
                 opool_ref, oconv_ref, pext, cext, ccarry, hbuf, *, tm):
    ti = pl.program_id(1)

    @pl.when((pl.program_id(0) == 0) & (ti == 0))
    def _():
        hbuf[...] = _rmsnorm(x_ref[...], g_ref[...]).astype(BF16)

    h = hbuf[...]

    def seg(lo, width):
        return _dot(h, w_ref[:, lo:lo + width])

    lane = lax.broadcasted_iota(jnp.int32, (tm, 128), 1)
    row = lax.broadcasted_iota(jnp.int32, (tm, 128), 0) + ti * tm
    first_half = (lane % HEAD_DIM) < ROPE_HALF
    cc = cc_ref[...]
    ss = ss_ref[...]

    def rope(xc):
        partner = jnp.where(first_half, pltpu.roll(xc, 128 - ROPE_HALF, 1), pltpu.roll(xc, ROPE_HALF, 1))
        return xc * cc + partner * ss

    @pl.when(ti == 0)
    def _():
        ccarry[...] = jnp.zeros_like(ccarry)
        pext[0:POOL_HALO, :] = jnp.zeros((POOL_HALO, BRANCH_WIDTH), F32)
        cext[0:CONV_HALO, :] = jnp.zeros((CONV_HALO, BRANCH_WIDTH), F32)

    g_small = seg(_P_VSW, 384)
    g_rope = seg(_P_ROPE, 512)

    z_t = jnp.transpose(g_small[:, 256:384])[0:N_MISC_T, :]
    gates = _sigmoid(z_t)
    row8 = lax.broadcasted_iota(jnp.int32, (8, tm), 0)
    zb = z_t[8:16] + fb_ref[...]
    logf = jnp.where(row8 >= N_GATES - 8, jnp.minimum(zb, 0.0) - jnp.log(1.0 + jnp.exp(-jnp.abs(zb))), 0.0)

    def split3(v):
        hi = v.astype(BF16).astype(F32)
        mid = (v - hi).astype(BF16).astype(F32)
        return hi, mid, (v - hi) - mid

    parts = jnp.concatenate(list(split3(logf)) + [jnp.zeros((8, tm), F32)], axis=0).astype(BF16)
    part_sums = _dot(parts, tri_ref[...])
    csum = (part_sums[0:8] + part_sums[8:16]) + part_sums[16:24] + ccarry[:, 0:1]
    ccarry[...] = jnp.broadcast_to(csum[:, tm - 1:tm], (8, 128))
    mt_ref[0:8, :] = gates[0:8]
    mt_ref[8:16, :] = jnp.where(row8 < N_GATES - 8, gates[8:16], csum)

    g_pool = seg(_P_POOL, 256)
    cv = seg(_P_CONV, 768)

    pext[POOL_HALO:POOL_HALO + tm, :] = g_pool

    def pld(k, c):
        return pext[pl.ds(POOL_HALO - k, tm), c * 128:(c + 1) * 128]

    left = lane < 64
    rowp1 = (row + 1).astype(F32)

    def cnt(w):
        return jnp.minimum(rowp1, float(w))

    e0 = pld(0, 0)
    s2 = e0 + pld(1, 0)
    s4 = s2 + pld(2, 0) + pld(3, 0)
    p0 = jnp.where(left, s2 / cnt(2), s4 / cnt(4)) - e0
    f0 = pld(0, 1)
    s8 = f0
    for k in range(1, 8):
        s8 = s8 + pld(k, 1)
    s16 = s8
    for k in range(8, 16):
        s16 = s16 + pld(k, 1)
    p1 = jnp.where(left, s8 / cnt(8), s16 / cnt(16)) - f0
    pw = pw_ref[...]
    mixed = _dot(p0.astype(BF16), pw[0:128, :]) + _dot(p1.astype(BF16), pw[128:256, :])
    opool_ref[...] = mixed * ps_ref[...]
    pext[0:POOL_HALO, :] = pext[tm:tm + POOL_HALO, :]

    g_fv = seg(_P_FOXV, 256)
    g_fk = seg(_P_FOXK, 512)
    g_fq = seg(_P_FOXQ, 256)

    q_ref[:, 0:128] = (rope(g_rope[:, 0:128]) * Q_SCALE).astype(BF16)
    q_ref[:, 128:256] = (rope(g_rope[:, 128:256]) * Q_SCALE).astype(BF16)
    block_onehot = jnp.where(lane - HEAD_DIM == row // SEL_LEN, 1.0, 0.0)
    ks_ref[...] = (rope(g_rope[:, 256:384]) + block_onehot).astype(BF16)
    kw_ref[...] = rope(g_rope[:, 384:512]).astype(BF16)
    vsw_t = jnp.transpose(g_small[:, 0:128]).astype(BF16)
    ones_upd = jnp.ones((V_ROWS - HEAD_DIM, UPD), BF16)
    for c in range(tm // UPD):
        vst_ref[c] = jnp.concatenate([vsw_t[0:HEAD_DIM, c * UPD:(c + 1) * UPD], ones_upd], axis=0)
    for c in range(tm // CHUNK):
        vwt_ref[c] = jnp.concatenate([vsw_t[HEAD_DIM:128, c * CHUNK:(c + 1) * CHUNK], ones_upd[:, 0:CHUNK]], axis=0)
    kvc_ref[...] = g_small[:, 128:256]

    fq_ref[...] = (g_fq * Q_SCALE).astype(BF16)
    terms = split3(csum * (-LOG2E))
    row16 = lax.broadcasted_iota(jnp.int32, (16, tm), 0)
    extra_t = jnp.zeros((16, tm), F32)
    for hd in range(N_HEADS):
        for j in range(3):
            src = jnp.broadcast_to(terms[j][N_GATES - 8 + hd:N_GATES - 7 + hd, :], (16, tm))
            extra_t = jnp.where(row16 == 3 * hd + j, src, extra_t)
    extra = jnp.transpose(jnp.concatenate(
        [jnp.zeros((HEAD_DIM, tm), F32), extra_t, jnp.zeros((128 - HEAD_DIM - 16, tm), F32)], axis=0))
    for hd in range(N_HEADS):
        fk_ref[:, hd * 128:(hd + 1) * 128] = (g_fk[:, hd * 128:(hd + 1) * 128] + extra).astype(BF16)
    fv_t = jnp.transpose(g_fv).astype(BF16)
    for c in range(tm // UPD):
        cols = slice(c * UPD, (c + 1) * UPD)
        fvt_ref[c] = jnp.concatenate(
            [piece for hd in range(N_HEADS) for piece in (fv_t[hd * HEAD_DIM:(hd + 1) * HEAD_DIM, cols], ones_upd)],
            axis=0)

    u = cv[:, 512:768] * cv[:, 0:256]
    cext[CONV_HALO:CONV_HALO + tm, :] = u
    y = (cext[pl.ds(CONV_HALO - 2, tm), :] * cw_ref[0:1, :]
         + cext[pl.ds(CONV_HALO - 1, tm), :] * cw_ref[1:2, :]
         + u * cw_ref[2:3, :])
    oconv_ref[...] = cv[:, 256:512] * y
    cext[0:CONV_HALO, :] = cext[tm:tm + CONV_HALO, :]

    hbuf[...] = _rmsnorm(xn_ref[...], g_ref[...]).astype(BF16)


def _proj_call(x, g, w1, cc, ss, fb, cw, pw, ps, tri):
    B, T, _ = x.shape
    tm = TM_PROJ
    nt = T // tm
    tok = lambda width: pl.BlockSpec((None, tm, width), lambda b, t: (b, t, 0))
    chunked = lambda rows, keys: pl.BlockSpec((None, tm // keys, rows, keys), lambda b, t: (b, t, 0, 0))
    const = lambda shape: pl.BlockSpec(shape, lambda b, t: (0,) * len(shape))
    out_shape = [
        jax.ShapeDtypeStruct((B, T, 256), BF16),
        jax.ShapeDtypeStruct((B, T, 128), BF16),
        jax.ShapeDtypeStruct((B, T, 128), BF16),
        jax.ShapeDtypeStruct((B, T // UPD, V_ROWS, UPD), BF16),
        jax.ShapeDtypeStruct((B, T // CHUNK, V_ROWS, CHUNK), BF16),
        jax.ShapeDtypeStruct((B, T, 128), F32),
        jax.ShapeDtypeStruct((B, nt, N_MISC_T, tm), F32),
        jax.ShapeDtypeStruct((B, T, 256), BF16),
        jax.ShapeDtypeStruct((B, T, 512), BF16),
        jax.ShapeDtypeStruct((B, T // UPD, N_HEADS * V_ROWS, UPD), BF16),
        jax.ShapeDtypeStruct((B, T, 256), F32),
        jax.ShapeDtypeStruct((B, T, 256), F32),
    ]
    out_specs = [tok(256), tok(128), tok(128), chunked(V_ROWS, UPD), chunked(V_ROWS, CHUNK), tok(128),
                 pl.BlockSpec((None, None, N_MISC_T, tm), lambda b, t: (b, t, 0, 0)),
                 tok(256), tok(512), chunked(N_HEADS * V_ROWS, UPD), tok(256), tok(256)]
    def next_tile(b, t):
        lin = jnp.minimum(b * nt + t + 1, B * nt - 1)
        return lin // nt, lin % nt, 0

    in_specs = [tok(D_MODEL), pl.BlockSpec((None, tm, D_MODEL), next_tile),
                const((1, D_MODEL)), const((D_MODEL, _P_COLS)),
                pl.BlockSpec((tm, 128), lambda b, t: (t, 0)), pl.BlockSpec((tm, 128), lambda b, t: (t, 0)),
                const((8, 1)), const((3, 256)), const((256, 256)), const((1, 256)), const((tm, tm))]
    return pl.pallas_call(
        functools.partial(_proj_kernel, tm=tm),
        grid=(B, nt),
        in_specs=in_specs,
        out_specs=out_specs,
        out_shape=out_shape,
        scratch_shapes=[pltpu.VMEM((tm + POOL_HALO, 256), F32), pltpu.VMEM((tm + CONV_HALO, 256), F32),
                        pltpu.VMEM((8, 128), F32), pltpu.VMEM((tm, D_MODEL), BF16)],
        compiler_params=pltpu.CompilerParams(dimension_semantics=("arbitrary", "arbitrary"),
                                             vmem_limit_bytes=VMEM_LIMIT),
        name="proj",
    )(x, x, g, w1, cc, ss, fb, cw, pw, ps, tri)


def _compress_kernel(x_ref, pt_ref, pb_ref, wt_ref, wb_ref, b1_ref, w2_ref, b2_ref, cc_ref, ss_ref,
                     kc_ref, vct_ref, *, rows):
    a = b = None
    for l in range(CMP_STRIDE):
        x_l = x_ref[pl.ds(l, rows, stride=CMP_STRIDE), :]
        a_l = _dot((x_l + pt_ref[l:l + 1, :]).astype(BF16), wt_ref[l])
        b_l = _dot((x_l + pb_ref[l:l + 1, :]).astype(BF16), wb_ref[l])
        a, b = (a_l, b_l) if a is None else (a + a_l, b + b_l)
    hid = a + pltpu.roll(b, rows - 1, 0) + b1_ref[...]
    act = hid * _sigmoid(hid)
    out = _dot(act.astype(BF16), w2_ref[...]) + b2_ref[...]
    lane = lax.broadcasted_iota(jnp.int32, (rows, 128), 1)
    roped = out[:, 0:128] * cc_ref[...] + out[:, 128:256] * ss_ref[...]
    block = lax.broadcasted_iota(jnp.int32, (rows, 128), 0)
    group_onehot = jnp.where(lane - HEAD_DIM == block // CMP_GROUP, 1.0, 0.0)
    kc_ref[...] = jnp.where(lane < HEAD_DIM, roped, group_onehot).astype(BF16)
    vct_ref[...] = jnp.transpose(out[:, 256:384])[0:HEAD_DIM, :].astype(BF16)


def _compress_call(kvc, pt, pb, wt, wb, b1, w2, b2, cc, ss):
    B, T, width = kvc.shape
    rows = T // CMP_STRIDE
    const = lambda shape: pl.BlockSpec(shape, lambda b: (0,) * len(shape))
    return pl.pallas_call(
        functools.partial(_compress_kernel, rows=rows),
        grid=(B,),
        in_specs=[pl.BlockSpec((None, T, width), lambda b: (b, 0, 0)),
                  const((CMP_STRIDE, width)), const((CMP_STRIDE, width)),
                  const((CMP_STRIDE, width, 256)), const((CMP_STRIDE, width, 256)),
                  const((1, 256)), const((256, 384)), const((1, 384)), const((rows, 128)), const((rows, 128))],
        out_specs=[pl.BlockSpec((None, rows, 128), lambda b: (b, 0, 0)),
                   pl.BlockSpec((None, HEAD_DIM, rows), lambda b: (b, 0, 0))],
        out_shape=[jax.ShapeDtypeStruct((B, rows, 128), BF16), jax.ShapeDtypeStruct((B, HEAD_DIM, rows), BF16)],
        compiler_params=pltpu.CompilerParams(dimension_semantics=("arbitrary",),
                                             vmem_limit_bytes=VMEM_LIMIT),
        name="compress",
    )(kvc, pt, pb, wt, wb, b1, w2, b2, cc, ss)


def _nsa_kernel(q_ref, ks_ref, kw_ref, vst_ref, vwt_ref, kc_ref, vct_ref, mt_ref, ovt_ref,
                o_ref, key_ref, sbuf, cbuf, mbuf, *, nblk, ncmp):
    step = pl.program_id(1)
    tq = CHUNK
    nq = N_HEADS * tq
    hw = nq // 2
    halves = (slice(0, hw), slice(hw, nq))
    n_c = KV_TILE // UPD
    n_win = WINDOW // tq
    subs = range(NSA_SUB)
    qis = [step * NSA_SUB + u for u in subs]
    q_local = lax.broadcasted_iota(jnp.int32, (1, nq), 1) % tq
    k_local = lax.broadcasted_iota(jnp.int32, (tq, hw), 0)
    q_loc = q_local[:, 0:hw]
    causal = k_local <= q_loc

    q4, qt_plain = [], []
    group = lax.broadcasted_iota(jnp.int32, (HEAD_DIM, nq), 0)
    slab = 2 * CMP_GROUP
    n_slab = lax.broadcasted_iota(jnp.int32, (slab, nq), 0)
    for u in subs:
        q_t = jnp.transpose(q_ref[u * tq:(u + 1) * tq, :].astype(F32))
        q4.append(jnp.concatenate([q_t[HEAD_DIM * h:HEAD_DIM * (h + 1)] for h in range(N_HEADS)], axis=1))
        qt_plain.append(jnp.concatenate([q4[u], jnp.zeros((HEAD_DIM, nq), F32)], axis=0).astype(BF16))
        group_bias = jnp.where(group > qis[u], MASK_VALUE, 0.0)
        qt_cmp = jnp.concatenate([q4[u], group_bias], axis=0).astype(BF16)
        cbuf[u] = _dot(kc_ref[...], qt_cmp)
        start = pl.multiple_of(jnp.maximum(qis[u] - 1, 0) * CMP_GROUP, CMP_GROUP)
        visible = ((start + n_slab) * CMP_STRIDE + (CMP_LEN - 1)) <= qis[u] * tq + q_local
        cbuf[u, pl.ds(start, slab), :] = jnp.where(visible, cbuf[u, pl.ds(start, slab), :], MASK_VALUE)

    o_cmp, imp = [], []
    for u in subs:
        s = cbuf[u]
        mc = jnp.max(s, axis=0, keepdims=True)
        mc = jnp.where(mc <= MASK_VALUE, 0.0, mc)
        pc = jnp.exp2(s - mc)
        lc = jnp.sum(pc, axis=0, keepdims=True)
        pc = pc * (1.0 / jnp.where(lc > 0.0, lc, 1.0))
        o_cmp.append(_dot(vct_ref[...], pc.astype(BF16)))
        psum = (pc[:, 0:tq] + pc[:, tq:2 * tq]) + (pc[:, 2 * tq:3 * tq] + pc[:, 3 * tq:4 * tq])
        imp.append(_dot(ovt_ref[...], psum, precision=lax.Precision.HIGHEST))

    jb = lax.broadcasted_iota(jnp.int32, (nblk, tq), 0)
    sub8 = lax.broadcasted_iota(jnp.int32, (8, tq), 0)
    later_rows = [sub8 > r for r in range(8)]
    keys = []
    for u in subs:
        tt = qis[u] * tq + lax.broadcasted_iota(jnp.int32, (nblk, tq), 1)
        jt = tt // SEL_LEN
        forced = (jb == 0) | (jb == jt) | (jb == jt - 1)
        valid = jb * SEL_LEN <= tt
        keys.append(jnp.where(valid, jnp.where(forced, imp[u] + FORCE_BONUS, imp[u]), MASK_VALUE))
        key_ref[u] = keys[u]

    def select_bias(n_rank):
        out = []
        for u in subs:
            key_rows = [keys[u][8 * v:8 * v + 8] for v in range(n_rank // 8)]
            ranks = [jnp.zeros((8, tq), F32) for _ in key_rows]
            for jp in range(n_rank):
                other = jnp.broadcast_to(key_ref[u, pl.ds(jp, 1), :], (8, tq))
                for v, kv in enumerate(key_rows):
                    if 8 * v > jp:
                        ahead = other >= kv
                    elif 8 * v + 7 <= jp:
                        ahead = other > kv
                    else:
                        ahead = (other > kv) | ((other == kv) & later_rows[jp - 8 * v])
                    ranks[v] = ranks[v] + jnp.where(ahead, 1.0, 0.0)
            bias = jnp.where(jnp.concatenate(ranks, axis=0) < float(min(N_SEL, nblk)), 0.0, MASK_VALUE)
            out.append(jnp.concatenate([bias, jnp.full((HEAD_DIM - n_rank, tq), MASK_VALUE, F32)], axis=0)
                       if n_rank < HEAD_DIM else bias)
        return tuple(out)

    visible = (step + 1) * (KV_TILE // SEL_LEN)
    levels = sorted({min(nblk, r) for r in RANK_LEVELS})
    level = sum((visible > r).astype(jnp.int32) for r in levels[:-1]) if len(levels) > 1 else 0
    sel_bias = lax.switch(level, [functools.partial(select_bias, r) for r in levels])
    qt_sel = [jnp.concatenate([q4[u], jnp.concatenate([sel_bias[u]] * N_HEADS, axis=1)], axis=0).astype(BF16)
              for u in subs]

    def win_stream(u, hs):
        qi = qis[u]

        def score(c):
            if c == 0:
                own = kw_ref[pl.ds(pl.multiple_of(qi * tq, tq), tq), :]
                return jnp.where(causal, _dot(own, qt_plain[u][:, hs]), MASK_VALUE)
            ci = qi - n_win + (c - 1)
            start = pl.multiple_of(jnp.maximum(ci, 0) * tq, tq)
            s = _dot(kw_ref[pl.ds(start, tq), :], qt_plain[u][:, hs])
            ok = ((k_local > q_loc) & (ci >= 0)) if c == 1 else jnp.broadcast_to(ci >= 0, (tq, hw))
            return jnp.where(ok, s, MASK_VALUE)

        def value(c):
            return vwt_ref[qi] if c == 0 else vwt_ref[jnp.maximum(qi - n_win + (c - 1), 0)]

        return score, value, _flash_init(hw)

    carries_w = _attend_chunks([win_stream(u, hs) for u in subs for hs in halves], n_win + 1)
    o_win = [jnp.concatenate([acc / l for _, l, acc in carries_w[2 * u:2 * u + 2]], axis=1) for u in subs]

    def slc_scores(u, kt, c):
        start = pl.multiple_of((kt * n_c + c) * UPD, UPD)
        return _dot(ks_ref[pl.ds(start, UPD), :], qt_sel[u])

    def slc_update(u, carry, s, kt, c, s_max=None):
        return tuple(_flash_t(s[:, hs], *carry[i], vst_ref[kt * n_c + c], None if s_max is None else s_max[:, hs])
                     for i, hs in enumerate(halves))

    def stage(u, kt, c):
        s = slc_scores(u, kt, c)
        sbuf[u, c] = s
        mbuf[pl.ds(u * n_c + c, 1), :] = jnp.max(s, axis=0, keepdims=True)

    for c in range(n_c):
        for u in subs:
            stage(u, 0, c)

    def slc_tile(kt, carries):
        carries = list(carries)
        for c in range(n_c):
            for u in subs:
                s, s_max = sbuf[u, c], mbuf[pl.ds(u * n_c + c, 1), :]
                stage(u, kt + 1, c)
                carries[u] = slc_update(u, carries[u], s, kt, c, s_max)
        return tuple(carries)

    carries = list(_run_tiles(slc_tile, step, tuple((_flash_init(hw), _flash_init(hw)) for u in subs)))
    k_upd = lax.broadcasted_iota(jnp.int32, (UPD, nq), 0)
    for c in range(n_c):
        first, last = c * UPD // tq, (c + 1) * UPD // tq - 1
        for u in subs:
            if first > u:
                continue
            s, s_max = sbuf[u, c], mbuf[pl.ds(u * n_c + c, 1), :]
            if last >= u:
                s, s_max = jnp.where(c * UPD + k_upd <= u * tq + q_local, s, MASK_VALUE), None
            carries[u] = slc_update(u, carries[u], s, step, c, s_max)

    for u in subs:
        o_slc = jnp.concatenate([acc / l for _, l, acc in carries[u]], axis=1)
        g = mt_ref[:, u * tq:(u + 1) * tq]
        heads = []
        for h in range(N_HEADS):
            cols = slice(h * tq, (h + 1) * tq)
            heads.append(g[h:h + 1] * o_cmp[u][:, cols] + g[N_HEADS + h:N_HEADS + h + 1] * o_slc[:, cols]
                         + g[2 * N_HEADS + h:2 * N_HEADS + h + 1] * o_win[u][:, cols])
        o_ref[u * tq:(u + 1) * tq, :] = jnp.transpose(jnp.concatenate(heads, axis=0))


def _nsa_call(q, ks, kw, vst, vwt, kc, vct, mt, ovt):
    B, T, _ = q.shape
    nblk, ncmp = ovt.shape
    tq = CHUNK * NSA_SUB
    per_mt = TM_PROJ // tq
    full = lambda width: pl.BlockSpec((None, T, width), lambda b, i: (b, 0, 0))
    vfull = lambda keys: pl.BlockSpec((None, T // keys, V_ROWS, keys), lambda b, i: (b, 0, 0, 0))
    return pl.pallas_call(
        functools.partial(_nsa_kernel, nblk=nblk, ncmp=ncmp),
        grid=(B, T // tq),
        in_specs=[pl.BlockSpec((None, tq, 256), lambda b, i: (b, i, 0)),
                  full(128), full(128), vfull(UPD), vfull(CHUNK),
                  pl.BlockSpec((None, ncmp, 128), lambda b, i: (b, 0, 0)),
                  pl.BlockSpec((None, HEAD_DIM, ncmp), lambda b, i: (b, 0, 0)),
                  pl.BlockSpec((None, None, N_MISC_T, tq), lambda b, i: (b, i // per_mt, 0, i % per_mt)),
                  pl.BlockSpec((nblk, ncmp), lambda b, i: (0, 0))],
        out_specs=pl.BlockSpec((None, tq, 256), lambda b, i: (b, i, 0)),
        out_shape=jax.ShapeDtypeStruct((B, T, 256), F32),
        scratch_shapes=[pltpu.VMEM((NSA_SUB, nblk, CHUNK), F32),
                        pltpu.VMEM((NSA_SUB, KV_TILE // UPD, UPD, N_HEADS * CHUNK), F32),
                        pltpu.VMEM((NSA_SUB, ncmp, N_HEADS * CHUNK), F32),
                        pltpu.VMEM((NSA_SUB * KV_TILE // UPD, N_HEADS * CHUNK), F32)],
        compiler_params=pltpu.CompilerParams(dimension_semantics=("arbitrary", "arbitrary"),
                                             vmem_limit_bytes=VMEM_LIMIT),
        name="nsa",
    )(q, ks, kw, vst, vwt, kc, vct, mt, ovt)


def _fox_kernel(q_ref, k_ref, vt_ref, o_ref, sbuf, mbuf):
    qi = pl.program_id(1)
    nq = TQ_FOX
    n_chunk = KV_TILE // UPD
    t_lane = qi * nq + lax.broadcasted_iota(jnp.int32, (1, nq), 1)
    k_local = lax.broadcasted_iota(jnp.int32, (UPD, nq), 0)
    q_t = jnp.transpose(q_ref[...].astype(F32))
    extra_row = lax.broadcasted_iota(jnp.int32, (HEAD_DIM, nq), 0)

    def ones_rows(h):
        return jnp.where((extra_row >= 3 * h) & (extra_row < 3 * h + 3), 1.0, 0.0)

    qts = [jnp.concatenate([q_t[HEAD_DIM * h:HEAD_DIM * (h + 1)], ones_rows(h)], axis=0).astype(BF16)
           for h in range(N_HEADS)]

    def scores(kt, c, h):
        start = pl.multiple_of((kt * n_chunk + c) * UPD, UPD)
        return _dot(k_ref[pl.ds(start, UPD), h * 128:(h + 1) * 128], qts[h])

    def value(kt, c, h):
        return vt_ref[kt * n_chunk + c, h * V_ROWS:(h + 1) * V_ROWS, :]

    def stage(kt, c, h):
        s = scores(kt, c, h)
        sbuf[h, c] = s
        mbuf[pl.ds(h * n_chunk + c, 1), :] = jnp.max(s, axis=0, keepdims=True)

    for c in range(n_chunk):
        for h in range(N_HEADS):
            stage(0, c, h)

    def tile(kt, carries):
        carries = list(carries)
        for c in range(n_chunk):
            for h in range(N_HEADS):
                s, s_max = sbuf[h, c], mbuf[pl.ds(h * n_chunk + c, 1), :]
                stage(kt + 1, c, h)
                carries[h] = _flash_t(s, *carries[h], value(kt, c, h), s_max)
        return tuple(carries)

    carries = list(_run_tiles(tile, qi, tuple(_flash_init(nq) for _ in range(N_HEADS))))
    for c in range(n_chunk):
        causal = qi * nq + c * UPD + k_local <= t_lane
        for h in range(N_HEADS):
            carries[h] = _flash_t(jnp.where(causal, sbuf[h, c], MASK_VALUE), *carries[h], value(qi, c, h))
    o_ref[...] = jnp.transpose(jnp.concatenate([acc / l for _, l, acc in carries], axis=0))


def _fox_call(fq, fk, fvt):
    B, T, _ = fq.shape
    tq = TQ_FOX
    return pl.pallas_call(
        _fox_kernel,
        grid=(B, T // tq),
        in_specs=[pl.BlockSpec((None, tq, 256), lambda b, i: (b, i, 0)),
                  pl.BlockSpec((None, T, 512), lambda b, i: (b, 0, 0)),
                  pl.BlockSpec((None, T // UPD, N_HEADS * V_ROWS, UPD), lambda b, i: (b, 0, 0, 0))],
        out_specs=pl.BlockSpec((None, tq, 256), lambda b, i: (b, i, 0)),
        out_shape=jax.ShapeDtypeStruct((B, T, 256), F32),
        scratch_shapes=[pltpu.VMEM((N_HEADS, KV_TILE // UPD, UPD, tq), F32),
                        pltpu.VMEM((N_HEADS * KV_TILE // UPD, tq), F32)],
        compiler_params=pltpu.CompilerParams(dimension_semantics=("arbitrary", "arbitrary"),
                                             vmem_limit_bytes=VMEM_LIMIT),
        name="fox",
    )(fq, fk, fvt)


def _merge_kernel(x_ref, g_ref, wg_ref, wm_ref, wb_ref, wo_ref, fg_ref, o0_ref, o1_ref, o2_ref, o3_ref,
                  out_ref, *, last, tm):
    rows = tm // 2
    halves = [slice(u * rows, (u + 1) * rows) for u in range(2)]
    xs = [x_ref[r, :] for r in halves]
    hs = [_rmsnorm(x, g_ref[...]).astype(BF16) for x in xs]
    accs = []
    for u, r in enumerate(halves):
        gates = _dot(hs[u], wg_ref[...])
        acc = None
        for i, o_ref in enumerate((o0_ref, o1_ref, o2_ref, o3_ref)):
            merge = _dot(hs[u], wm_ref[:, i * D_MODEL:(i + 1) * D_MODEL])
            gate = gates[:, i * BRANCH_WIDTH:(i + 1) * BRANCH_WIDTH]
            a = (o_ref[r, :] * (gate * _sigmoid(gate))).astype(BF16)
            term = _sigmoid(merge) * _dot(a, wb_ref[i])
            acc = term if acc is None else acc + term
        accs.append(acc)
    for u, r in enumerate(halves):
        y = xs[u] + _dot(accs[u].astype(BF16), wo_ref[...])
        if last:
            y = _rmsnorm(y, fg_ref[...])
        out_ref[r, :] = y


def _merge_call(x, g, wg, wm, wb, wo, fg, o_nsa, o_pool, o_conv, o_fox, last):
    B, T, _ = x.shape
    tm = TM_MERGE
    tok = lambda width: pl.BlockSpec((None, tm, width), lambda b, t: (b, t, 0))
    const = lambda shape: pl.BlockSpec(shape, lambda b, t: (0,) * len(shape))
    return pl.pallas_call(
        functools.partial(_merge_kernel, last=last, tm=tm),
        grid=(B, T // tm),
        in_specs=[tok(D_MODEL), const((1, D_MODEL)), const((D_MODEL, N_BRANCH * BRANCH_WIDTH)),
                  const((D_MODEL, N_BRANCH * D_MODEL)), const((N_BRANCH, BRANCH_WIDTH, D_MODEL)),
                  const((D_MODEL, D_MODEL)), const((1, D_MODEL)), tok(256), tok(256), tok(256), tok(256)],
        out_specs=tok(D_MODEL),
        out_shape=jax.ShapeDtypeStruct((B, T, D_MODEL), F32),
        compiler_params=pltpu.CompilerParams(dimension_semantics=("arbitrary", "arbitrary"),
                                             vmem_limit_bytes=VMEM_LIMIT),
        name="merge",
    )(x, g, wg, wm, wb, wo, fg, o_nsa, o_pool, o_conv, o_fox)


def _rope_tables(pos):
    n = pos.shape[0]
    inv = ROPE_THETA ** (-jnp.arange(ROPE_HALF, dtype=F32) / ROPE_HALF)
    ang = pos.astype(F32)[:, None] * inv[None, :]
    cos, sin = jnp.cos(ang), jnp.sin(ang)
    rest = HEAD_DIM - 2 * ROPE_HALF
    c = jnp.concatenate([cos, cos, jnp.ones((n, rest), F32)], axis=1)
    s = jnp.concatenate([-sin, sin, jnp.zeros((n, rest), F32)], axis=1)
    return jnp.concatenate([c, c], axis=1), jnp.concatenate([s, s], axis=1)


def _proj_weight(w_in):
    col = lambda name, width: w_in[..., _OFF[name]:_OFF[name] + width]
    z64 = jnp.zeros(w_in.shape[:-1] + (HEAD_DIM,), w_in.dtype)
    misc = jnp.concatenate([col("nsa_g", N_GATES), col("fox_f", N_HEADS),
                            jnp.zeros(w_in.shape[:-1] + (128 - N_MISC_T,), w_in.dtype)], axis=-1)
    fox_k = []
    for h in range(N_HEADS):
        fox_k += [w_in[..., _OFF["fox_k"] + h * HEAD_DIM:_OFF["fox_k"] + (h + 1) * HEAD_DIM], z64]
    w = jnp.concatenate([col("nsa_q", 256), col("k_s", 64), z64, col("k_w", 64), z64,
                         col("v_s", 64), col("v_w", 64), col("k_c", 128), misc,
                         col("fox_q", 256)] + fox_k + [col("fox_v", 256), col("pool", 256), col("conv", 768)],
                        axis=-1)
    assert w.shape[-1] == _P_COLS and w.dtype == BF16
    return w


def _compress_weights(cmp_pos, cmp_w1, cmp_b1, cmp_w2, cmp_b2):
    half = CMP_LEN // 2

    def w1_half(lo):
        wk = cmp_w1[0, lo * HEAD_DIM:(lo + half) * HEAD_DIM].reshape(half, HEAD_DIM, CMP_HIDDEN)
        wv = cmp_w1[1, lo * HEAD_DIM:(lo + half) * HEAD_DIM].reshape(half, HEAD_DIM, CMP_HIDDEN)
        z = jnp.zeros_like(wk)
        top = jnp.concatenate([wk, z], axis=2)
        bot = jnp.concatenate([z, wv], axis=2)
        return jnp.concatenate([top, bot], axis=1).astype(BF16)

    def pos_half(lo):
        return jnp.concatenate([cmp_pos[0, lo:lo + half], cmp_pos[1, lo:lo + half]], axis=1)

    lane = np.arange(HEAD_DIM)
    perm = np.where(lane < ROPE_HALF, lane + ROPE_HALF, np.where(lane < 2 * ROPE_HALF, lane - ROPE_HALF, lane))
    w2k, w2v = cmp_w2[0], cmp_w2[1]
    zk = jnp.zeros_like(w2k)
    w2 = jnp.concatenate([
        jnp.concatenate([w2k, w2k, w2k[:, perm], w2k[:, perm], zk, zk], axis=1),
        jnp.concatenate([zk, zk, zk, zk, w2v, w2v], axis=1)], axis=0).astype(BF16)
    b2k, b2v = cmp_b2[0], cmp_b2[1]
    b2 = jnp.concatenate([b2k, b2k, b2k[perm], b2k[perm], b2v, b2v])[None, :]
    b1 = jnp.concatenate([cmp_b1[0], cmp_b1[1]])[None, :]
    return pos_half(0), pos_half(half), w1_half(0), w1_half(half), b1, w2, b2


def _block_diag(pool_w):
    n, c, _ = pool_w.shape
    out = jnp.zeros((n * c, n * c), pool_w.dtype)
    for i in range(n):
        out = out.at[i * c:(i + 1) * c, i * c:(i + 1) * c].set(pool_w[i])
    return out


def kernel(x, norm_g, w_in, fox_f_bias, cmp_pos, cmp_w1, cmp_b1, cmp_w2, cmp_b2, pool_w, pool_scale, conv_w,
           w_branch, w_out, final_norm_g):
    B, T, _ = x.shape
    depth = norm_g.shape[0]
    assert T % TM_PROJ == 0 and T % KV_TILE == 0 and T // SEL_LEN <= HEAD_DIM and T >= WINDOW + CHUNK
    assert NSA_SUB * CHUNK == KV_TILE == TQ_FOX and TM_PROJ % KV_TILE == 0
    rows = T // CMP_STRIDE
    nblk = T // SEL_LEN

    cc_tok, ss_tok = _rope_tables(jnp.arange(T))
    cmp_end = jnp.arange(rows) * CMP_STRIDE + CMP_LEN - 1
    cc_cmp, ss_cmp = _rope_tables(cmp_end)
    ci = np.arange(rows)[:, None] * CMP_STRIDE
    sj = np.arange(nblk)[None, :] * SEL_LEN
    overlap_t = jnp.asarray(((ci < sj + SEL_LEN) & (ci + CMP_LEN > sj)).T, F32)
    tri = jnp.asarray(np.triu(np.ones((TM_PROJ, TM_PROJ), np.float32)), BF16)

    w_in_bf = w_in.astype(BF16)
    w_proj = _proj_weight(w_in_bf)
    w_gate = w_in_bf[..., _OFF["gate"]:_OFF["merge"]]
    w_merge = w_in_bf[..., _OFF["merge"]:_OFF["end"]]
    for l in range(depth):
        w1 = w_proj[l]
        fb = jnp.zeros((8, 1), F32).at[4:8, 0].set(fox_f_bias[l])
        (q, ks, kw, vst, vwt, kvc, mt, fq, fk, fvt, o_pool, o_conv) = _proj_call(
            x, norm_g[l][None, :], w1, cc_tok, ss_tok, fb, conv_w[l], _block_diag(pool_w[l]).astype(BF16),
            pool_scale[l][None, :], tri)
        kc, vct = _compress_call(kvc, *_compress_weights(cmp_pos[l], cmp_w1[l], cmp_b1[l], cmp_w2[l], cmp_b2[l]),
                                 cc_cmp, ss_cmp)
        o_nsa = _nsa_call(q, ks, kw, vst, vwt, kc, vct, mt, overlap_t)
        o_fox = _fox_call(fq, fk, fvt)
        x = _merge_call(x, norm_g[l][None, :], w_gate[l], w_merge[l], w_branch[l].astype(BF16), w_out[l].astype(BF16),
                        final_norm_g[None, :], o_nsa, o_pool, o_conv, o_fox, last=(l == depth - 1))
    return x
```

```python
import functools

import numpy as np
import jax
import jax.numpy as jnp
from jax import lax
from jax.experimental import pallas as pl
from jax.experimental.pallas import tpu as pltpu

F32 = jnp.float32
BF16 = jnp.bfloat16

D_MODEL = 1024
N_BRANCH = 4
BRANCH_WIDTH = 256
HEAD_DIM = 64
N_HEADS = 4
CMP_LEN = 32
CMP_STRIDE = 16
CMP_HIDDEN = 128
CMP_GROUP = 8
SEL_LEN = 64
N_SEL = 16
RANK_LEVELS = (16, 24, 32, 40, 48, 56, 64)
WINDOW = 512
FORCE_BONUS = 1.0e4
ROPE_THETA = 500000.0
ROPE_HALF = 8
NORM_EPS = 1e-6
MASK_VALUE = -1e30

_OFF = dict(nsa_q=0, k_c=256, v_c=320, k_s=384, v_s=448, k_w=512, v_w=576, nsa_g=640, pool=652,
            conv=908, fox_q=1676, fox_k=1932, fox_v=2188, fox_f=2444, gate=2448, merge=3472, end=7568)

_P_ROPE = 0
_P_VSW = 512
_P_FOXQ = 896
_P_FOXK = 1152
_P_FOXV = 1664
_P_POOL = 1920
_P_CONV = 2176
_P_COLS = 2944

CHUNK = 128
UPD = 256
V_ROWS = HEAD_DIM + 16
LOG2E = 1.4426950408889634
Q_SCALE = HEAD_DIM ** -0.5 * LOG2E
TM_PROJ = 512
KV_TILE = 512
NSA_SUB = 4
TQ_FOX = 512
TM_MERGE = 512
N_GATES = 3 * N_HEADS
N_MISC_T = N_GATES + N_HEADS
POOL_HALO = 16
CONV_HALO = 8
VMEM_LIMIT = 56 * 1024 * 1024


def _dot(a, b, precision=None):
    return jnp.dot(a, b, preferred_element_type=F32, precision=precision)


def _sigmoid(x):
    return 1.0 / (1.0 + jnp.exp(-x))


def _rmsnorm(x, g):
    return x * lax.rsqrt(jnp.mean(x * x, axis=-1, keepdims=True) + NORM_EPS) * g


def _flash_t(s, m, l, acc, v_t, s_max=None):
    m_new = jnp.maximum(m, jnp.max(s, axis=0, keepdims=True) if s_max is None else s_max)
    alpha = jnp.exp2(m - m_new)
    pv = _dot(v_t, jnp.exp2(s - m_new).astype(BF16))
    l = alpha * l + pv[HEAD_DIM:HEAD_DIM + 1]
    acc = alpha * acc + pv[0:HEAD_DIM]
    return m_new, l, acc


def _attend_chunks(streams, n):
    nxt = [st[0](0) for st in streams]
    carries = [st[2] for st in streams]
    for c in range(n):
        cur = nxt
        if c + 1 < n:
            nxt = [st[0](c + 1) for st in streams]
        carries = [_flash_t(cur[i], *carries[i], streams[i][1](c)) for i in range(len(streams))]
    return carries


def _run_tiles(tile, n, carries):
    def run(first, count, carries):
        for j in range(count):
            carries = tile(first + j, carries)
        return carries

    quads = n // 4
    carries = lax.fori_loop(0, quads, lambda i, cs: run(4 * i, 4, cs), carries)
    pairs = (n - 4 * quads) // 2
    carries = lax.fori_loop(0, pairs, lambda i, cs: run(4 * quads, 2, cs), carries)
    return lax.fori_loop(4 * quads + 2 * pairs, n, tile, carries)


def _flash_init(n):
    return (jnp.full((1, n), MASK_VALUE, F32), jnp.zeros((1, n), F32), jnp.zeros((HEAD_DIM, n), F32))


def _proj_kernel(x_ref, xn_ref, g_ref, w_ref, cc_ref, ss_ref, fb_ref, cw_ref, pw_ref, ps_ref, tri_ref,
                 q_ref, ks_ref, kw_ref, vst_ref, vwt_ref, kvc_ref, mt_ref, fq_ref, fk_ref, fvt_ref,
                 opool_ref, oconv_ref, pext, cext, ccarry, hbuf, *, tm):
    ti = pl.program_id(1)

    @pl.when((pl.program_id(0) == 0) & (ti == 0))
    def _():
        hbuf[...] = _rmsnorm(x_ref[...], g_ref[...]).astype(BF16)

    h = hbuf[...]

    def seg(lo, width):
        return _dot(h, w_ref[:, lo:lo + width])

    lane = lax.broadcasted_iota(jnp.int32, (tm, 128), 1)
    row = lax.broadcasted_iota(jnp.int32, (tm, 128), 0) + ti * tm
    first_half = (lane % HEAD_DIM) < ROPE_HALF
    cc = cc_ref[...]
    ss = ss_ref[...]

    def rope(xc):
        partner = jnp.where(first_half, pltpu.roll(xc, 128 - ROPE_HALF, 1), pltpu.roll(xc, ROPE_HALF, 1))
        return xc * cc + partner * ss

    @pl.when(ti == 0)
    def _():
        ccarry[...] = jnp.zeros_like(ccarry)
        pext[0:POOL_HALO, :] = jnp.zeros((POOL_HALO, BRANCH_WIDTH), F32)
        cext[0:CONV_HALO, :] = jnp.zeros((CONV_HALO, BRANCH_WIDTH), F32)

    g_small = seg(_P_VSW, 384)
    g_rope = seg(_P_ROPE, 512)

    z_t = jnp.transpose(g_small[:, 256:384])[0:N_MISC_T, :]
    gates = _sigmoid(z_t)
    row8 = lax.broadcasted_iota(jnp.int32, (8, tm), 0)
    zb = z_t[8:16] + fb_ref[...]
    logf = jnp.where(row8 >= N_GATES - 8, jnp.minimum(zb, 0.0) - jnp.log(1.0 + jnp.exp(-jnp.abs(zb))), 0.0)

    def split3(v):
        hi = v.astype(BF16).astype(F32)
        mid = (v - hi).astype(BF16).astype(F32)
        return hi, mid, (v - hi) - mid

    parts = jnp.concatenate(list(split3(logf)) + [jnp.zeros((8, tm), F32)], axis=0).astype(BF16)
    part_sums = _dot(parts, tri_ref[...])
    csum = (part_sums[0:8] + part_sums[8:16]) + part_sums[16:24] + ccarry[:, 0:1]
    ccarry[...] = jnp.broadcast_to(csum[:, tm - 1:tm], (8, 128))
    mt_ref[0:8, :] = gates[0:8]
    mt_ref[8:16, :] = jnp.where(row8 < N_GATES - 8, gates[8:16], csum)

    g_pool = seg(_P_POOL, 256)
    cv = seg(_P_CONV, 768)

    pext[POOL_HALO:POOL_HALO + tm, :] = g_pool

    def pld(k, c):
        return pext[pl.ds(POOL_HALO - k, tm), c * 128:(c + 1) * 128]

    left = lane < 64
    rowp1 = (row + 1).astype(F32)

    def cnt(w):
        return jnp.minimum(rowp1, float(w))

    e0 = pld(0, 0)
    s2 = e0 + pld(1, 0)
    s4 = s2 + pld(2, 0) + pld(3, 0)
    p0 = jnp.where(left, s2 / cnt(2), s4 / cnt(4)) - e0
    f0 = pld(0, 1)
    s8 = f0
    for k in range(1, 8):
        s8 = s8 + pld(k, 1)
    s16 = s8
    for k in range(8, 16):
        s16 = s16 + pld(k, 1)
    p1 = jnp.where(left, s8 / cnt(8), s16 / cnt(16)) - f0
    pw = pw_ref[...]
    mixed = _dot(p0.astype(BF16), pw[0:128, :]) + _dot(p1.astype(BF16), pw[128:256, :])
    opool_ref[...] = mixed * ps_ref[...]
    pext[0:POOL_HALO, :] = pext[tm:tm + POOL_HALO, :]

    g_fv = seg(_P_FOXV, 256)
    g_fk = seg(_P_FOXK, 512)
    g_fq = seg(_P_FOXQ, 256)

    q_ref[:, 0:128] = (rope(g_rope[:, 0:128]) * Q_SCALE).astype(BF16)
    q_ref[:, 128:256] = (rope(g_rope[:, 128:256]) * Q_SCALE).astype(BF16)
    block_onehot = jnp.where(lane - HEAD_DIM == row // SEL_LEN, 1.0, 0.0)
    ks_ref[...] = (rope(g_rope[:, 256:384]) + block_onehot).astype(BF16)
    kw_ref[...] = rope(g_rope[:, 384:512]).astype(BF16)
    vsw_t = jnp.transpose(g_small[:, 0:128]).astype(BF16)
    ones_upd = jnp.ones((V_ROWS - HEAD_DIM, UPD), BF16)
    for c in range(tm // UPD):
        vst_ref[c] = jnp.concatenate([vsw_t[0:HEAD_DIM, c * UPD:(c + 1) * UPD], ones_upd], axis=0)
    for c in range(tm // CHUNK):
        vwt_ref[c] = jnp.concatenate([vsw_t[HEAD_DIM:128, c * CHUNK:(c + 1) * CHUNK], ones_upd[:, 0:CHUNK]], axis=0)
    kvc_ref[...] = g_small[:, 128:256]

    fq_ref[...] = (g_fq * Q_SCALE).astype(BF16)
    terms = split3(csum * (-LOG2E))
    row16 = lax.broadcasted_iota(jnp.int32, (16, tm), 0)
    extra_t = jnp.zeros((16, tm), F32)
    for hd in range(N_HEADS):
        for j in range(3):
            src = jnp.broadcast_to(terms[j][N_GATES - 8 + hd:N_GATES - 7 + hd, :], (16, tm))
            extra_t = jnp.where(row16 == 3 * hd + j, src, extra_t)
    extra = jnp.transpose(jnp.concatenate(
        [jnp.zeros((HEAD_DIM, tm), F32), extra_t, jnp.zeros((128 - HEAD_DIM - 16, tm), F32)], axis=0))
    for hd in range(N_HEADS):
        fk_ref[:, hd * 128:(hd + 1) * 128] = (g_fk[:, hd * 128:(hd + 1) * 128] + extra).astype(BF16)
    fv_t = jnp.transpose(g_fv).astype(BF16)
    for c in range(tm // UPD):
        cols = slice(c * UPD, (c + 1) * UPD)
        fvt_ref[c] = jnp.concatenate(
            [piece for hd in range(N_HEADS) for piece in (fv_t[hd * HEAD_DIM:(hd + 1) * HEAD_DIM, cols], ones_upd)],
            axis=0)

    u = cv[:, 512:768] * cv[:, 0:256]
    cext[CONV_HALO:CONV_HALO + tm, :] = u
    y = (cext[pl.ds(CONV_HALO - 2, tm), :] * cw_ref[0:1, :]
         + cext[pl.ds(CONV_HALO - 1, tm), :] * cw_ref[1:2, :]
         + u * cw_ref[2:3, :])
    oconv_ref[...] = cv[:, 256:512] * y
    cext[0:CONV_HALO, :] = cext[tm:tm + CONV_HALO, :]

    hbuf[...] = _rmsnorm(xn_ref[...], g_ref[...]).astype(BF16)


def _proj_call(x, g, w1, cc, ss, fb, cw, pw, ps, tri):
    B, T, _ = x.shape
    tm = TM_PROJ
    nt = T // tm
    tok = lambda width: pl.BlockSpec((None, tm, width), lambda b, t: (b, t, 0))
    chunked = lambda rows, keys: pl.BlockSpec((None, tm // keys, rows, keys), lambda b, t: (b, t, 0, 0))
    const = lambda shape: pl.BlockSpec(shape, lambda b, t: (0,) * len(shape))
    out_shape = [
        jax.ShapeDtypeStruct((B, T, 256), BF16),
        jax.ShapeDtypeStruct((B, T, 128), BF16),
        jax.ShapeDtypeStruct((B, T, 128), BF16),
        jax.ShapeDtypeStruct((B, T // UPD, V_ROWS, UPD), BF16),
        jax.ShapeDtypeStruct((B, T // CHUNK, V_ROWS, CHUNK), BF16),
        jax.ShapeDtypeStruct((B, T, 128), F32),
        jax.ShapeDtypeStruct((B, nt, N_MISC_T, tm), F32),
        jax.ShapeDtypeStruct((B, T, 256), BF16),
        jax.ShapeDtypeStruct((B, T, 512), BF16),
        jax.ShapeDtypeStruct((B, T // UPD, N_HEADS * V_ROWS, UPD), BF16),
        jax.ShapeDtypeStruct((B, T, 256), F32),
        jax.ShapeDtypeStruct((B, T, 256), F32),
    ]
    out_specs = [tok(256), tok(128), tok(128), chunked(V_ROWS, UPD), chunked(V_ROWS, CHUNK), tok(128),
                 pl.BlockSpec((None, None, N_MISC_T, tm), lambda b, t: (b, t, 0, 0)),
                 tok(256), tok(512), chunked(N_HEADS * V_ROWS, UPD), tok(256), tok(256)]
    def next_tile(b, t):
        lin = jnp.minimum(b * nt + t + 1, B * nt - 1)
        return lin // nt, lin % nt, 0

    in_specs = [tok(D_MODEL), pl.BlockSpec((None, tm, D_MODEL), next_tile),
                const((1, D_MODEL)), const((D_MODEL, _P_COLS)),
                pl.BlockSpec((tm, 128), lambda b, t: (t, 0)), pl.BlockSpec((tm, 128), lambda b, t: (t, 0)),
                const((8, 1)), const((3, 256)), const((256, 256)), const((1, 256)), const((tm, tm))]
    return pl.pallas_call(
        functools.partial(_proj_kernel, tm=tm),
        grid=(B, nt),
        in_specs=in_specs,
        out_specs=out_specs,
        out_shape=out_shape,
        scratch_shapes=[pltpu.VMEM((tm + POOL_HALO, 256), F32), pltpu.VMEM((tm + CONV_HALO, 256), F32),
                        pltpu.VMEM((8, 128), F32), pltpu.VMEM((tm, D_MODEL), BF16)],
        compiler_params=pltpu.CompilerParams(dimension_semantics=("arbitrary", "arbitrary"),
                                             vmem_limit_bytes=VMEM_LIMIT),
        name="proj",
    )(x, x, g, w1, cc, ss, fb, cw, pw, ps, tri)


def _compress_kernel(x_ref, pt_ref, pb_ref, wt_ref, wb_ref, b1_ref, w2_ref, b2_ref, cc_ref, ss_ref,
                     kc_ref, vct_ref, *, rows):
    a = b = None
    for l in range(CMP_STRIDE):
        x_l = x_ref[pl.ds(l, rows, stride=CMP_STRIDE), :]
        a_l = _dot((x_l + pt_ref[l:l + 1, :]).astype(BF16), wt_ref[l])
        b_l = _dot((x_l + pb_ref[l:l + 1, :]).astype(BF16), wb_ref[l])
        a, b = (a_l, b_l) if a is None else (a + a_l, b + b_l)
    hid = a + pltpu.roll(b, rows - 1, 0) + b1_ref[...]
    act = hid * _sigmoid(hid)
    out = _dot(act.astype(BF16), w2_ref[...]) + b2_ref[...]
    lane = lax.broadcasted_iota(jnp.int32, (rows, 128), 1)
    roped = out[:, 0:128] * cc_ref[...] + out[:, 128:256] * ss_ref[...]
    block = lax.broadcasted_iota(jnp.int32, (rows, 128), 0)
    group_onehot = jnp.where(lane - HEAD_DIM == block // CMP_GROUP, 1.0, 0.0)
    kc_ref[...] = jnp.where(lane < HEAD_DIM, roped, group_onehot).astype(BF16)
    vct_ref[...] = jnp.transpose(out[:, 256:384])[0:HEAD_DIM, :].astype(BF16)


def _compress_call(kvc, pt, pb, wt, wb, b1, w2, b2, cc, ss):
    B, T, width = kvc.shape
    rows = T // CMP_STRIDE
    const = lambda shape: pl.BlockSpec(shape, lambda b: (0,) * len(shape))
    return pl.pallas_call(
        functools.partial(_compress_kernel, rows=rows),
        grid=(B,),
        in_specs=[pl.BlockSpec((None, T, width), lambda b: (b, 0, 0)),
                  const((CMP_STRIDE, width)), const((CMP_STRIDE, width)),
                  const((CMP_STRIDE, width, 256)), const((CMP_STRIDE, width, 256)),
                  const((1, 256)), const((256, 384)), const((1, 384)), const((rows, 128)), const((rows, 128))],
        out_specs=[pl.BlockSpec((None, rows, 128), lambda b: (b, 0, 0)),
                   pl.BlockSpec((None, HEAD_DIM, rows), lambda b: (b, 0, 0))],
        out_shape=[jax.ShapeDtypeStruct((B, rows, 128), BF16), jax.ShapeDtypeStruct((B, HEAD_DIM, rows), BF16)],
        compiler_params=pltpu.CompilerParams(dimension_semantics=("arbitrary",),
                                             vmem_limit_bytes=VMEM_LIMIT),
        name="compress",
    )(kvc, pt, pb, wt, wb, b1, w2, b2, cc, ss)


def _nsa_kernel(q_ref, ks_ref, kw_ref, vst_ref, vwt_ref, kc_ref, vct_ref, mt_ref, ovt_ref,
                o_ref, key_ref, sbuf, cbuf, mbuf, *, nblk, ncmp):
    step = pl.program_id(1)
    tq = CHUNK
    nq = N_HEADS * tq
    hw = nq // 2
    halves = (slice(0, hw), slice(hw, nq))
    n_c = KV_TILE // UPD
    n_win = WINDOW // tq
    subs = range(NSA_SUB)
    qis = [step * NSA_SUB + u for u in subs]
    q_local = lax.broadcasted_iota(jnp.int32, (1, nq), 1) % tq
    k_local = lax.broadcasted_iota(jnp.int32, (tq, hw), 0)
    q_loc = q_local[:, 0:hw]
    causal = k_local <= q_loc

    q4, qt_plain = [], []
    group = lax.broadcasted_iota(jnp.int32, (HEAD_DIM, nq), 0)
    slab = 2 * CMP_GROUP
    n_slab = lax.broadcasted_iota(jnp.int32, (slab, nq), 0)
    for u in subs:
        q_t = jnp.transpose(q_ref[u * tq:(u + 1) * tq, :].astype(F32))
        q4.append(jnp.concatenate([q_t[HEAD_DIM * h:HEAD_DIM * (h + 1)] for h in range(N_HEADS)], axis=1))
        qt_plain.append(jnp.concatenate([q4[u], jnp.zeros((HEAD_DIM, nq), F32)], axis=0).astype(BF16))
        group_bias = jnp.where(group > qis[u], MASK_VALUE, 0.0)
        qt_cmp = jnp.concatenate([q4[u], group_bias], axis=0).astype(BF16)
        cbuf[u] = _dot(kc_ref[...], qt_cmp)
        start = pl.multiple_of(jnp.maximum(qis[u] - 1, 0) * CMP_GROUP, CMP_GROUP)
        visible = ((start + n_slab) * CMP_STRIDE + (CMP_LEN - 1)) <= qis[u] * tq + q_local
        cbuf[u, pl.ds(start, slab), :] = jnp.where(visible, cbuf[u, pl.ds(start, slab), :], MASK_VALUE)

    o_cmp, imp = [], []
    for u in subs:
        s = cbuf[u]
        mc = jnp.max(s, axis=0, keepdims=True)
        mc = jnp.where(mc <= MASK_VALUE, 0.0, mc)
        pc = jnp.exp2(s - mc)
        lc = jnp.sum(pc, axis=0, keepdims=True)
        pc = pc * (1.0 / jnp.where(lc > 0.0, lc, 1.0))
        o_cmp.append(_dot(vct_ref[...], pc.astype(BF16)))
        psum = (pc[:, 0:tq] + pc[:, tq:2 * tq]) + (pc[:, 2 * tq:3 * tq] + pc[:, 3 * tq:4 * tq])
        imp.append(_dot(ovt_ref[...], psum, precision=lax.Precision.HIGHEST))

    jb = lax.broadcasted_iota(jnp.int32, (nblk, tq), 0)
    sub8 = lax.broadcasted_iota(jnp.int32, (8, tq), 0)
    later_rows = [sub8 > r for r in range(8)]
    keys = []
    for u in subs:
        tt = qis[u] * tq + lax.broadcasted_iota(jnp.int32, (nblk, tq), 1)
        jt = tt // SEL_LEN
        forced = (jb == 0) | (jb == jt) | (jb == jt - 1)
        valid = jb * SEL_LEN <= tt
        keys.append(jnp.where(valid, jnp.where(forced, imp[u] + FORCE_BONUS, imp[u]), MASK_VALUE))
        key_ref[u] = keys[u]

    def select_bias(n_rank):
        out = []
        for u in subs:
            key_rows = [keys[u][8 * v:8 * v + 8] for v in range(n_rank // 8)]
            ranks = [jnp.zeros((8, tq), F32) for _ in key_rows]
            for jp in range(n_rank):
                other = jnp.broadcast_to(key_ref[u, pl.ds(jp, 1), :], (8, tq))
                for v, kv in enumerate(key_rows):
                    if 8 * v > jp:
                        ahead = other >= kv
                    elif 8 * v + 7 <= jp:
                        ahead = other > kv
                    else:
                        ahead = (other > kv) | ((other == kv) & later_rows[jp - 8 * v])
                    ranks[v] = ranks[v] + jnp.where(ahead, 1.0, 0.0)
            bias = jnp.where(jnp.concatenate(ranks, axis=0) < float(min(N_SEL, nblk)), 0.0, MASK_VALUE)
            out.append(jnp.concatenate([bias, jnp.full((HEAD_DIM - n_rank, tq), MASK_VALUE, F32)], axis=0)
                       if n_rank < HEAD_DIM else bias)
        return tuple(out)

    visible = (step + 1) * (KV_TILE // SEL_LEN)
    levels = sorted({min(nblk, r) for r in RANK_LEVELS})
    level = sum((visible > r).astype(jnp.int32) for r in levels[:-1]) if len(levels) > 1 else 0
    sel_bias = lax.switch(level, [functools.partial(select_bias, r) for r in levels])
    qt_sel = [jnp.concatenate([q4[u], jnp.concatenate([sel_bias[u]] * N_HEADS, axis=1)], axis=0).astype(BF16)
              for u in subs]

    def win_stream(u, hs):
        qi = qis[u]

        def score(c):
            if c == 0:
                own = kw_ref[pl.ds(pl.multiple_of(qi * tq, tq), tq), :]
                return jnp.where(causal, _dot(own, qt_plain[u][:, hs]), MASK_VALUE)
            ci = qi - n_win + (c - 1)
            start = pl.multiple_of(jnp.maximum(ci, 0) * tq, tq)
            s = _dot(kw_ref[pl.ds(start, tq), :], qt_plain[u][:, hs])
            ok = ((k_local > q_loc) & (ci >= 0)) if c == 1 else jnp.broadcast_to(ci >= 0, (tq, hw))
            return jnp.where(ok, s, MASK_VALUE)

        def value(c):
            return vwt_ref[qi] if c == 0 else vwt_ref[jnp.maximum(qi - n_win + (c - 1), 0)]

        return score, value, _flash_init(hw)

    carries_w = _attend_chunks([win_stream(u, hs) for u in subs for hs in halves], n_win + 1)
    o_win = [jnp.concatenate([acc / l for _, l, acc in carries_w[2 * u:2 * u + 2]], axis=1) for u in subs]

    def slc_scores(u, kt, c):
        start = pl.multiple_of((kt * n_c + c) * UPD, UPD)
        return _dot(ks_ref[pl.ds(start, UPD), :], qt_sel[u])

    def slc_update(u, carry, s, kt, c, s_max=None):
        return tuple(_flash_t(s[:, hs], *carry[i], vst_ref[kt * n_c + c], None if s_max is None else s_max[:, hs])
                     for i, hs in enumerate(halves))

    def stage(u, kt, c):
        s = slc_scores(u, kt, c)
        sbuf[u, c] = s
        mbuf[pl.ds(u * n_c + c, 1), :] = jnp.max(s, axis=0, keepdims=True)

    for c in range(n_c):
        for u in subs:
            stage(u, 0, c)

    def slc_tile(kt, carries):
        carries = list(carries)
        for c in range(n_c):
            for u in subs:
                s, s_max = sbuf[u, c], mbuf[pl.ds(u * n_c + c, 1), :]
                stage(u, kt + 1, c)
                carries[u] = slc_update(u, carries[u], s, kt, c, s_max)
        return tuple(carries)

    carries = list(_run_tiles(slc_tile, step, tuple((_flash_init(hw), _flash_init(hw)) for u in subs)))
    k_upd = lax.broadcasted_iota(jnp.int32, (UPD, nq), 0)
    for c in range(n_c):
        first, last = c * UPD // tq, (c + 1) * UPD // tq - 1
        for u in subs:
            if first > u:
                continue
            s, s_max = sbuf[u, c], mbuf[pl.ds(u * n_c + c, 1), :]
            if last >= u:
                s, s_max = jnp.where(c * UPD + k_upd <= u * tq + q_local, s, MASK_VALUE), None
            carries[u] = slc_update(u, carries[u], s, step, c, s_max)

    for u in subs:
        o_slc = jnp.concatenate([acc / l for _, l, acc in carries[u]], axis=1)
        g = mt_ref[:, u * tq:(u + 1) * tq]
        heads = []
        for h in range(N_HEADS):
            cols = slice(h * tq, (h + 1) * tq)
            heads.append(g[h:h + 1] * o_cmp[u][:, cols] + g[N_HEADS + h:N_HEADS + h + 1] * o_slc[:, cols]
                         + g[2 * N_HEADS + h:2 * N_HEADS + h + 1] * o_win[u][:, cols])
        o_ref[u * tq:(u + 1) * tq, :] = jnp.transpose(jnp.concatenate(heads, axis=0))


def _nsa_call(q, ks, kw, vst, vwt, kc, vct, mt, ovt):
    B, T, _ = q.shape
    nblk, ncmp = ovt.shape
    tq = CHUNK * NSA_SUB
    per_mt = TM_PROJ // tq
    full = lambda width: pl.BlockSpec((None, T, width), lambda b, i: (b, 0, 0))
    vfull = lambda keys: pl.BlockSpec((None, T // keys, V_ROWS, keys), lambda b, i: (b, 0, 0, 0))
    return pl.pallas_call(
        functools.partial(_nsa_kernel, nblk=nblk, ncmp=ncmp),
        grid=(B, T // tq),
        in_specs=[pl.BlockSpec((None, tq, 256), lambda b, i: (b, i, 0)),
                  full(128), full(128), vfull(UPD), vfull(CHUNK),
                  pl.BlockSpec((None, ncmp, 128), lambda b, i: (b, 0, 0)),
                  pl.BlockSpec((None, HEAD_DIM, ncmp), lambda b, i: (b, 0, 0)),
                  pl.BlockSpec((None, None, N_MISC_T, tq), lambda b, i: (b, i // per_mt, 0, i % per_mt)),
                  pl.BlockSpec((nblk, ncmp), lambda b, i: (0, 0))],
        out_specs=pl.BlockSpec((None, tq, 256), lambda b, i: (b, i, 0)),
        out_shape=jax.ShapeDtypeStruct((B, T, 256), F32),
        scratch_shapes=[pltpu.VMEM((NSA_SUB, nblk, CHUNK), F32),
                        pltpu.VMEM((NSA_SUB, KV_TILE // UPD, UPD, N_HEADS * CHUNK), F32),
                        pltpu.VMEM((NSA_SUB, ncmp, N_HEADS * CHUNK), F32),
                        pltpu.VMEM((NSA_SUB * KV_TILE // UPD, N_HEADS * CHUNK), F32)],
        compiler_params=pltpu.CompilerParams(dimension_semantics=("arbitrary", "arbitrary"),
                                             vmem_limit_bytes=VMEM_LIMIT),
        name="nsa",
    )(q, ks, kw, vst, vwt, kc, vct, mt, ovt)


def _fox_kernel(q_ref, k_ref, vt_ref, o_ref, sbuf, mbuf):
    qi = pl.program_id(1)
    nq = TQ_FOX
    n_chunk = KV_TILE // UPD
    t_lane = qi * nq + lax.broadcasted_iota(jnp.int32, (1, nq), 1)
    k_local = lax.broadcasted_iota(jnp.int32, (UPD, nq), 0)
    q_t = jnp.transpose(q_ref[...].astype(F32))
    extra_row = lax.broadcasted_iota(jnp.int32, (HEAD_DIM, nq), 0)

    def ones_rows(h):
        return jnp.where((extra_row >= 3 * h) & (extra_row < 3 * h + 3), 1.0, 0.0)

    qts = [jnp.concatenate([q_t[HEAD_DIM * h:HEAD_DIM * (h + 1)], ones_rows(h)], axis=0).astype(BF16)
           for h in range(N_HEADS)]

    def scores(kt, c, h):
        start = pl.multiple_of((kt * n_chunk + c) * UPD, UPD)
        return _dot(k_ref[pl.ds(start, UPD), h * 128:(h + 1) * 128], qts[h])

    def value(kt, c, h):
        return vt_ref[kt * n_chunk + c, h * V_ROWS:(h + 1) * V_ROWS, :]

    def stage(kt, c, h):
        s = scores(kt, c, h)
        sbuf[h, c] = s
        mbuf[pl.ds(h * n_chunk + c, 1), :] = jnp.max(s, axis=0, keepdims=True)

    for c in range(n_chunk):
        for h in range(N_HEADS):
            stage(0, c, h)

    def tile(kt, carries):
        carries = list(carries)
        for c in range(n_chunk):
            for h in range(N_HEADS):
                s, s_max = sbuf[h, c], mbuf[pl.ds(h * n_chunk + c, 1), :]
                stage(kt + 1, c, h)
                carries[h] = _flash_t(s, *carries[h], value(kt, c, h), s_max)
        return tuple(carries)

    carries = list(_run_tiles(tile, qi, tuple(_flash_init(nq) for _ in range(N_HEADS))))
    for c in range(n_chunk):
        causal = qi * nq + c * UPD + k_local <= t_lane
        for h in range(N_HEADS):
            carries[h] = _flash_t(jnp.where(causal, sbuf[h, c], MASK_VALUE), *carries[h], value(qi, c, h))
    o_ref[...] = jnp.transpose(jnp.concatenate([acc / l for _, l, acc in carries], axis=0))


def _fox_call(fq, fk, fvt):
    B, T, _ = fq.shape
    tq = TQ_FOX
    return pl.pallas_call(
        _fox_kernel,
        grid=(B, T // tq),
        in_specs=[pl.BlockSpec((None, tq, 256), lambda b, i: (b, i, 0)),
                  pl.BlockSpec((None, T, 512), lambda b, i: (b, 0, 0)),
                  pl.BlockSpec((None, T // UPD, N_HEADS * V_ROWS, UPD), lambda b, i: (b, 0, 0, 0))],
        out_specs=pl.BlockSpec((None, tq, 256), lambda b, i: (b, i, 0)),
        out_shape=jax.ShapeDtypeStruct((B, T, 256), F32),
        scratch_shapes=[pltpu.VMEM((N_HEADS, KV_TILE // UPD, UPD, tq), F32),
                        pltpu.VMEM((N_HEADS * KV_TILE // UPD, tq), F32)],
        compiler_params=pltpu.CompilerParams(dimension_semantics=("arbitrary", "arbitrary"),
                                             vmem_limit_bytes=VMEM_LIMIT),
        name="fox",
    )(fq, fk, fvt)


def _merge_kernel(x_ref, g_ref, wg_ref, wm_ref, wb_ref, wo_ref, fg_ref, o0_ref, o1_ref, o2_ref, o3_ref,
                  out_ref, *, last, tm):
    rows = tm // 4
    halves = [slice(u * rows, (u + 1) * rows) for u in range(4)]
    xs = [x_ref[r, :] for r in halves]
    hs = [_rmsnorm(x, g_ref[...]).astype(BF16) for x in xs]
    accs = []
    for u, r in enumerate(halves):
        gates = _dot(hs[u], wg_ref[...])
        acc = None
        for i, o_ref in enumerate((o0_ref, o1_ref, o2_ref, o3_ref)):
            merge = _dot(hs[u], wm_ref[:, i * D_MODEL:(i + 1) * D_MODEL])
            gate = gates[:, i * BRANCH_WIDTH:(i + 1) * BRANCH_WIDTH]
            a = (o_ref[r, :] * (gate * _sigmoid(gate))).astype(BF16)
            term = _sigmoid(merge) * _dot(a, wb_ref[i])
            acc = term if acc is None else acc + term
        accs.append(acc)
    for u, r in enumerate(halves):
        y = xs[u] + _dot(accs[u].astype(BF16), wo_ref[...])
        if last:
            y = _rmsnorm(y, fg_ref[...])
        out_ref[r, :] = y


def _merge_call(x, g, wg, wm, wb, wo, fg, o_nsa, o_pool, o_conv, o_fox, last):
    B, T, _ = x.shape
    tm = TM_MERGE
    tok = lambda width: pl.BlockSpec((None, tm, width), lambda b, t: (b, t, 0))
    const = lambda shape: pl.BlockSpec(shape, lambda b, t: (0,) * len(shape))
    return pl.pallas_call(
        functools.partial(_merge_kernel, last=last, tm=tm),
        grid=(B, T // tm),
        in_specs=[tok(D_MODEL), const((1, D_MODEL)), const((D_MODEL, N_BRANCH * BRANCH_WIDTH)),
                  const((D_MODEL, N_BRANCH * D_MODEL)), const((N_BRANCH, BRANCH_WIDTH, D_MODEL)),
                  const((D_MODEL, D_MODEL)), const((1, D_MODEL)), tok(256), tok(256), tok(256), tok(256)],
        out_specs=tok(D_MODEL),
        out_shape=jax.ShapeDtypeStruct((B, T, D_MODEL), F32),
        compiler_params=pltpu.CompilerParams(dimension_semantics=("arbitrary", "arbitrary"),
                                             vmem_limit_bytes=VMEM_LIMIT),
        name="merge",
    )(x, g, wg, wm, wb, wo, fg, o_nsa, o_pool, o_conv, o_fox)


def _rope_tables(pos):
    n = pos.shape[0]
    inv = ROPE_THETA ** (-jnp.arange(ROPE_HALF, dtype=F32) / ROPE_HALF)
    ang = pos.astype(F32)[:, None] * inv[None, :]
    cos, sin = jnp.cos(ang), jnp.sin(ang)
    rest = HEAD_DIM - 2 * ROPE_HALF
    c = jnp.concatenate([cos, cos, jnp.ones((n, rest), F32)], axis=1)
    s = jnp.concatenate([-sin, sin, jnp.zeros((n, rest), F32)], axis=1)
    return jnp.concatenate([c, c], axis=1), jnp.concatenate([s, s], axis=1)


def _proj_weight(w_in):
    col = lambda name, width: w_in[..., _OFF[name]:_OFF[name] + width]
    z64 = jnp.zeros(w_in.shape[:-1] + (HEAD_DIM,), w_in.dtype)
    misc = jnp.concatenate([col("nsa_g", N_GATES), col("fox_f", N_HEADS),
                            jnp.zeros(w_in.shape[:-1] + (128 - N_MISC_T,), w_in.dtype)], axis=-1)
    fox_k = []
    for h in range(N_HEADS):
        fox_k += [w_in[..., _OFF["fox_k"] + h * HEAD_DIM:_OFF["fox_k"] + (h + 1) * HEAD_DIM], z64]
    w = jnp.concatenate([col("nsa_q", 256), col("k_s", 64), z64, col("k_w", 64), z64,
                         col("v_s", 64), col("v_w", 64), col("k_c", 128), misc,
                         col("fox_q", 256)] + fox_k + [col("fox_v", 256), col("pool", 256), col("conv", 768)],
                        axis=-1)
    assert w.shape[-1] == _P_COLS and w.dtype == BF16
    return w


def _compress_weights(cmp_pos, cmp_w1, cmp_b1, cmp_w2, cmp_b2):
    half = CMP_LEN // 2

    def w1_half(lo):
        wk = cmp_w1[0, lo * HEAD_DIM:(lo + half) * HEAD_DIM].reshape(half, HEAD_DIM, CMP_HIDDEN)
        wv = cmp_w1[1, lo * HEAD_DIM:(lo + half) * HEAD_DIM].reshape(half, HEAD_DIM, CMP_HIDDEN)
        z = jnp.zeros_like(wk)
        top = jnp.concatenate([wk, z], axis=2)
        bot = jnp.concatenate([z, wv], axis=2)
        return jnp.concatenate([top, bot], axis=1).astype(BF16)

    def pos_half(lo):
        return jnp.concatenate([cmp_pos[0, lo:lo + half], cmp_pos[1, lo:lo + half]], axis=1)

    lane = np.arange(HEAD_DIM)
    perm = np.where(lane < ROPE_HALF, lane + ROPE_HALF, np.where(lane < 2 * ROPE_HALF, lane - ROPE_HALF, lane))
    w2k, w2v = cmp_w2[0], cmp_w2[1]
    zk = jnp.zeros_like(w2k)
    w2 = jnp.concatenate([
        jnp.concatenate([w2k, w2k, w2k[:, perm], w2k[:, perm], zk, zk], axis=1),
        jnp.concatenate([zk, zk, zk, zk, w2v, w2v], axis=1)], axis=0).astype(BF16)
    b2k, b2v = cmp_b2[0], cmp_b2[1]
    b2 = jnp.concatenate([b2k, b2k, b2k[perm], b2k[perm], b2v, b2v])[None, :]
    b1 = jnp.concatenate([cmp_b1[0], cmp_b1[1]])[None, :]
    return pos_half(0), pos_half(half), w1_half(0), w1_half(half), b1, w2, b2


def _block_diag(pool_w):
    n, c, _ = pool_w.shape
    out = jnp.zeros((n * c, n * c), pool_w.dtype)
    for i in range(n):
        out = out.at[i * c:(i + 1) * c, i * c:(i + 1) * c].set(pool_w[i])
    return out


def kernel(x, norm_g, w_in, fox_f_bias, cmp_pos, cmp_w1, cmp_b1, cmp_w2, cmp_b2, pool_w, pool_scale, conv_w,
           w_branch, w_out, final_norm_g):
    B, T, _ = x.shape
    depth = norm_g.shape[0]
    assert T % TM_PROJ == 0 and T % KV_TILE == 0 and T // SEL_LEN <= HEAD_DIM and T >= WINDOW + CHUNK
    assert NSA_SUB * CHUNK == KV_TILE == TQ_FOX and TM_PROJ % KV_TILE == 0
    rows = T // CMP_STRIDE
    nblk = T // SEL_LEN

    cc_tok, ss_tok = _rope_tables(jnp.arange(T))
    cmp_end = jnp.arange(rows) * CMP_STRIDE + CMP_LEN - 1
    cc_cmp, ss_cmp = _rope_tables(cmp_end)
    ci = np.arange(rows)[:, None] * CMP_STRIDE
    sj = np.arange(nblk)[None, :] * SEL_LEN
    overlap_t = jnp.asarray(((ci < sj + SEL_LEN) & (ci + CMP_LEN > sj)).T, F32)
    tri = jnp.asarray(np.triu(np.ones((TM_PROJ, TM_PROJ), np.float32)), BF16)

    w_in_bf = w_in.astype(BF16)
    w_proj = _proj_weight(w_in_bf)
    w_gate = w_in_bf[..., _OFF["gate"]:_OFF["merge"]]
    w_merge = w_in_bf[..., _OFF["merge"]:_OFF["end"]]
    for l in range(depth):
        w1 = w_proj[l]
        fb = jnp.zeros((8, 1), F32).at[4:8, 0].set(fox_f_bias[l])
        (q, ks, kw, vst, vwt, kvc, mt, fq, fk, fvt, o_pool, o_conv) = _proj_call(
            x, norm_g[l][None, :], w1, cc_tok, ss_tok, fb, conv_w[l], _block_diag(pool_w[l]).astype(BF16),
            pool_scale[l][None, :], tri)
        kc, vct = _compress_call(kvc, *_compress_weights(cmp_pos[l], cmp_w1[l], cmp_b1[l], cmp_w2[l], cmp_b2[l]),
                                 cc_cmp, ss_cmp)
        o_nsa = _nsa_call(q, ks, kw, vst, vwt, kc, vct, mt, overlap_t)
        o_fox = _fox_call(fq, fk, fvt)
        x = _merge_call(x, norm_g[l][None, :], w_gate[l], w_merge[l], w_branch[l].astype(BF16), w_out[l].astype(BF16),
                        final_norm_g[None, :], o_nsa, o_pool, o_conv, o_fox, last=(l == depth - 1))
    return x
```
